```python
import math
import jax, jax.numpy as jnp
from jax import lax
import numpy as np

D_MODEL = 1024
BATCH = 16
SEQ = 4096
DEPTH = 4

GRID_W = 64
CTX_LEN = 256
N_EVEN = (DEPTH + 1) // 2
N_ODD = DEPTH // 2
EPS = 1e-6
ADA_CHUNKS = 6

H_A = 8
Q_LORA = 256
KV_LORA = 128
NOPE_DIM = 64
ROPE_DIM = 32
QK_DIM = NOPE_DIM + ROPE_DIM
V_DIM = 64
ROPE_BASE = 10000.0
Q_BLOCK = 128

H_R = 8
DK_R = 64
DV_R = 64
RET_CHUNK = 128
RET_W = H_R * DK_R

MLA_IN = Q_LORA + KV_LORA + ROPE_DIM
IN_WIDTH = MLA_IN + 4 * RET_W
MIX_W = H_A * V_DIM + H_R * DV_R

S5_GROUP = 16
S5_GROUPS = D_MODEL // S5_GROUP
S5_STATE = 64
S5_CHUNK = 128
DT_MIN = 1e-3
DT_MAX = 1e-1

D_FF = 4 * D_MODEL

kernel_name = 'hybrid_mla_retention_s5_dit'


def rmsnorm(x, g):
    xf = x.astype(jnp.float32)
    y = xf * lax.rsqrt(jnp.mean(xf * xf, axis=-1, keepdims=True) + EPS)
    return (y * g.astype(jnp.float32)).astype(x.dtype)


def head_norm(o):
    mu = jnp.mean(o, axis=-1, keepdims=True)
    var = jnp.mean(jnp.square(o - mu), axis=-1, keepdims=True)
    return (o - mu) * lax.rsqrt(var + EPS)


def modulate(h, shift, scale):
    return h * (1 + scale) + shift


def apply_rotary(x, ang):
    half = x.shape[-1] // 2
    cos = jnp.cos(ang)[None, :, None, :].astype(x.dtype)
    sin = jnp.sin(ang)[None, :, None, :].astype(x.dtype)
    x1, x2 = x[..., :half], x[..., half:]
    return jnp.concatenate([x1 * cos - x2 * sin, x1 * sin + x2 * cos], axis=-1)


def axial_angles(n_tok):
    rows = n_tok // GRID_W
    r = jnp.repeat(jnp.arange(rows, dtype=jnp.float32), GRID_W)
    col = jnp.tile(jnp.arange(GRID_W, dtype=jnp.float32), rows)
    nf = ROPE_DIM // 4
    f = ROPE_BASE ** (-jnp.arange(nf, dtype=jnp.float32) / nf)
    return jnp.concatenate([r[:, None] * f, col[:, None] * f], axis=-1)


def retnet_angles(n_tok):
    nf = DK_R // 2
    theta = ROPE_BASE ** (-jnp.arange(nf, dtype=jnp.float32) / nf)
    return jnp.arange(n_tok, dtype=jnp.float32)[:, None] * theta


def squared_relu_mlp(h, w1, w2):
    return jnp.square(jax.nn.relu(h @ w1)) @ w2


def mla_heads(z, q_norm_g, w_uq, kv_norm_g, w_ukv, qn_g, kn_g, ang):
    b, t = z.shape[:2]
    cq = z[..., :Q_LORA]
    ckv = z[..., Q_LORA:Q_LORA + KV_LORA]
    kr = z[..., Q_LORA + KV_LORA:]
    q = (rmsnorm(cq, q_norm_g) @ w_uq).reshape(b, t, H_A, QK_DIM)
    kv = (rmsnorm(ckv, kv_norm_g) @ w_ukv).reshape(b, t, H_A, NOPE_DIM + V_DIM)
    k = jnp.concatenate([kv[..., :NOPE_DIM], jnp.broadcast_to(kr[:, :, None, :], (b, t, H_A, ROPE_DIM))], axis=-1)
    v = kv[..., NOPE_DIM:]
    q = rmsnorm(q, qn_g)
    k = rmsnorm(k, kn_g)
    if ang is not None:
        q = jnp.concatenate([q[..., :NOPE_DIM], apply_rotary(q[..., NOPE_DIM:], ang)], axis=-1)
        k = jnp.concatenate([k[..., :NOPE_DIM], apply_rotary(k[..., NOPE_DIM:], ang)], axis=-1)
    return q, k, v


def attend(q, k, v):
    s = jnp.einsum('bqhd,bkhd->bhqk', q, k).astype(jnp.float32) * (QK_DIM ** -0.5)
    p = jax.nn.softmax(s, axis=-1).astype(v.dtype)
    return jnp.einsum('bhqk,bkhd->bqhd', p, v)


def blocked_attend(q, k, v):
    b, n_tok = q.shape[:2]
    nb = n_tok // Q_BLOCK
    qb = q.reshape(b, nb, Q_BLOCK, H_A, QK_DIM).swapaxes(0, 1)
    o = lax.map(lambda qq: attend(qq, k, v), qb)
    return o.swapaxes(0, 1).reshape(b, n_tok, H_A * V_DIM)


def ret_heads(z, ang):
    b, t = z.shape[:2]
    rq, rk, rv, rg = jnp.split(z, 4, axis=-1)
    rq = rq.reshape(b, t, H_R, DK_R).astype(jnp.float32)
    rk = rk.reshape(b, t, H_R, DK_R).astype(jnp.float32)
    rv = rv.reshape(b, t, H_R, DV_R).astype(jnp.float32)
    if ang is not None:
        rq = apply_rotary(rq, ang)
        rk = apply_rotary(rk, ang)
    return rq, rk * (DK_R ** -0.5), rv, rg


def retention_scan(q, k, v, log_gamma, s0, strict):
    b, t, h, _ = q.shape
    dv = v.shape[-1]
    n = t // RET_CHUNK
    i = jnp.arange(RET_CHUNK, dtype=jnp.float32)
    diff = i[:, None] - i[None, :]
    mask = (diff > 0) if strict else (diff >= 0)
    intra_decay = jnp.where(mask[None], jnp.exp(jnp.where(mask, diff, 0.0)[None] * log_gamma[:, None, None]), 0.0)
    q_decay = jnp.exp((i + 1)[:, None] * log_gamma[None])
    k_decay = jnp.exp((RET_CHUNK - 1 - i)[:, None] * log_gamma[None])
    chunk_decay = jnp.exp(RET_CHUNK * log_gamma)

    def blocks(a):
        return a.reshape(b, n, RET_CHUNK, h, a.shape[-1]).swapaxes(0, 1)

    def step(s, qkv):
        qc, kc, vc = qkv
        scores = jnp.einsum('bihd,bjhd->bhij', qc, kc) * intra_decay
        o = jnp.einsum('bhij,bjhe->bihe', scores, vc) + jnp.einsum('bihd,bhde->bihe', qc, s) * q_decay[None, :, :, None]
        s = s * chunk_decay[None, :, None, None] + jnp.einsum('bjhd,bjhe->bhde', kc * k_decay[None, :, :, None], vc)
        return s, o

    s_final, o = lax.scan(step, s0, (blocks(q), blocks(k), blocks(v)))
    return o.swapaxes(0, 1).reshape(b, t, h, dv), s_final


def ret_out(o, g):
    b, t = o.shape[:2]
    return head_norm(o).reshape(b, t, H_R * DV_R).astype(g.dtype) * jax.nn.silu(g)


def mla_retention_mixer(h_c, h_l, w_in, q_norm_g, w_uq, kv_norm_g, w_ukv, qn_g, kn_g, lg_f_raw, lg_b_raw, w_out, need_ctx):
    b, n_lat = h_l.shape[:2]
    z_c = h_c @ w_in
    z_l = h_l @ w_in
    q_c, k_c, v_c = mla_heads(z_c[..., :MLA_IN], q_norm_g, w_uq, kv_norm_g, w_ukv, qn_g, kn_g, None)
    q_l, k_l, v_l = mla_heads(z_l[..., :MLA_IN], q_norm_g, w_uq, kv_norm_g, w_ukv, qn_g, kn_g, axial_angles(n_lat))
    k_all = jnp.concatenate([k_c, k_l], axis=1)
    v_all = jnp.concatenate([v_c, v_l], axis=1)
    a_l = blocked_attend(q_l, k_all, v_all)
    lg_f = jnp.log1p(-jnp.exp2(lg_f_raw.astype(jnp.float32)))
    lg_b = jnp.log1p(-jnp.exp2(lg_b_raw.astype(jnp.float32)))
    rq_c, rk_c, rv_c, rg_c = ret_heads(z_c[..., MLA_IN:], None)
    rq_l, rk_l, rv_l, rg_l = ret_heads(z_l[..., MLA_IN:], retnet_angles(n_lat))
    zero = jnp.zeros((b, H_R, DK_R, DV_R), jnp.float32)
    o_cf, s_cf = retention_scan(rq_c, rk_c, rv_c, lg_f, zero, False)
    o_cb, s_cb = retention_scan(rq_c[:, ::-1], rk_c[:, ::-1], rv_c[:, ::-1], lg_b, zero, True)
    o_lf, _ = retention_scan(rq_l, rk_l, rv_l, lg_f, s_cf, False)
    o_lb, _ = retention_scan(rq_l[:, ::-1], rk_l[:, ::-1], rv_l[:, ::-1], lg_b, s_cb, True)
    r_l = ret_out(o_lf + o_lb[:, ::-1], rg_l)
    y_l = jnp.concatenate([a_l, r_l.astype(a_l.dtype)], axis=-1) @ w_out
    if need_ctx:
        a_c = attend(q_c, k_c, v_c).reshape(b, h_c.shape[1], H_A * V_DIM)
        r_c = ret_out(o_cf + o_cb[:, ::-1], rg_c)
        y_c = jnp.concatenate([a_c, r_c.astype(a_c.dtype)], axis=-1) @ w_out
    else:
        y_c = None
    return y_c, y_l


def s5_discretize(a_re, a_im, b_re, b_im, c_re, c_im, log_dt):
    f = lambda t: t.astype(jnp.float32)
    a = lax.complex(f(a_re), f(a_im))
    dt = jnp.exp(f(log_dt))[:, None]
    a_bar = jnp.exp(dt * a)
    b_bar = ((a_bar - 1) / a)[..., None] * lax.complex(f(b_re), f(b_im))
    c_mat = lax.complex(f(c_re), f(c_im))
    return a_bar, b_bar, c_mat


def _affine_compose(e1, e2):
    a1, b1 = e1
    a2, b2 = e2
    return a1 * a2, a2 * b1 + b2


def s5_scan(u, a_bar, b_bar, c_mat, x0):
    b, t = u.shape[:2]
    n = t // S5_CHUNK
    ub = u.reshape(b, n, S5_CHUNK, S5_GROUPS, S5_GROUP).swapaxes(0, 1)

    def step(x, u_blk):
        bu = jnp.einsum('btgh,gph->btgp', u_blk.astype(jnp.complex64), b_bar)
        bu = bu.at[:, 0].add(a_bar * x)
        a = jnp.broadcast_to(a_bar, bu.shape)
        _, xs = lax.associative_scan(_affine_compose, (a, bu), axis=1)
        y = jnp.einsum('btgp,ghp->btgh', xs, c_mat).real
        return xs[:, -1], y

    x_final, ys = lax.scan(step, x0, ub)
    return ys.swapaxes(0, 1).reshape(b, t, S5_GROUPS, S5_GROUP), x_final


def s5_out(y, u, d_skip, w_glu, dtype):
    b, t = u.shape[:2]
    y = (y + d_skip.astype(jnp.float32).reshape(S5_GROUPS, S5_GROUP) * u).reshape(b, t, D_MODEL)
    z = jax.nn.gelu(y).astype(dtype) @ w_glu
    za, zb = jnp.split(z, 2, axis=-1)
    return za * jax.nn.sigmoid(zb)


def s5_mixer(h_c, h_l, p_f, p_b, d_skip, w_glu, need_ctx):
    ab_f, bb_f, c_f = s5_discretize(*p_f)
    ab_b, bb_b, c_b = s5_discretize(*p_b)
    b = h_l.shape[0]
    u_c = h_c.astype(jnp.float32).reshape(b, h_c.shape[1], S5_GROUPS, S5_GROUP)
    u_l = h_l.astype(jnp.float32).reshape(b, h_l.shape[1], S5_GROUPS, S5_GROUP)
    x0 = jnp.zeros((b, S5_GROUPS, S5_STATE), jnp.complex64)
    y_cf, x_cf = s5_scan(u_c, ab_f, bb_f, c_f, x0)
    y_cb, x_cb = s5_scan(u_c[:, ::-1], ab_b, bb_b, c_b, x0)
    y_lf, _ = s5_scan(u_l, ab_f, bb_f, c_f, x_cf)
    y_lb, _ = s5_scan(u_l[:, ::-1], ab_b, bb_b, c_b, x_cb)
    out_l = s5_out(y_lf + y_lb[:, ::-1], u_l, d_skip, w_glu, h_l.dtype)
    out_c = s5_out(y_cf + y_cb[:, ::-1], u_c, d_skip, w_glu, h_c.dtype) if need_ctx else None
    return out_c, out_l


def setup_inputs(seed: int = 0) -> dict:
    key = jax.random.key(seed)
    ks = iter(jax.random.split(key, 48))
    f32 = jnp.float32

    def nrm(shape, scale=1.0):
        return jax.random.normal(next(ks), shape, f32) * scale

    def gain(shape):
        return 1.0 + nrm(shape, 0.02)

    D = D_MODEL
    G, P, Hg = S5_GROUPS, S5_STATE, S5_GROUP
    n_idx = jnp.arange(P, dtype=f32)
    decay_init = -(5.0 + jnp.arange(H_R, dtype=f32))
    inputs = {
        'x': nrm((BATCH, SEQ, D)),
        'c': nrm((BATCH, D)),
        'ctx': nrm((BATCH, CTX_LEN, D)),
        'c_ctx': nrm((D,)),
        'ada_w': nrm((DEPTH, D, ADA_CHUNKS * D), 0.5 * D ** -0.5),
        'ada_b': nrm((DEPTH, ADA_CHUNKS * D), 0.02),
        'norm1_g': gain((DEPTH, D)),
        'norm2_g': gain((DEPTH, D)),
        'mlp_w1': nrm((DEPTH, D, D_FF), D ** -0.5),
        'mlp_w2': nrm((DEPTH, D_FF, D), D_FF ** -0.5),
        'w_in': nrm((N_EVEN, D, IN_WIDTH), D ** -0.5),
        'mla_q_norm_g': gain((N_EVEN, Q_LORA)),
        'mla_w_uq': nrm((N_EVEN, Q_LORA, H_A * QK_DIM), Q_LORA ** -0.5),
        'mla_kv_norm_g': gain((N_EVEN, KV_LORA)),
        'mla_w_ukv': nrm((N_EVEN, KV_LORA, H_A * (NOPE_DIM + V_DIM)), KV_LORA ** -0.5),
        'mla_qn_g': gain((N_EVEN, QK_DIM)),
        'mla_kn_g': gain((N_EVEN, QK_DIM)),
        'ret_lg_f': decay_init + nrm((N_EVEN, H_R), 0.1),
        'ret_lg_b': decay_init + nrm((N_EVEN, H_R), 0.1),
        'w_out': nrm((N_EVEN, MIX_W, D), MIX_W ** -0.5),
    }
    for d in ('f', 'b'):
        inputs['s5_a_re_' + d] = -0.5 + nrm((N_ODD, G, P), 0.01)
        inputs['s5_a_im_' + d] = jnp.pi * n_idx + nrm((N_ODD, G, P), 0.01)
        inputs['s5_b_re_' + d] = nrm((N_ODD, G, P, Hg), (2 * Hg) ** -0.5)
        inputs['s5_b_im_' + d] = nrm((N_ODD, G, P, Hg), (2 * Hg) ** -0.5)
        inputs['s5_c_re_' + d] = nrm((N_ODD, G, Hg, P), P ** -0.5)
        inputs['s5_c_im_' + d] = nrm((N_ODD, G, Hg, P), P ** -0.5)
        inputs['s5_log_dt_' + d] = math.log(DT_MIN) + jax.random.uniform(next(ks), (N_ODD, G), f32) * (math.log(DT_MAX) - math.log(DT_MIN))
    inputs['s5_d'] = nrm((N_ODD, D))
    inputs['s5_w_glu'] = nrm((N_ODD, D, 2 * D), D ** -0.5)
    return inputs


def reference(x, c, ctx, c_ctx, ada_w, ada_b, norm1_g, norm2_g, mlp_w1, mlp_w2, w_in, mla_q_norm_g, mla_w_uq, mla_kv_norm_g, mla_w_ukv, mla_qn_g, mla_kn_g, ret_lg_f, ret_lg_b, w_out, s5_a_re_f, s5_a_im_f, s5_b_re_f, s5_b_im_f, s5_c_re_f, s5_c_im_f, s5_log_dt_f, s5_a_re_b, s5_a_im_b, s5_b_re_b, s5_b_im_b, s5_c_re_b, s5_c_im_b, s5_log_dt_b, s5_d, s5_w_glu):
    xc = ctx
    b = x.shape[0]
    for l in range(DEPTH):
        need_ctx = l < DEPTH - 1
        mod_l = (jax.nn.silu(c) @ ada_w[l] + ada_b[l]).reshape(b, ADA_CHUNKS, 1, D_MODEL)
        mod_c = (jax.nn.silu(c_ctx) @ ada_w[l] + ada_b[l]).reshape(ADA_CHUNKS, D_MODEL)
        h_l = modulate(rmsnorm(x, norm1_g[l]), mod_l[:, 0], mod_l[:, 1])
        h_c = modulate(rmsnorm(xc, norm1_g[l]), mod_c[0], mod_c[1])
        if l % 2 == 0:
            e = l // 2
            o_c, o_l = mla_retention_mixer(h_c, h_l, w_in[e], mla_q_norm_g[e], mla_w_uq[e], mla_kv_norm_g[e], mla_w_ukv[e], mla_qn_g[e], mla_kn_g[e], ret_lg_f[e], ret_lg_b[e], w_out[e], need_ctx)
        else:
            o = l // 2
            p_f = (s5_a_re_f[o], s5_a_im_f[o], s5_b_re_f[o], s5_b_im_f[o], s5_c_re_f[o], s5_c_im_f[o], s5_log_dt_f[o])
            p_b = (s5_a_re_b[o], s5_a_im_b[o], s5_b_re_b[o], s5_b_im_b[o], s5_c_re_b[o], s5_c_im_b[o], s5_log_dt_b[o])
            o_c, o_l = s5_mixer(h_c, h_l, p_f, p_b, s5_d[o], s5_w_glu[o], need_ctx)
        x = x + mod_l[:, 2] * o_l
        h_l = modulate(rmsnorm(x, norm2_g[l]), mod_l[:, 3], mod_l[:, 4])
        x = x + mod_l[:, 5] * squared_relu_mlp(h_l, mlp_w1[l], mlp_w2[l])
        if need_ctx:
            xc = xc + mod_c[2] * o_c
            h_c = modulate(rmsnorm(xc, norm2_g[l]), mod_c[3], mod_c[4])
            xc = xc + mod_c[5] * squared_relu_mlp(h_c, mlp_w1[l], mlp_w2[l])
    return x
```

```python
import functools
import math

import jax
import jax.numpy as jnp
from jax import lax
from jax.experimental import pallas as pl
from jax.experimental.pallas import tpu as pltpu

F32 = jnp.float32
BF16 = jnp.bfloat16

D_MODEL = 1024
EPS = 1e-6
ADA_CHUNKS = 6
GRID_W = 64
ROPE_BASE = 10000.0

H_A = 8
Q_LORA = 256
KV_LORA = 128
NOPE_DIM = 64
ROPE_DIM = 32
QK_DIM = NOPE_DIM + ROPE_DIM
V_DIM = 64
HEAD_SLOT = 128
MLA_IN = Q_LORA + KV_LORA + ROPE_DIM
MLA_IN_PAD = 512

H_R = 8
DK_R = 64
DV_R = 64
RET_W = H_R * DK_R
N_PAIR = H_R // 2

S5_GROUP = 16
S5_GROUPS = D_MODEL // S5_GROUP
S5_STATE = 64
S5_L = 16
S5_K = S5_L * S5_GROUP

D_FF = 4 * D_MODEL
Z_WIDTH = MLA_IN_PAD + 4 * RET_W

VMEM_LIMIT = 56 * 1024 * 1024

_NT = (((1,), (1,)), ((), ()))


def _params(n_grid):
    return pltpu.CompilerParams(
        dimension_semantics=("arbitrary",) * n_grid, vmem_limit_bytes=VMEM_LIMIT)


def _sigmoid(x):
    return 1.0 / (1.0 + jnp.exp(-x))


def _pick_tile(t, pref):
    tile = min(t, pref)
    while t % tile:
        tile //= 2
    return tile


def _ada_kernel(c_ref, w_ref, b_ref, o_ref):
    cc = c_ref[...]
    s = cc * _sigmoid(cc)
    o_ref[0] = jnp.dot(s, w_ref[0], preferred_element_type=F32,
                       precision=lax.Precision.HIGHEST) + b_ref[0]


def _ada_all(cc, ada_w, ada_b):
    depth, _, width = ada_w.shape
    r = cc.shape[0]
    tn = 1536
    return pl.pallas_call(
        _ada_kernel,
        grid=(depth, width // tn),
        in_specs=[
            pl.BlockSpec((r, D_MODEL), lambda l, j: (0, 0)),
            pl.BlockSpec((1, D_MODEL, tn), lambda l, j: (l, 0, j)),
            pl.BlockSpec((1, 1, tn), lambda l, j: (l, 0, j)),
        ],
        out_specs=pl.BlockSpec((1, r, tn), lambda l, j: (l, 0, j)),
        out_shape=jax.ShapeDtypeStruct((depth, r, width), F32),
        compiler_params=_params(2),
    )(cc, ada_w, ada_b.reshape(depth, 1, width))


def _norm_mod(x, g_ref, mod_ref, shift_row, scale_row):
    ms = jnp.mean(x * x, axis=-1, keepdims=True)
    y = x * lax.rsqrt(ms + EPS) * g_ref[...]
    return y * (1.0 + mod_ref[0, scale_row:scale_row + 1, :]) + mod_ref[0, shift_row:shift_row + 1, :]


def _nm_kernel(x_ref, g_ref, mod_ref, o_ref, *, shift_row, scale_row):
    o_ref[...] = _norm_mod(x_ref[...], g_ref, mod_ref, shift_row, scale_row).astype(o_ref.dtype)


def _nmm_kernel(x_ref, g_ref, mod_ref, w_ref, o_ref, *, shift_row, scale_row):
    h = _norm_mod(x_ref[...], g_ref, mod_ref, shift_row, scale_row).astype(BF16)
    o_ref[...] = jnp.dot(h, w_ref[...], preferred_element_type=F32).astype(o_ref.dtype)


def _norm_mod_call(x2, g, mod, t, w=None, shift_row=0, scale_row=1):
    n = x2.shape[0]
    tm = _pick_tile(t, 512)
    tpb = t // tm
    in_specs = [
        pl.BlockSpec((tm, D_MODEL), lambda i: (i, 0)),
        pl.BlockSpec((1, D_MODEL), lambda i: (0, 0)),
        pl.BlockSpec((1, ADA_CHUNKS, D_MODEL), lambda i: (i // tpb, 0, 0)),
    ]
    args = [x2, g.reshape(1, D_MODEL), mod]
    if w is None:
        body = functools.partial(_nm_kernel, shift_row=shift_row, scale_row=scale_row)
        n_out = D_MODEL
    else:
        body = functools.partial(_nmm_kernel, shift_row=shift_row, scale_row=scale_row)
        n_out = w.shape[1]
        in_specs.append(pl.BlockSpec((D_MODEL, n_out), lambda i: (0, 0)))
        args.append(w)
    return pl.pallas_call(
        body,
        grid=(n // tm,),
        in_specs=in_specs,
        out_specs=pl.BlockSpec((tm, n_out), lambda i: (i, 0)),
        out_shape=jax.ShapeDtypeStruct((n, n_out), BF16),
        compiler_params=_params(1),
    )(*args)


def _mla_prep_kernel(z_ref, gq_ref, gkv_ref, wq_ref, wqs_ref, wkc_ref, wkcs_ref, wkr_ref, wkrs_ref,
                     wv_ref, gqn_ref, gqns_ref, gkn_ref, gkns_ref, cos_ref, sin_ref,
                     q_ref, k_ref, v_ref):
    cq = z_ref[:, 0:Q_LORA].astype(F32)
    ckv = z_ref[:, Q_LORA:Q_LORA + KV_LORA].astype(F32)
    kr = z_ref[:, Q_LORA + KV_LORA:MLA_IN_PAD]
    cqn = (cq * lax.rsqrt(jnp.mean(cq * cq, axis=-1, keepdims=True) + EPS) * gq_ref[...]).astype(BF16)
    ckn = (ckv * lax.rsqrt(jnp.mean(ckv * ckv, axis=-1, keepdims=True) + EPS) * gkv_ref[...]).astype(BF16)
    q = jnp.dot(cqn, wq_ref[...], preferred_element_type=F32)
    qs = jnp.dot(cqn, wqs_ref[...], preferred_element_type=F32)
    k = jnp.dot(ckn, wkc_ref[...], preferred_element_type=F32) + jnp.dot(kr, wkr_ref[...], preferred_element_type=F32)
    ks = jnp.dot(ckn, wkcs_ref[...], preferred_element_type=F32) + jnp.dot(kr, wkrs_ref[...], preferred_element_type=F32)
    v_ref[...] = jnp.dot(ckn, wv_ref[...], preferred_element_type=F32).astype(v_ref.dtype)
    cos = cos_ref[...]
    sin = sin_ref[...]
    q_scale = QK_DIM ** -0.5
    for h in range(H_A):
        sl = slice(h * HEAD_SLOT, (h + 1) * HEAD_SLOT)
        qh = q[:, sl]
        rq = lax.rsqrt(jnp.sum(qh * qh, axis=-1, keepdims=True) * (1.0 / QK_DIM) + EPS)
        q_rot = (qh * gqn_ref[...] * cos + qs[:, sl] * gqns_ref[...] * sin) * (rq * q_scale)
        q_ref[:, sl] = q_rot.astype(q_ref.dtype)
        kh = k[:, sl]
        rk = lax.rsqrt(jnp.sum(kh * kh, axis=-1, keepdims=True) * (1.0 / QK_DIM) + EPS)
        k_rot = (kh * gkn_ref[...] * cos + ks[:, sl] * gkns_ref[...] * sin) * rk
        k_ref[:, sl] = k_rot.astype(k_ref.dtype)


def _mla_prep_call(z, t, wts, cos_t, sin_t):
    n = z.shape[0]
    tm = _pick_tile(t, 512)
    tpb = t // tm
    full = lambda a: pl.BlockSpec(a.shape, lambda i: (0,) * a.ndim)
    in_specs = [pl.BlockSpec((tm, MLA_IN_PAD), lambda i: (i, 0))] + [full(a) for a in wts] + [
        pl.BlockSpec((tm, HEAD_SLOT), lambda i: (i % tpb, 0)),
        pl.BlockSpec((tm, HEAD_SLOT), lambda i: (i % tpb, 0)),
    ]
    hw = H_A * HEAD_SLOT
    return pl.pallas_call(
        _mla_prep_kernel,
        grid=(n // tm,),
        in_specs=in_specs,
        out_specs=[
            pl.BlockSpec((tm, hw), lambda i: (i, 0)),
            pl.BlockSpec((tm, hw), lambda i: (i, 0)),
            pl.BlockSpec((tm, H_A * V_DIM), lambda i: (i, 0)),
        ],
        out_shape=[
            jax.ShapeDtypeStruct((n, hw), BF16),
            jax.ShapeDtypeStruct((n, hw), BF16),
            jax.ShapeDtypeStruct((n, H_A * V_DIM), BF16),
        ],
        compiler_params=_params(1),
    )(z, *wts, cos_t, sin_t)


def _mla_weights(q_norm_g, w_uq, kv_norm_g, w_ukv, qn_g, kn_g):
    pad = HEAD_SLOT - QK_DIM
    half = ROPE_DIM // 2
    perm = jnp.arange(HEAD_SLOT)
    perm = perm.at[NOPE_DIM:NOPE_DIM + half].set(jnp.arange(NOPE_DIM + half, NOPE_DIM + ROPE_DIM))
    perm = perm.at[NOPE_DIM + half:NOPE_DIM + ROPE_DIM].set(jnp.arange(NOPE_DIM, NOPE_DIM + half))

    def slots(w):
        wp = jnp.pad(w, ((0, 0), (0, 0), (0, pad)))
        return wp, wp[:, :, perm]

    def flat(w):
        return w.reshape(w.shape[0], H_A * HEAD_SLOT).astype(BF16)

    wq, wqs = slots(w_uq.reshape(Q_LORA, H_A, QK_DIM))
    w_kv = w_ukv.reshape(KV_LORA, H_A, NOPE_DIM + V_DIM)
    wkc, wkcs = slots(jnp.pad(w_kv[:, :, :NOPE_DIM], ((0, 0), (0, 0), (0, ROPE_DIM))))
    eye = jnp.zeros((HEAD_SLOT, QK_DIM), F32).at[jnp.arange(ROPE_DIM), NOPE_DIM + jnp.arange(ROPE_DIM)].set(1.0)
    wkr, wkrs = slots(jnp.broadcast_to(eye[:, None, :], (HEAD_SLOT, H_A, QK_DIM)))
    wv = w_kv[:, :, NOPE_DIM:].reshape(KV_LORA, H_A * V_DIM).astype(BF16)

    def gains(g):
        gp = jnp.pad(g.astype(F32), (0, pad))
        return gp.reshape(1, HEAD_SLOT), gp[perm].reshape(1, HEAD_SLOT)

    gqn, gqns = gains(qn_g)
    gkn, gkns = gains(kn_g)
    return [q_norm_g.reshape(1, Q_LORA).astype(F32), kv_norm_g.reshape(1, KV_LORA).astype(F32),
            flat(wq), flat(wqs), flat(wkc), flat(wkcs), flat(wkr), flat(wkrs), wv, gqn, gqns, gkn, gkns]


def _mla_tables(n_tok, rotary):
    cos = jnp.ones((n_tok, HEAD_SLOT), F32)
    sin = jnp.zeros((n_tok, HEAD_SLOT), F32)
    if rotary:
        rows = n_tok // GRID_W
        r = jnp.repeat(jnp.arange(rows, dtype=F32), GRID_W)
        col = jnp.tile(jnp.arange(GRID_W, dtype=F32), rows)
        nf = ROPE_DIM // 4
        f = ROPE_BASE ** (-jnp.arange(nf, dtype=F32) / nf)
        ang = jnp.concatenate([r[:, None] * f, col[:, None] * f], axis=-1)
        c, s = jnp.cos(ang), jnp.sin(ang)
        half = ROPE_DIM // 2
        cos = cos.at[:, NOPE_DIM:NOPE_DIM + half].set(c).at[:, NOPE_DIM + half:NOPE_DIM + ROPE_DIM].set(c)
        sin = sin.at[:, NOPE_DIM:NOPE_DIM + half].set(-s).at[:, NOPE_DIM + half:NOPE_DIM + ROPE_DIM].set(s)
    return cos, sin


def _attn_kernel(*refs, nseg):
    q_ref = refs[0]
    k_refs = refs[1:1 + nseg]
    v_refs = refs[1 + nseg:1 + 2 * nseg]
    o_ref = refs[1 + 2 * nseg]
    lane = lax.broadcasted_iota(jnp.int32, (1, HEAD_SLOT), 1)
    lo = lane < V_DIM
    for j in range(H_A // 2):
        vsl = slice(j * HEAD_SLOT, (j + 1) * HEAD_SLOT)
        acc = None
        for e in range(2):
            h = 2 * j + e
            sl = slice(h * HEAD_SLOT, (h + 1) * HEAD_SLOT)
            qh = q_ref[:, sl]
            ss = [lax.dot_general(qh, k[:, sl], _NT, preferred_element_type=F32) for k in k_refs]
            m = functools.reduce(jnp.maximum, [jnp.max(s, axis=-1, keepdims=True) for s in ss])
            ps = [jnp.exp(s - m) for s in ss]
            l = functools.reduce(lambda a, b: a + b, [jnp.sum(p, axis=-1, keepdims=True) for p in ps])
            mask = lo if e == 0 else jnp.logical_not(lo)
            o = None
            for p, v in zip(ps, v_refs):
                vm = jnp.where(mask, v[:, vsl], jnp.zeros((), BF16))
                t = jnp.dot(p.astype(BF16), vm, preferred_element_type=F32)
                o = t if o is None else o + t
            o = o * (1.0 / l)
            acc = o if acc is None else acc + o
        o_ref[:, vsl] = acc.astype(o_ref.dtype)


def _attn_call(q, ks, vs, b, t, tks):
    n = q.shape[0]
    tq = _pick_tile(t, 256)
    tpb = t // tq
    hw = H_A * HEAD_SLOT
    vw = H_A * V_DIM
    nseg = len(ks)
    in_specs = [pl.BlockSpec((tq, hw), lambda i: (i, 0))]
    in_specs += [pl.BlockSpec((tk, hw), lambda i: (i // tpb, 0)) for tk in tks]
    in_specs += [pl.BlockSpec((tk, vw), lambda i: (i // tpb, 0)) for tk in tks]
    return pl.pallas_call(
        functools.partial(_attn_kernel, nseg=nseg),
        grid=(n // tq,),
        in_specs=in_specs,
        out_specs=pl.BlockSpec((tq, vw), lambda i: (i, 0)),
        out_shape=jax.ShapeDtypeStruct((n, vw), BF16),
        compiler_params=_params(1),
    )(q, *ks, *vs)


def _rotate_pairs(x, cos, sin_lo, sin_hi):
    return x * cos + pltpu.roll(x, 128 - DK_R // 2, 1) * sin_lo + pltpu.roll(x, DK_R // 2, 1) * sin_hi


def _ret_kernel(lg_ref, qf_ref, kf_ref, vf_ref, cf_ref, slf_ref, shf_ref,
                qb_ref, kb_ref, vb_ref, cb_ref, slb_ref, shb_ref, s0f_ref, s0b_ref,
                of_ref, ob_ref, sff_ref, sfb_ref, df_s, db_s, sf_s, sb_s, *, chunk):
    b = pl.program_id(0)
    c = pl.program_id(1)
    nc = pl.num_programs(1)

    @pl.when(jnp.logical_and(b == 0, c == 0))
    def _():
        ii = lax.broadcasted_iota(jnp.int32, (chunk, chunk), 0)
        jj = lax.broadcasted_iota(jnp.int32, (chunk, chunk), 1)
        diff = (ii - jj).astype(F32)
        for h in range(H_R):
            df_s[h] = jnp.where(ii >= jj, jnp.exp(jnp.where(ii >= jj, diff, 0.0) * lg_ref[0, h]), 0.0)
            db_s[h] = jnp.where(jj > ii, jnp.exp(jnp.where(jj > ii, -diff, 0.0) * lg_ref[1, h]), 0.0)

    @pl.when(c == 0)
    def _():
        sf_s[...] = s0f_ref[0]
        sb_s[...] = s0b_ref[0]

    lane = lax.broadcasted_iota(jnp.int32, (1, 128), 1)
    lo = lane < DK_R
    row = lax.broadcasted_iota(jnp.int32, (128, 128), 0)
    colm = lax.broadcasted_iota(jnp.int32, (128, 128), 1)
    blockdiag = (row < DK_R) == (colm < DK_R)
    pos = lax.broadcasted_iota(jnp.int32, (chunk, 1), 0).astype(F32)

    def one_direction(d, q_ref, k_ref, v_ref, cos_ref, sl_ref, sh_ref, d_s, s_s, o_ref):
        cos, sin_lo, sin_hi = cos_ref[...], sl_ref[...], sh_ref[...]
        for j in range(N_PAIR):
            sl = slice(j * 128, (j + 1) * 128)
            lg = jnp.where(lo, lg_ref[d, 2 * j], lg_ref[d, 2 * j + 1])
            q2 = _rotate_pairs(q_ref[:, sl].astype(F32), cos, sin_lo, sin_hi)
            k2 = _rotate_pairs(k_ref[:, sl].astype(F32), cos, sin_lo, sin_hi) * (DK_R ** -0.5)
            v2 = v_ref[:, sl]
            if d == 0:
                q_dec = jnp.exp((pos + 1.0) * lg)
                k_dec = jnp.exp((chunk - 1.0 - pos) * lg)
            else:
                q_dec = jnp.exp((chunk - pos) * lg)
                k_dec = jnp.exp(pos * lg)
            c_dec = jnp.exp(chunk * lg)
            s2 = s_s[j]
            o = jnp.dot((q2 * q_dec).astype(BF16), s2.astype(BF16), preferred_element_type=F32)
            k2b = k2.astype(BF16)
            for e in range(2):
                mask = lo if e == 0 else jnp.logical_not(lo)
                qe = jnp.where(mask, q2, 0.0).astype(BF16)
                sc = lax.dot_general(qe, k2b, _NT, preferred_element_type=F32) * d_s[2 * j + e]
                ve = jnp.where(mask, v2, jnp.zeros((), BF16))
                o = o + jnp.dot(sc.astype(BF16), ve, preferred_element_type=F32)
            o_ref[:, sl] = o.astype(o_ref.dtype)
            kd_t = (k2 * k_dec).T.astype(BF16)
            upd = jnp.dot(kd_t, v2, preferred_element_type=F32)
            s_s[j] = s2 * c_dec + jnp.where(blockdiag, upd, 0.0)

    one_direction(0, qf_ref, kf_ref, vf_ref, cf_ref, slf_ref, shf_ref, df_s, sf_s, of_ref)
    one_direction(1, qb_ref, kb_ref, vb_ref, cb_ref, slb_ref, shb_ref, db_s, sb_s, ob_ref)

    @pl.when(c == nc - 1)
    def _():
        sff_ref[0] = sf_s[...]
        sfb_ref[0] = sb_s[...]


def _ret_call(z, lg, tables, s0f, s0b, b, t):
    n = z.shape[0]
    chunk = _pick_tile(t, 256)
    nc = t // chunk
    cos_t, sin_lo_t, sin_hi_t = tables
    col0 = MLA_IN_PAD // RET_W

    def zspec(part, rev):
        if rev:
            return pl.BlockSpec((chunk, RET_W), lambda bi, ci: (bi * nc + nc - 1 - ci, col0 + part))
        return pl.BlockSpec((chunk, RET_W), lambda bi, ci: (bi * nc + ci, col0 + part))

    def tspec(rev):
        if rev:
            return pl.BlockSpec((chunk, 128), lambda bi, ci: (nc - 1 - ci, 0))
        return pl.BlockSpec((chunk, 128), lambda bi, ci: (ci, 0))

    st_spec = pl.BlockSpec((1, N_PAIR, 128, 128), lambda bi, ci: (bi, 0, 0, 0))
    in_specs = [pl.BlockSpec(memory_space=pltpu.SMEM)]
    in_specs += [zspec(0, False), zspec(1, False), zspec(2, False), tspec(False), tspec(False), tspec(False)]
    in_specs += [zspec(0, True), zspec(1, True), zspec(2, True), tspec(True), tspec(True), tspec(True)]
    in_specs += [st_spec, st_spec]
    out_specs = [
        pl.BlockSpec((chunk, RET_W), lambda bi, ci: (bi * nc + ci, 0)),
        pl.BlockSpec((chunk, RET_W), lambda bi, ci: (bi * nc + nc - 1 - ci, 0)),
        st_spec, st_spec,
    ]
    st_shape = jax.ShapeDtypeStruct((b, N_PAIR, 128, 128), F32)
    return pl.pallas_call(
        functools.partial(_ret_kernel, chunk=chunk),
        grid=(b, nc),
        in_specs=in_specs,
        out_specs=out_specs,
        out_shape=[jax.ShapeDtypeStruct((n, RET_W), F32), jax.ShapeDtypeStruct((n, RET_W), F32),
                   st_shape, st_shape],
        scratch_shapes=[
            pltpu.VMEM((H_R, chunk, chunk), F32), pltpu.VMEM((H_R, chunk, chunk), F32),
            pltpu.VMEM((N_PAIR, 128, 128), F32), pltpu.VMEM((N_PAIR, 128, 128), F32),
        ],
        compiler_params=_params(2),
    )(lg, z, z, z, cos_t, sin_lo_t, sin_hi_t, z, z, z, cos_t, sin_lo_t, sin_hi_t, s0f, s0b)


def _ret_tables(n_tok, rotary):
    cos = jnp.ones((n_tok, 128), F32)
    sin_lo = jnp.zeros((n_tok, 128), F32)
    sin_hi = jnp.zeros((n_tok, 128), F32)
    if rotary:
        nf = DK_R // 2
        theta = ROPE_BASE ** (-jnp.arange(nf, dtype=F32) / nf)
        ang = jnp.arange(n_tok, dtype=F32)[:, None] * theta
        c, s = jnp.cos(ang), jnp.sin(ang)
        z = jnp.zeros_like(s)
        cos = jnp.concatenate([c, c, c, c], axis=-1)
        sin_lo = jnp.concatenate([-s, z, -s, z], axis=-1)
        sin_hi = jnp.concatenate([z, s, z, s], axis=-1)
    return cos, sin_lo, sin_hi


def _mix_out_kernel(x_ref, a_ref, of_ref, ob_ref, g_ref, w_ref, mod_ref, o_ref):
    lane = lax.broadcasted_iota(jnp.int32, (1, 128), 1)
    lo = lane < DV_R
    y = jnp.dot(a_ref[...], w_ref[0:H_A * V_DIM, :], preferred_element_type=F32)
    for j in range(N_PAIR):
        sl = slice(j * 128, (j + 1) * 128)
        o = of_ref[:, sl] + ob_ref[:, sl]
        s_lo = jnp.sum(jnp.where(lo, o, 0.0), axis=-1, keepdims=True)
        s_all = jnp.sum(o, axis=-1, keepdims=True)
        mu = jnp.where(lo, s_lo, s_all - s_lo) * (1.0 / DV_R)
        oc = o - mu
        q = oc * oc
        q_lo = jnp.sum(jnp.where(lo, q, 0.0), axis=-1, keepdims=True)
        q_all = jnp.sum(q, axis=-1, keepdims=True)
        var = jnp.where(lo, q_lo, q_all - q_lo) * (1.0 / DV_R)
        g = g_ref[:, sl].astype(F32)
        r = (oc * lax.rsqrt(var + EPS)) * (g * _sigmoid(g))
        row0 = H_A * V_DIM + j * 128
        y = y + jnp.dot(r.astype(BF16), w_ref[row0:row0 + 128, :], preferred_element_type=F32)
    o_ref[...] = x_ref[...] + mod_ref[0, 2:3, :] * y


def _mix_out_call(x2, a, o_f, o_b, z, w_out, mod, t):
    n = x2.shape[0]
    tm = _pick_tile(t, 512)
    tpb = t // tm
    gate_col = MLA_IN_PAD // RET_W + 3
    return pl.pallas_call(
        _mix_out_kernel,
        grid=(n // tm,),
        in_specs=[
            pl.BlockSpec((tm, D_MODEL), lambda i: (i, 0)),
            pl.BlockSpec((tm, H_A * V_DIM), lambda i: (i, 0)),
            pl.BlockSpec((tm, RET_W), lambda i: (i, 0)),
            pl.BlockSpec((tm, RET_W), lambda i: (i, 0)),
            pl.BlockSpec((tm, RET_W), lambda i: (i, gate_col)),
            pl.BlockSpec(w_out.shape, lambda i: (0, 0)),
            pl.BlockSpec((1, ADA_CHUNKS, D_MODEL), lambda i: (i // tpb, 0, 0)),
        ],
        out_specs=pl.BlockSpec((tm, D_MODEL), lambda i: (i, 0)),
        out_shape=jax.ShapeDtypeStruct((n, D_MODEL), F32),
        compiler_params=_params(1),
    )(x2, a, o_f, o_b, z, w_out, mod)


def _mlp_kernel(x_ref, g_ref, mod_ref, w1_ref, w2_ref, o_ref, h_s, acc_s):
    j = pl.program_id(1)

    @pl.when(j == 0)
    def _():
        h_s[...] = _norm_mod(x_ref[...], g_ref, mod_ref, 3, 4).astype(BF16)
        acc_s[...] = jnp.zeros_like(acc_s)

    a = jnp.maximum(jnp.dot(h_s[...], w1_ref[...], preferred_element_type=F32), 0.0)
    acc_s[...] += jnp.dot((a * a).astype(BF16), w2_ref[...], preferred_element_type=F32)

    @pl.when(j == pl.num_programs(1) - 1)
    def _():
        o_ref[...] = x_ref[...] + mod_ref[0, 5:6, :] * acc_s[...]


def _mlp_call(x2, g, mod, w1, w2, t):
    n = x2.shape[0]
    tm = _pick_tile(t, 1024)
    tpb = t // tm
    tf = 512
    return pl.pallas_call(
        _mlp_kernel,
        grid=(n // tm, D_FF // tf),
        in_specs=[
            pl.BlockSpec((tm, D_MODEL), lambda i, j: (i, 0)),
            pl.BlockSpec((1, D_MODEL), lambda i, j: (0, 0)),
            pl.BlockSpec((1, ADA_CHUNKS, D_MODEL), lambda i, j: (i // tpb, 0, 0)),
            pl.BlockSpec((D_MODEL, tf), lambda i, j: (0, j)),
            pl.BlockSpec((tf, D_MODEL), lambda i, j: (j, 0)),
        ],
        out_specs=pl.BlockSpec((tm, D_MODEL), lambda i, j: (i, 0)),
        out_shape=jax.ShapeDtypeStruct((n, D_MODEL), F32),
        scratch_shapes=[pltpu.VMEM((tm, D_MODEL), BF16), pltpu.VMEM((tm, D_MODEL), F32)],
        compiler_params=_params(2),
    )(x2, g.reshape(1, D_MODEL), mod, w1, w2)


def _s5_weights(a_re, a_im, b_re, b_im, c_re, c_im, log_dt, backward):
    hp = lax.Precision.HIGHEST
    f = lambda v: v.astype(F32)
    a = lax.complex(f(a_re), f(a_im))
    dt = jnp.exp(f(log_dt))[:, None]
    lam = dt * a
    a_bar = jnp.exp(lam)
    b_bar = ((a_bar - 1) / a)[..., None] * lax.complex(f(b_re), f(b_im))
    c_mat = lax.complex(f(c_re), f(c_im))
    tau = jnp.arange(S5_L + 1, dtype=F32)
    pw = jnp.exp(tau[:, None, None] * lam[None])
    kern = jnp.einsum('ghp,tgp,gpk->tghk', c_mat, pw[:S5_L], b_bar, precision=hp).real
    s_idx = jnp.arange(S5_L)[:, None]
    t_idx = jnp.arange(S5_L)[None, :]
    lag = (s_idx - t_idx) if backward else (t_idx - s_idx)
    kt = jnp.where((lag >= 0)[:, :, None, None, None], kern[jnp.clip(lag, 0, S5_L - 1)], 0.0)
    t_mat = kt.transpose(2, 0, 4, 1, 3).reshape(S5_GROUPS, S5_K, S5_K)
    w_pow = pw[:S5_L] if backward else pw[:S5_L][::-1]
    w_c = w_pow[:, :, :, None] * b_bar[None]
    w_c = w_c.transpose(1, 0, 3, 2).reshape(S5_GROUPS, S5_K, S5_STATE)
    w_mat = jnp.concatenate([w_c.real, w_c.imag], axis=-1)
    w_swp = jnp.concatenate([w_c.imag, w_c.real], axis=-1)
    v_pow = pw[1:][::-1] if backward else pw[1:]
    v_c = c_mat[None] * v_pow[:, :, None, :]
    v_c = v_c.transpose(1, 3, 0, 2).reshape(S5_GROUPS, S5_STATE, S5_K)
    v_mat = jnp.concatenate([v_c.real, -v_c.imag], axis=1)
    al = pw[S5_L]
    coef = jnp.stack([
        jnp.concatenate([al.real, al.real], axis=-1),
        jnp.concatenate([-al.imag, al.imag], axis=-1),
        jnp.concatenate([al.imag, -al.imag], axis=-1),
    ], axis=1)
    return t_mat, w_mat, w_swp, v_mat, coef


def _s5_kernel(u_ref, t_ref, wf_ref, wfs_ref, wb_ref, wbs_ref, vf_ref, vb_ref, coef_ref, y_ref,
               sf_s, sfs_s, sb_s, sbs_s, xf_s, xb_s, *, nb, nk_ctx, nk):
    u = u_ref[0]
    sf_s[...] = jnp.dot(u, wf_ref[0], preferred_element_type=F32)
    sfs_s[...] = jnp.dot(u, wfs_ref[0], preferred_element_type=F32)
    sb_s[...] = jnp.dot(u, wb_ref[0], preferred_element_type=F32)
    sbs_s[...] = jnp.dot(u, wbs_ref[0], preferred_element_type=F32)
    coef = coef_ref[0]
    zero = jnp.zeros((nb, 2 * S5_STATE), F32)

    def step(k, carry, s_s, ss_s, x_s, ar, ai, ais):
        x, xs = carry
        r = pl.multiple_of(k * nb, nb)
        x_s[pl.ds(r, nb), :] = x
        xn = ar * x + ai * xs + s_s[pl.ds(r, nb), :]
        xsn = ar * xs + ais * x + ss_s[pl.ds(r, nb), :]
        return xn, xsn

    fstep = functools.partial(step, s_s=sf_s, ss_s=sfs_s, x_s=xf_s,
                              ar=coef[0:1], ai=coef[1:2], ais=coef[2:3])
    bstep = functools.partial(step, s_s=sb_s, ss_s=sbs_s, x_s=xb_s,
                              ar=coef[3:4], ai=coef[4:5], ais=coef[5:6])
    lax.fori_loop(0, nk, fstep, (zero, zero))
    carry = lax.fori_loop(0, nk_ctx, lambda i, cr: bstep(nk_ctx - 1 - i, cr), (zero, zero))
    lax.fori_loop(0, nk - nk_ctx, lambda i, cr: bstep(nk - 1 - i, cr), carry)
    y = jnp.dot(u, t_ref[0], preferred_element_type=F32)
    y = y + jnp.dot(xf_s[...].astype(BF16), vf_ref[0], preferred_element_type=F32)
    y = y + jnp.dot(xb_s[...].astype(BF16), vb_ref[0], preferred_element_type=F32)
    y_ref[0] = y.astype(y_ref.dtype)


def _s5_call(u, wts, nb, nk_ctx, nk):
    g, rows, _ = u.shape
    t_mat, wf, wfs, wb, wbs, vf, vb, coef = wts
    spec3 = lambda a: pl.BlockSpec((1,) + a.shape[1:], lambda i: (i, 0, 0))
    return pl.pallas_call(
        functools.partial(_s5_kernel, nb=nb, nk_ctx=nk_ctx, nk=nk),
        grid=(g,),
        in_specs=[spec3(a) for a in (u, t_mat, wf, wfs, wb, wbs, vf, vb, coef)],
        out_specs=pl.BlockSpec((1, rows, S5_K), lambda i: (i, 0, 0)),
        out_shape=jax.ShapeDtypeStruct((g, rows, S5_K), F32),
        scratch_shapes=[pltpu.VMEM((rows, 2 * S5_STATE), F32) for _ in range(6)],
        compiler_params=_params(1),
    )(u, t_mat, wf, wfs, wb, wbs, vf, vb, coef)


def _s5_out_kernel(x_ref, y_ref, h_ref, d_ref, w_ref, mod_ref, o_ref):
    y = y_ref[...] + d_ref[...] * h_ref[...].astype(F32)
    g = 0.5 * y * (1.0 + jnp.tanh(math.sqrt(2.0 / math.pi) * (y + 0.044715 * (y * y * y))))
    z = jnp.dot(g.astype(BF16), w_ref[...], preferred_element_type=F32)
    out = z[:, :D_MODEL] * _sigmoid(z[:, D_MODEL:])
    o_ref[...] = x_ref[...] + mod_ref[0, 2:3, :] * out


def _s5_out_call(x2, y, h, d_skip, w_glu, mod, t):
    n = x2.shape[0]
    tm = _pick_tile(t, 512)
    tpb = t // tm
    tile = pl.BlockSpec((tm, D_MODEL), lambda i: (i, 0))
    return pl.pallas_call(
        _s5_out_kernel,
        grid=(n // tm,),
        in_specs=[
            tile, tile, tile,
            pl.BlockSpec((1, D_MODEL), lambda i: (0, 0)),
            pl.BlockSpec(w_glu.shape, lambda i: (0, 0)),
            pl.BlockSpec((1, ADA_CHUNKS, D_MODEL), lambda i: (i // tpb, 0, 0)),
        ],
        out_specs=tile,
        out_shape=jax.ShapeDtypeStruct((n, D_MODEL), F32),
        compiler_params=_params(1),
    )(x2, y, h, d_skip.reshape(1, D_MODEL).astype(F32), w_glu, mod)


def _even_layer(x2, xc2, mod_l, mod_c, b, t, tc, need_ctx, p):
    w_in = jnp.concatenate([p['w_in'][:, :MLA_IN], jnp.zeros((D_MODEL, MLA_IN_PAD - MLA_IN), F32),
                            p['w_in'][:, MLA_IN:]], axis=1).astype(BF16)
    z_c = _norm_mod_call(xc2, p['norm1_g'], mod_c, tc, w=w_in)
    z_l = _norm_mod_call(x2, p['norm1_g'], mod_l, t, w=w_in)
    wts = _mla_weights(p['q_norm_g'], p['w_uq'], p['kv_norm_g'], p['w_ukv'], p['qn_g'], p['kn_g'])
    q_c, k_c, v_c = _mla_prep_call(z_c, tc, wts, *_mla_tables(tc, False))
    q_l, k_l, v_l = _mla_prep_call(z_l, t, wts, *_mla_tables(t, True))
    a_l = _attn_call(q_l, [k_c, k_l], [v_c, v_l], b, t, [tc, t])
    lg = jnp.stack([jnp.log1p(-jnp.exp2(p['lg_f'].astype(F32))), jnp.log1p(-jnp.exp2(p['lg_b'].astype(F32)))])
    zero = jnp.zeros((b, N_PAIR, 128, 128), F32)
    of_c, ob_c, s_cf, s_cb = _ret_call(z_c, lg, _ret_tables(tc, False), zero, zero, b, tc)
    of_l, ob_l, _, _ = _ret_call(z_l, lg, _ret_tables(t, True), s_cf, s_cb, b, t)
    w_out = p['w_out'].astype(BF16)
    x2 = _mix_out_call(x2, a_l, of_l, ob_l, z_l, w_out, mod_l, t)
    if need_ctx:
        a_c = _attn_call(q_c, [k_c], [v_c], b, tc, [tc])
        xc2 = _mix_out_call(xc2, a_c, of_c, ob_c, z_c, w_out, mod_c, tc)
    return x2, xc2


def _odd_layer(x2, xc2, mod_l, mod_c, b, t, tc, need_ctx, p):
    h_c = _norm_mod_call(xc2, p['norm1_g'], mod_c, tc)
    h_l = _norm_mod_call(x2, p['norm1_g'], mod_l, t)
    wf = _s5_weights(*p['s5_f'], backward=False)
    wb = _s5_weights(*p['s5_b'], backward=True)
    t_mat = (wf[0] + wb[0]).astype(BF16)
    coef = jnp.concatenate([wf[4], wb[4], jnp.zeros((S5_GROUPS, 2, 2 * S5_STATE), F32)], axis=1)
    wts = (t_mat, wf[1].astype(BF16), wf[2].astype(BF16), wb[1].astype(BF16), wb[2].astype(BF16),
           wf[3].astype(BF16), wb[3].astype(BF16), coef)
    nk_ctx, nk = tc // S5_L, (tc + t) // S5_L
    h_all = jnp.concatenate([h_c.reshape(b, tc, D_MODEL), h_l.reshape(b, t, D_MODEL)], axis=1)
    u = h_all.reshape(b, nk, S5_L, S5_GROUPS, S5_GROUP).transpose(3, 1, 0, 2, 4).reshape(S5_GROUPS, nk * b, S5_K)
    y = _s5_call(u, wts, b, nk_ctx, nk)
    y = y.reshape(S5_GROUPS, nk, b, S5_L, S5_GROUP).transpose(2, 1, 3, 0, 4).reshape(b, tc + t, D_MODEL)
    w_glu = p['w_glu'].astype(BF16)
    x2 = _s5_out_call(x2, y[:, tc:].reshape(b * t, D_MODEL), h_l, p['d_skip'], w_glu, mod_l, t)
    if need_ctx:
        xc2 = _s5_out_call(xc2, y[:, :tc].reshape(b * tc, D_MODEL), h_c, p['d_skip'], w_glu, mod_c, tc)
    return x2, xc2


def kernel(x, c, ctx, c_ctx, ada_w, ada_b, norm1_g, norm2_g, mlp_w1, mlp_w2, w_in, mla_q_norm_g, mla_w_uq, mla_kv_norm_g, mla_w_ukv, mla_qn_g, mla_kn_g, ret_lg_f, ret_lg_b, w_out, s5_a_re_f, s5_a_im_f, s5_b_re_f, s5_b_im_f, s5_c_re_f, s5_c_im_f, s5_log_dt_f, s5_a_re_b, s5_a_im_b, s5_b_re_b, s5_b_im_b, s5_c_re_b, s5_c_im_b, s5_log_dt_b, s5_d, s5_w_glu):
    b, t, _ = x.shape
    tc = ctx.shape[1]
    depth = ada_w.shape[0]
    assert t % S5_L == 0 and tc % S5_L == 0 and b % 8 == 0
    rows = -(-(b + 1) // 8) * 8
    cc = jnp.zeros((rows, D_MODEL), F32).at[:b].set(c.astype(F32)).at[b].set(c_ctx.astype(F32))
    mod = _ada_all(cc, ada_w.astype(F32), ada_b.astype(F32))
    x2 = x.reshape(b * t, D_MODEL).astype(F32)
    xc2 = ctx.reshape(b * tc, D_MODEL).astype(F32)
    for l in range(depth):
        need_ctx = l < depth - 1
        mod_l = mod[l, :b].reshape(b, ADA_CHUNKS, D_MODEL)
        mod_c = jnp.broadcast_to(mod[l, b].reshape(1, ADA_CHUNKS, D_MODEL), (b, ADA_CHUNKS, D_MODEL))
        if l % 2 == 0:
            e = l // 2
            p = dict(norm1_g=norm1_g[l], w_in=w_in[e], q_norm_g=mla_q_norm_g[e], w_uq=mla_w_uq[e],
                     kv_norm_g=mla_kv_norm_g[e], w_ukv=mla_w_ukv[e], qn_g=mla_qn_g[e], kn_g=mla_kn_g[e],
                     lg_f=ret_lg_f[e], lg_b=ret_lg_b[e], w_out=w_out[e])
            x2, xc2 = _even_layer(x2, xc2, mod_l, mod_c, b, t, tc, need_ctx, p)
        else:
            o = l // 2
            p = dict(norm1_g=norm1_g[l], d_skip=s5_d[o], w_glu=s5_w_glu[o],
                     s5_f=(s5_a_re_f[o], s5_a_im_f[o], s5_b_re_f[o], s5_b_im_f[o], s5_c_re_f[o], s5_c_im_f[o], s5_log_dt_f[o]),
                     s5_b=(s5_a_re_b[o], s5_a_im_b[o], s5_b_re_b[o], s5_b_im_b[o], s5_c_re_b[o], s5_c_im_b[o], s5_log_dt_b[o]))
            x2, xc2 = _odd_layer(x2, xc2, mod_l, mod_c, b, t, tc, need_ctx, p)
        w1 = mlp_w1[l].astype(BF16)
        w2 = mlp_w2[l].astype(BF16)
        x2 = _mlp_call(x2, norm2_g[l], mod_l, w1, w2, t)
        if need_ctx:
            xc2 = _mlp_call(xc2, norm2_g[l], mod_c, w1, w2, tc)
    return x2.reshape(b, t, D_MODEL).astype(x.dtype)
```

```python
import functools
import math

import jax
import jax.numpy as jnp
import numpy as np
from jax import lax
from jax.experimental import pallas as pl
from jax.experimental.pallas import tpu as pltpu

F32 = jnp.float32
BF16 = jnp.bfloat16

D_MODEL = 1024
EPS = 1e-6
ADA_CHUNKS = 6
GRID_W = 64
ROPE_BASE = 10000.0
LANES = 128
N_SLAB = D_MODEL // LANES

H_A = 8
Q_LORA = 256
KV_LORA = 128
NOPE_DIM = 64
ROPE_DIM = 32
QK_DIM = NOPE_DIM + ROPE_DIM
V_DIM = 64
HEAD_SLOT = LANES
MLA_IN = Q_LORA + KV_LORA + ROPE_DIM
MLA_IN_PAD = 512

H_R = 8
DK_R = 64
DV_R = 64
RET_W = H_R * DK_R
N_PAIR = H_R // 2

S5_GROUP = 16
S5_GROUPS = D_MODEL // S5_GROUP
S5_STATE = 64
S5_L = 8
S5_K = S5_L * LANES
S5_GPS = LANES // S5_GROUP
S5_HALF = S5_GPS * S5_STATE

D_FF = 4 * D_MODEL
Z_WIDTH = MLA_IN_PAD + 4 * RET_W

VMEM_LIMIT = 56 * 1024 * 1024

_NT = (((1,), (1,)), ((), ()))
_HP = lax.Precision.HIGHEST


def _params(n_grid):
    return pltpu.CompilerParams(
        dimension_semantics=("arbitrary",) * n_grid, vmem_limit_bytes=VMEM_LIMIT)


def _sigmoid(x):
    return 1.0 / (1.0 + jnp.exp(-x))


def _pick_tile(t, pref):
    tile = min(t, pref)
    while t % tile:
        tile //= 2
    return tile


def _ada_kernel(c_ref, w_ref, b_ref, o_ref):
    cc = c_ref[...]
    s = cc * _sigmoid(cc)
    o_ref[0] = jnp.dot(s, w_ref[0], preferred_element_type=F32, precision=_HP) + b_ref[0]


def _ada_all(cc, ada_w, ada_b):
    depth, _, width = ada_w.shape
    r = cc.shape[0]
    tn = 1536
    return pl.pallas_call(
        _ada_kernel,
        grid=(depth, width // tn),
        in_specs=[
            pl.BlockSpec((r, D_MODEL), lambda l, j: (0, 0)),
            pl.BlockSpec((1, D_MODEL, tn), lambda l, j: (l, 0, j)),
            pl.BlockSpec((1, 1, tn), lambda l, j: (l, 0, j)),
        ],
        out_specs=pl.BlockSpec((1, r, tn), lambda l, j: (l, 0, j)),
        out_shape=jax.ShapeDtypeStruct((depth, r, width), F32),
        compiler_params=_params(2),
        name="ada_mod",
    )(cc, ada_w, ada_b.reshape(depth, 1, width))


def _norm_mod(x, g_ref, mod_ref, shift_row, scale_row):
    ms = jnp.mean(x * x, axis=-1, keepdims=True)
    y = x * lax.rsqrt(ms + EPS) * g_ref[...]
    return y * (1.0 + mod_ref[0, scale_row:scale_row + 1, :]) + mod_ref[0, shift_row:shift_row + 1, :]


def _nmm_kernel(x_ref, g_ref, mod_ref, w_ref, o_ref):
    h = _norm_mod(x_ref[...], g_ref, mod_ref, 0, 1).astype(BF16)
    o_ref[...] = jnp.dot(h, w_ref[...], preferred_element_type=F32).astype(o_ref.dtype)


def _norm_mod_matmul_call(x2, g, mod, t, w):
    n = x2.shape[0]
    tm = _pick_tile(t, 512)
    tpb = t // tm
    n_out = w.shape[1]
    return pl.pallas_call(
        _nmm_kernel,
        grid=(n // tm,),
        in_specs=[
            pl.BlockSpec((tm, D_MODEL), lambda i: (i, 0)),
            pl.BlockSpec((1, D_MODEL), lambda i: (0, 0)),
            pl.BlockSpec((1, ADA_CHUNKS, D_MODEL), lambda i: (i // tpb, 0, 0)),
            pl.BlockSpec((D_MODEL, n_out), lambda i: (0, 0)),
        ],
        out_specs=pl.BlockSpec((tm, n_out), lambda i: (i, 0)),
        out_shape=jax.ShapeDtypeStruct((n, n_out), BF16),
        compiler_params=_params(1),
        name="norm_mod_w_in",
    )(x2, g.reshape(1, D_MODEL), mod, w)


def _mla_prep_kernel(z_ref, gq_ref, gkv_ref, wq_ref, wqs_ref, wkc_ref, wkcs_ref, wkr_ref, wkrs_ref,
                     wv_ref, gqn_ref, gqns_ref, gkn_ref, gkns_ref, cos_ref, sin_ref,
                     q_ref, k_ref, v_ref):
    cq = z_ref[:, 0:Q_LORA].astype(F32)
    ckv = z_ref[:, Q_LORA:Q_LORA + KV_LORA].astype(F32)
    kr = z_ref[:, Q_LORA + KV_LORA:MLA_IN_PAD]
    cqn = (cq * lax.rsqrt(jnp.mean(cq * cq, axis=-1, keepdims=True) + EPS) * gq_ref[...]).astype(BF16)
    ckn = (ckv * lax.rsqrt(jnp.mean(ckv * ckv, axis=-1, keepdims=True) + EPS) * gkv_ref[...]).astype(BF16)
    q = jnp.dot(cqn, wq_ref[...], preferred_element_type=F32)
    qs = jnp.dot(cqn, wqs_ref[...], preferred_element_type=F32)
    k = jnp.dot(ckn, wkc_ref[...], preferred_element_type=F32) + jnp.dot(kr, wkr_ref[...], preferred_element_type=F32)
    ks = jnp.dot(ckn, wkcs_ref[...], preferred_element_type=F32) + jnp.dot(kr, wkrs_ref[...], preferred_element_type=F32)
    v_ref[...] = jnp.dot(ckn, wv_ref[...], preferred_element_type=F32).astype(v_ref.dtype)
    cos = cos_ref[...]
    sin = sin_ref[...]
    q_scale = QK_DIM ** -0.5
    for h in range(H_A):
        sl = slice(h * HEAD_SLOT, (h + 1) * HEAD_SLOT)
        qh = q[:, sl]
        rq = lax.rsqrt(jnp.sum(qh * qh, axis=-1, keepdims=True) * (1.0 / QK_DIM) + EPS)
        q_rot = (qh * gqn_ref[...] * cos + qs[:, sl] * gqns_ref[...] * sin) * (rq * q_scale)
        q_ref[:, sl] = q_rot.astype(q_ref.dtype)
        kh = k[:, sl]
        rk = lax.rsqrt(jnp.sum(kh * kh, axis=-1, keepdims=True) * (1.0 / QK_DIM) + EPS)
        k_rot = (kh * gkn_ref[...] * cos + ks[:, sl] * gkns_ref[...] * sin) * rk
        k_ref[:, sl] = k_rot.astype(k_ref.dtype)


def _mla_prep_call(z, t, wts, cos_t, sin_t):
    n = z.shape[0]
    tm = _pick_tile(t, 512)
    tpb = t // tm
    full = lambda a: pl.BlockSpec(a.shape, lambda i: (0,) * a.ndim)
    in_specs = [pl.BlockSpec((tm, MLA_IN_PAD), lambda i: (i, 0))] + [full(a) for a in wts] + [
        pl.BlockSpec((tm, HEAD_SLOT), lambda i: (i % tpb, 0)),
        pl.BlockSpec((tm, HEAD_SLOT), lambda i: (i % tpb, 0)),
    ]
    hw = H_A * HEAD_SLOT
    return pl.pallas_call(
        _mla_prep_kernel,
        grid=(n // tm,),
        in_specs=in_specs,
        out_specs=[
            pl.BlockSpec((tm, hw), lambda i: (i, 0)),
            pl.BlockSpec((tm, hw), lambda i: (i, 0)),
            pl.BlockSpec((tm, H_A * V_DIM), lambda i: (i, 0)),
        ],
        out_shape=[
            jax.ShapeDtypeStruct((n, hw), BF16),
            jax.ShapeDtypeStruct((n, hw), BF16),
            jax.ShapeDtypeStruct((n, H_A * V_DIM), BF16),
        ],
        compiler_params=_params(1),
        name="mla_prep",
    )(z, *wts, cos_t, sin_t)


def _mla_weights(q_norm_g, w_uq, kv_norm_g, w_ukv, qn_g, kn_g):
    pad = HEAD_SLOT - QK_DIM
    half = ROPE_DIM // 2
    perm = jnp.arange(HEAD_SLOT)
    perm = perm.at[NOPE_DIM:NOPE_DIM + half].set(jnp.arange(NOPE_DIM + half, NOPE_DIM + ROPE_DIM))
    perm = perm.at[NOPE_DIM + half:NOPE_DIM + ROPE_DIM].set(jnp.arange(NOPE_DIM, NOPE_DIM + half))

    def slots(w):
        wp = jnp.pad(w, ((0, 0), (0, 0), (0, pad)))
        return wp, wp[:, :, perm]

    def flat(w):
        return w.reshape(w.shape[0], H_A * HEAD_SLOT).astype(BF16)

    wq, wqs = slots(w_uq.reshape(Q_LORA, H_A, QK_DIM))
    w_kv = w_ukv.reshape(KV_LORA, H_A, NOPE_DIM + V_DIM)
    wkc, wkcs = slots(jnp.pad(w_kv[:, :, :NOPE_DIM], ((0, 0), (0, 0), (0, ROPE_DIM))))
    eye = jnp.zeros((HEAD_SLOT, QK_DIM), F32).at[jnp.arange(ROPE_DIM), NOPE_DIM + jnp.arange(ROPE_DIM)].set(1.0)
    wkr, wkrs = slots(jnp.broadcast_to(eye[:, None, :], (HEAD_SLOT, H_A, QK_DIM)))
    wv = w_kv[:, :, NOPE_DIM:].reshape(KV_LORA, H_A * V_DIM).astype(BF16)

    def gains(g):
        gp = jnp.pad(g.astype(F32), (0, pad))
        return gp.reshape(1, HEAD_SLOT), gp[perm].reshape(1, HEAD_SLOT)

    gqn, gqns = gains(qn_g)
    gkn, gkns = gains(kn_g)
    return [q_norm_g.reshape(1, Q_LORA).astype(F32), kv_norm_g.reshape(1, KV_LORA).astype(F32),
            flat(wq), flat(wqs), flat(wkc), flat(wkcs), flat(wkr), flat(wkrs), wv, gqn, gqns, gkn, gkns]


def _mla_tables(n_tok, rotary):
    cos = jnp.ones((n_tok, HEAD_SLOT), F32)
    sin = jnp.zeros((n_tok, HEAD_SLOT), F32)
    if rotary:
        rows = n_tok // GRID_W
        r = jnp.repeat(jnp.arange(rows, dtype=F32), GRID_W)
        col = jnp.tile(jnp.arange(GRID_W, dtype=F32), rows)
        nf = ROPE_DIM // 4
        f = ROPE_BASE ** (-jnp.arange(nf, dtype=F32) / nf)
        ang = jnp.concatenate([r[:, None] * f, col[:, None] * f], axis=-1)
        c, s = jnp.cos(ang), jnp.sin(ang)
        half = ROPE_DIM // 2
        cos = cos.at[:, NOPE_DIM:NOPE_DIM + half].set(c).at[:, NOPE_DIM + half:NOPE_DIM + ROPE_DIM].set(c)
        sin = sin.at[:, NOPE_DIM:NOPE_DIM + half].set(-s).at[:, NOPE_DIM + half:NOPE_DIM + ROPE_DIM].set(s)
    return cos, sin


def _attn_kernel(*refs, nseg):
    q_ref = refs[0]
    k_refs = refs[1:1 + nseg]
    v_refs = refs[1 + nseg:1 + 2 * nseg]
    o_ref = refs[1 + 2 * nseg]
    lane = lax.broadcasted_iota(jnp.int32, (1, HEAD_SLOT), 1)
    lo = lane < V_DIM
    for j in range(H_A // 2):
        vsl = slice(j * HEAD_SLOT, (j + 1) * HEAD_SLOT)
        acc = None
        for e in range(2):
            h = 2 * j + e
            sl = slice(h * HEAD_SLOT, (h + 1) * HEAD_SLOT)
            qh = q_ref[:, sl]
            ss = [lax.dot_general(qh, k[:, sl], _NT, preferred_element_type=F32) for k in k_refs]
            m = functools.reduce(jnp.maximum, [jnp.max(s, axis=-1, keepdims=True) for s in ss])
            ps = [jnp.exp(s - m) for s in ss]
            l = functools.reduce(lambda a, b: a + b, [jnp.sum(p, axis=-1, keepdims=True) for p in ps])
            mask = lo if e == 0 else jnp.logical_not(lo)
            o = None
            for p, v in zip(ps, v_refs):
                vm = jnp.where(mask, v[:, vsl], jnp.zeros((), BF16))
                t = jnp.dot(p.astype(BF16), vm, preferred_element_type=F32)
                o = t if o is None else o + t
            o = o * (1.0 / l)
            acc = o if acc is None else acc + o
        o_ref[:, vsl] = acc.astype(o_ref.dtype)


def _attn_call(q, ks, vs, b, t, tks):
    n = q.shape[0]
    tq = _pick_tile(t, 256)
    tpb = t // tq
    hw = H_A * HEAD_SLOT
    vw = H_A * V_DIM
    nseg = len(ks)
    in_specs = [pl.BlockSpec((tq, hw), lambda i: (i, 0))]
    in_specs += [pl.BlockSpec((tk, hw), lambda i: (i // tpb, 0)) for tk in tks]
    in_specs += [pl.BlockSpec((tk, vw), lambda i: (i // tpb, 0)) for tk in tks]
    return pl.pallas_call(
        functools.partial(_attn_kernel, nseg=nseg),
        grid=(n // tq,),
        in_specs=in_specs,
        out_specs=pl.BlockSpec((tq, vw), lambda i: (i, 0)),
        out_shape=jax.ShapeDtypeStruct((n, vw), BF16),
        compiler_params=_params(1),
        name="attention_%dseg" % nseg,
    )(q, *ks, *vs)


def _rotate_pairs(x, cos, sin_lo, sin_hi):
    return x * cos + pltpu.roll(x, LANES - DK_R // 2, 1) * sin_lo + pltpu.roll(x, DK_R // 2, 1) * sin_hi


def _ret_kernel(lg_ref, qf_ref, kf_ref, vf_ref, cf_ref, slf_ref, shf_ref,
                qb_ref, kb_ref, vb_ref, cb_ref, slb_ref, shb_ref, s0f_ref, s0b_ref,
                of_ref, ob_ref, sff_ref, sfb_ref, df_s, db_s, sf_s, sb_s, *, chunk):
    b = pl.program_id(0)
    c = pl.program_id(1)
    nc = pl.num_programs(1)

    @pl.when(jnp.logical_and(b == 0, c == 0))
    def _():
        ii = lax.broadcasted_iota(jnp.int32, (chunk, chunk), 0)
        jj = lax.broadcasted_iota(jnp.int32, (chunk, chunk), 1)
        diff = (ii - jj).astype(F32)
        for h in range(H_R):
            df_s[h] = jnp.where(ii >= jj, jnp.exp(jnp.where(ii >= jj, diff, 0.0) * lg_ref[0, h]), 0.0)
            db_s[h] = jnp.where(jj > ii, jnp.exp(jnp.where(jj > ii, -diff, 0.0) * lg_ref[1, h]), 0.0)

    @pl.when(c == 0)
    def _():
        sf_s[...] = s0f_ref[0]
        sb_s[...] = s0b_ref[0]

    lane = lax.broadcasted_iota(jnp.int32, (1, LANES), 1)
    lo = lane < DK_R
    row = lax.broadcasted_iota(jnp.int32, (LANES, LANES), 0)
    colm = lax.broadcasted_iota(jnp.int32, (LANES, LANES), 1)
    blockdiag = (row < DK_R) == (colm < DK_R)
    pos = lax.broadcasted_iota(jnp.int32, (chunk, 1), 0).astype(F32)

    def one_direction(d, q_ref, k_ref, v_ref, cos_ref, sl_ref, sh_ref, d_s, s_s, o_ref):
        cos, sin_lo, sin_hi = cos_ref[...], sl_ref[...], sh_ref[...]
        for j in range(N_PAIR):
            sl = slice(j * LANES, (j + 1) * LANES)
            lg = jnp.where(lo, lg_ref[d, 2 * j], lg_ref[d, 2 * j + 1])
            q2 = _rotate_pairs(q_ref[:, sl].astype(F32), cos, sin_lo, sin_hi)
            k2 = _rotate_pairs(k_ref[:, sl].astype(F32), cos, sin_lo, sin_hi) * (DK_R ** -0.5)
            v2 = v_ref[:, sl]
            if d == 0:
                q_dec = jnp.exp((pos + 1.0) * lg)
                k_dec = jnp.exp((chunk - 1.0 - pos) * lg)
            else:
                q_dec = jnp.exp((chunk - pos) * lg)
                k_dec = jnp.exp(pos * lg)
            c_dec = jnp.exp(chunk * lg)
            s2 = s_s[j]
            o = jnp.dot((q2 * q_dec).astype(BF16), s2.astype(BF16), preferred_element_type=F32)
            k2b = k2.astype(BF16)
            for e in range(2):
                mask = lo if e == 0 else jnp.logical_not(lo)
                qe = jnp.where(mask, q2, 0.0).astype(BF16)
                sc = lax.dot_general(qe, k2b, _NT, preferred_element_type=F32) * d_s[2 * j + e]
                ve = jnp.where(mask, v2, jnp.zeros((), BF16))
                o = o + jnp.dot(sc.astype(BF16), ve, preferred_element_type=F32)
            o_ref[:, sl] = o.astype(o_ref.dtype)
            kd_t = (k2 * k_dec).T.astype(BF16)
            upd = jnp.dot(kd_t, v2, preferred_element_type=F32)
            s_s[j] = s2 * c_dec + jnp.where(blockdiag, upd, 0.0)

    one_direction(0, qf_ref, kf_ref, vf_ref, cf_ref, slf_ref, shf_ref, df_s, sf_s, of_ref)
    one_direction(1, qb_ref, kb_ref, vb_ref, cb_ref, slb_ref, shb_ref, db_s, sb_s, ob_ref)

    @pl.when(c == nc - 1)
    def _():
        sff_ref[0] = sf_s[...]
        sfb_ref[0] = sb_s[...]


def _ret_call(z, lg, tables, s0f, s0b, b, t):
    n = z.shape[0]
    chunk = _pick_tile(t, 256)
    nc = t // chunk
    cos_t, sin_lo_t, sin_hi_t = tables
    col0 = MLA_IN_PAD // RET_W

    def zspec(part, rev):
        if rev:
            return pl.BlockSpec((chunk, RET_W), lambda bi, ci: (bi * nc + nc - 1 - ci, col0 + part))
        return pl.BlockSpec((chunk, RET_W), lambda bi, ci: (bi * nc + ci, col0 + part))

    def tspec(rev):
        if rev:
            return pl.BlockSpec((chunk, LANES), lambda bi, ci: (nc - 1 - ci, 0))
        return pl.BlockSpec((chunk, LANES), lambda bi, ci: (ci, 0))

    st_spec = pl.BlockSpec((1, N_PAIR, LANES, LANES), lambda bi, ci: (bi, 0, 0, 0))
    in_specs = [pl.BlockSpec(memory_space=pltpu.SMEM)]
    in_specs += [zspec(0, False), zspec(1, False), zspec(2, False), tspec(False), tspec(False), tspec(False)]
    in_specs += [zspec(0, True), zspec(1, True), zspec(2, True), tspec(True), tspec(True), tspec(True)]
    in_specs += [st_spec, st_spec]
    out_specs = [
        pl.BlockSpec((chunk, RET_W), lambda bi, ci: (bi * nc + ci, 0)),
        pl.BlockSpec((chunk, RET_W), lambda bi, ci: (bi * nc + nc - 1 - ci, 0)),
        st_spec, st_spec,
    ]
    st_shape = jax.ShapeDtypeStruct((b, N_PAIR, LANES, LANES), F32)
    return pl.pallas_call(
        functools.partial(_ret_kernel, chunk=chunk),
        grid=(b, nc),
        in_specs=in_specs,
        out_specs=out_specs,
        out_shape=[jax.ShapeDtypeStruct((n, RET_W), BF16), jax.ShapeDtypeStruct((n, RET_W), BF16),
                   st_shape, st_shape],
        scratch_shapes=[
            pltpu.VMEM((H_R, chunk, chunk), F32), pltpu.VMEM((H_R, chunk, chunk), F32),
            pltpu.VMEM((N_PAIR, LANES, LANES), F32), pltpu.VMEM((N_PAIR, LANES, LANES), F32),
        ],
        compiler_params=_params(2),
        name="retention",
    )(lg, z, z, z, cos_t, sin_lo_t, sin_hi_t, z, z, z, cos_t, sin_lo_t, sin_hi_t, s0f, s0b)


def _ret_tables(n_tok, rotary):
    cos = jnp.ones((n_tok, LANES), F32)
    sin_lo = jnp.zeros((n_tok, LANES), F32)
    sin_hi = jnp.zeros((n_tok, LANES), F32)
    if rotary:
        nf = DK_R // 2
        theta = ROPE_BASE ** (-jnp.arange(nf, dtype=F32) / nf)
        ang = jnp.arange(n_tok, dtype=F32)[:, None] * theta
        c, s = jnp.cos(ang), jnp.sin(ang)
        z = jnp.zeros_like(s)
        cos = jnp.concatenate([c, c, c, c], axis=-1)
        sin_lo = jnp.concatenate([-s, z, -s, z], axis=-1)
        sin_hi = jnp.concatenate([z, s, z, s], axis=-1)
    return cos, sin_lo, sin_hi


def _mix_out_kernel(x_ref, a_ref, of_ref, ob_ref, g_ref, w_ref, mod_ref, o_ref):
    lane = lax.broadcasted_iota(jnp.int32, (1, LANES), 1)
    lo = lane < DV_R
    y = jnp.dot(a_ref[...], w_ref[0:H_A * V_DIM, :], preferred_element_type=F32)
    for j in range(N_PAIR):
        sl = slice(j * LANES, (j + 1) * LANES)
        o = of_ref[:, sl].astype(F32) + ob_ref[:, sl].astype(F32)
        s_lo = jnp.sum(jnp.where(lo, o, 0.0), axis=-1, keepdims=True)
        s_all = jnp.sum(o, axis=-1, keepdims=True)
        mu = jnp.where(lo, s_lo, s_all - s_lo) * (1.0 / DV_R)
        oc = o - mu
        q = oc * oc
        q_lo = jnp.sum(jnp.where(lo, q, 0.0), axis=-1, keepdims=True)
        q_all = jnp.sum(q, axis=-1, keepdims=True)
        var = jnp.where(lo, q_lo, q_all - q_lo) * (1.0 / DV_R)
        g = g_ref[:, sl].astype(F32)
        r = (oc * lax.rsqrt(var + EPS)) * (g * _sigmoid(g))
        row0 = H_A * V_DIM + j * LANES
        y = y + jnp.dot(r.astype(BF16), w_ref[row0:row0 + LANES, :], preferred_element_type=F32)
    o_ref[...] = x_ref[...] + mod_ref[0, 2:3, :] * y


def _mix_out_call(x2, a, o_f, o_b, z, w_out, mod, t):
    n = x2.shape[0]
    tm = _pick_tile(t, 512)
    tpb = t // tm
    gate_col = MLA_IN_PAD // RET_W + 3
    return pl.pallas_call(
        _mix_out_kernel,
        grid=(n // tm,),
        in_specs=[
            pl.BlockSpec((tm, D_MODEL), lambda i: (i, 0)),
            pl.BlockSpec((tm, H_A * V_DIM), lambda i: (i, 0)),
            pl.BlockSpec((tm, RET_W), lambda i: (i, 0)),
            pl.BlockSpec((tm, RET_W), lambda i: (i, 0)),
            pl.BlockSpec((tm, RET_W), lambda i: (i, gate_col)),
            pl.BlockSpec(w_out.shape, lambda i: (0, 0)),
            pl.BlockSpec((1, ADA_CHUNKS, D_MODEL), lambda i: (i // tpb, 0, 0)),
        ],
        out_specs=pl.BlockSpec((tm, D_MODEL), lambda i: (i, 0)),
        out_shape=jax.ShapeDtypeStruct((n, D_MODEL), F32),
        compiler_params=_params(1),
        name="mix_out",
    )(x2, a, o_f, o_b, z, w_out, mod)


def _mlp_kernel(x_ref, g_ref, mod_ref, w1_ref, w2_ref, o_ref, h_s, acc_s):
    j = pl.program_id(1)

    @pl.when(j == 0)
    def _():
        h_s[...] = _norm_mod(x_ref[...], g_ref, mod_ref, 3, 4).astype(BF16)
        acc_s[...] = jnp.zeros_like(acc_s)

    a = jnp.maximum(jnp.dot(h_s[...], w1_ref[...], preferred_element_type=F32), 0.0)
    acc_s[...] += jnp.dot((a * a).astype(BF16), w2_ref[...], preferred_element_type=F32)

    @pl.when(j == pl.num_programs(1) - 1)
    def _():
        o_ref[...] = x_ref[...] + mod_ref[0, 5:6, :] * acc_s[...]


def _mlp_call(x2, g, mod, w1, w2, t):
    n = x2.shape[0]
    tm = _pick_tile(t, 1024)
    tpb = t // tm
    tf = 512
    return pl.pallas_call(
        _mlp_kernel,
        grid=(n // tm, D_FF // tf),
        in_specs=[
            pl.BlockSpec((tm, D_MODEL), lambda i, j: (i, 0)),
            pl.BlockSpec((1, D_MODEL), lambda i, j: (0, 0)),
            pl.BlockSpec((1, ADA_CHUNKS, D_MODEL), lambda i, j: (i // tpb, 0, 0)),
            pl.BlockSpec((D_MODEL, tf), lambda i, j: (0, j)),
            pl.BlockSpec((tf, D_MODEL), lambda i, j: (j, 0)),
        ],
        out_specs=pl.BlockSpec((tm, D_MODEL), lambda i, j: (i, 0)),
        out_shape=jax.ShapeDtypeStruct((n, D_MODEL), F32),
        scratch_shapes=[pltpu.VMEM((tm, D_MODEL), BF16), pltpu.VMEM((tm, D_MODEL), F32)],
        compiler_params=_params(2),
        name="mlp",
    )(x2, g.reshape(1, D_MODEL), mod, w1, w2)


def _s5_weights(a_re, a_im, b_re, b_im, c_re, c_im, log_dt, backward):
    f = lambda v: v.astype(F32)
    ar, ai = f(a_re), f(a_im)
    dt = jnp.exp(f(log_dt))[:, None]
    lr, li = dt * ar, dt * ai
    steps = np.arange(S5_L)

    def powers(taus):
        tau = jnp.asarray(np.asarray(taus, np.float32))[:, None, None]
        mag = jnp.exp(tau * lr[None])
        return mag * jnp.cos(tau * li[None]), mag * jnp.sin(tau * li[None])

    def c_times(pr, pi):
        cr, ci = f(c_re)[None], f(c_im)[None]
        return cr * pr[:, :, None, :] - ci * pi[:, :, None, :], cr * pi[:, :, None, :] + ci * pr[:, :, None, :]

    a1r, a1i = powers([1])
    nr, ni = a1r[0] - 1.0, a1i[0]
    den = ar * ar + ai * ai
    qr, qi = (nr * ar + ni * ai) / den, (ni * ar - nr * ai) / den
    br, bi = f(b_re), f(b_im)
    bbr = qr[..., None] * br - qi[..., None] * bi
    bbi = qr[..., None] * bi + qi[..., None] * br
    dr, di = c_times(*powers(steps))
    kern = (jnp.einsum('tghp,gpk->tghk', dr, bbr, precision=_HP)
            - jnp.einsum('tghp,gpk->tghk', di, bbi, precision=_HP))
    eye = jnp.eye(S5_GPS, dtype=F32)
    s_idx = jnp.arange(S5_L)[:, None]
    t_idx = jnp.arange(S5_L)[None, :]
    lag = (s_idx - t_idx) if backward else (t_idx - s_idx)
    kt = jnp.where((lag >= 0)[:, :, None, None, None], kern[jnp.clip(lag, 0, S5_L - 1)], 0.0)
    kt = kt.reshape(S5_L, S5_L, N_SLAB, S5_GPS, S5_GROUP, S5_GROUP)
    t_mat = kt.transpose(2, 0, 3, 5, 1, 4)[:, :, :, :, :, None, :] * eye[None, None, :, None, None, :, None]
    t_mat = t_mat.reshape(N_SLAB, S5_K, S5_K)
    wpr, wpi = powers(steps if backward else S5_L - 1 - steps)
    wpr, wpi = wpr[..., None], wpi[..., None]
    wr = wpr * bbr[None] - wpi * bbi[None]
    wi = wpr * bbi[None] + wpi * bbr[None]

    def w_slab(w):
        w = w.reshape(S5_L, N_SLAB, S5_GPS, S5_STATE, S5_GROUP)
        w = w.transpose(1, 0, 2, 4, 3)[:, :, :, :, None, :] * eye[None, None, :, None, :, None]
        return w.reshape(N_SLAB, S5_K, S5_HALF)

    w_mat = jnp.concatenate([w_slab(wr), w_slab(wi)], axis=-1)
    vr, vi = c_times(*powers(S5_L - steps if backward else steps + 1))

    def v_slab(v):
        v = v.reshape(S5_L, N_SLAB, S5_GPS, S5_GROUP, S5_STATE)
        v = v.transpose(1, 2, 4, 0, 3)[:, :, :, :, None, :] * eye[None, :, None, None, :, None]
        return v.reshape(N_SLAB, S5_HALF, S5_K)

    v_mat = jnp.concatenate([v_slab(vr), -v_slab(vi)], axis=1)
    alr, ali = powers([S5_L])
    coef = jnp.stack([alr[0].reshape(N_SLAB, S5_HALF), ali[0].reshape(N_SLAB, S5_HALF)], axis=1)
    return t_mat.astype(BF16), w_mat.astype(BF16), v_mat.astype(BF16), coef


def _nm_s5_kernel(x_ref, g_ref, mod_ref, h_ref, u_ref, hs_s, *, nkb):
    h = _norm_mod(x_ref[...], g_ref, mod_ref, 0, 1)
    h_ref[...] = h.astype(h_ref.dtype)
    for j in range(N_SLAB):
        hs_s[j] = h[:, j * LANES:(j + 1) * LANES]
    for s in range(S5_L):
        for j in range(N_SLAB):
            blk = hs_s[j, pl.ds(s, nkb, stride=S5_L), :]
            u_ref[j, :, s * LANES:(s + 1) * LANES] = blk.astype(u_ref.dtype)


def _norm_mod_s5_call(x2, g, mod, b, t):
    n = x2.shape[0]
    tm = _pick_tile(t, 512)
    tpb = t // tm
    nkb = tm // S5_L
    return pl.pallas_call(
        functools.partial(_nm_s5_kernel, nkb=nkb),
        grid=(n // tm,),
        in_specs=[
            pl.BlockSpec((tm, D_MODEL), lambda i: (i, 0)),
            pl.BlockSpec((1, D_MODEL), lambda i: (0, 0)),
            pl.BlockSpec((1, ADA_CHUNKS, D_MODEL), lambda i: (i // tpb, 0, 0)),
        ],
        out_specs=[
            pl.BlockSpec((tm, D_MODEL), lambda i: (i, 0)),
            pl.BlockSpec((N_SLAB, nkb, S5_K), lambda i: (0, i % tpb, i // tpb)),
        ],
        out_shape=[
            jax.ShapeDtypeStruct((n, D_MODEL), BF16),
            jax.ShapeDtypeStruct((N_SLAB, t // S5_L, b * S5_K), BF16),
        ],
        scratch_shapes=[pltpu.VMEM((N_SLAB, tm, LANES), F32)],
        compiler_params=_params(1),
        name="norm_mod_s5",
    )(x2, g.reshape(1, D_MODEL), mod)


def _s5_kernel(uc_ref, ulf_ref, ulb_ref, tf_ref, tb_ref, wf_ref, wb_ref, vf_ref, vb_ref, coef_ref,
               ycf_ref, ycb_ref, ylf_ref, ylb_ref, u_s, s_s, x_s, carry_s, *, nb, nblk):
    i = pl.program_id(1)

    @pl.when(i == 0)
    def _():
        carry_s[...] = jnp.zeros_like(carry_s)
        u_s[0] = uc_ref[0]
        u_s[1] = uc_ref[0]

    @pl.when(i > 0)
    def _():
        u_s[0] = ulf_ref[0]
        u_s[1] = ulb_ref[0]

    ys = []
    for d, (t_ref, w_ref, v_ref) in enumerate(((tf_ref, wf_ref, vf_ref), (tb_ref, wb_ref, vb_ref))):
        u = u_s[d]
        s_s[d] = jnp.dot(u, w_ref[0], preferred_element_type=F32)
        a_r = coef_ref[0, 2 * d:2 * d + 1, :]
        a_i = coef_ref[0, 2 * d + 1:2 * d + 2, :]

        def body(k, carry, d=d, a_r=a_r, a_i=a_i):
            x_r, x_i = carry
            kk = k if d == 0 else nblk - 1 - k
            r = pl.multiple_of(kk * nb, nb)
            x_s[d, pl.ds(r, nb), 0:S5_HALF] = x_r
            x_s[d, pl.ds(r, nb), S5_HALF:2 * S5_HALF] = x_i
            s_r = s_s[d, pl.ds(r, nb), 0:S5_HALF]
            s_i = s_s[d, pl.ds(r, nb), S5_HALF:2 * S5_HALF]
            return a_r * x_r - a_i * x_i + s_r, a_r * x_i + a_i * x_r + s_i

        x_r, x_i = lax.fori_loop(0, nblk, body, (carry_s[d, :, 0:S5_HALF], carry_s[d, :, S5_HALF:2 * S5_HALF]))
        carry_s[d, :, 0:S5_HALF] = x_r
        carry_s[d, :, S5_HALF:2 * S5_HALF] = x_i
        y = jnp.dot(u, t_ref[0], preferred_element_type=F32)
        y = y + jnp.dot(x_s[d].astype(BF16), v_ref[0], preferred_element_type=F32)
        ys.append(y.astype(BF16))

    @pl.when(i == 0)
    def _():
        ycf_ref[0] = ys[0]
        ycb_ref[0] = ys[1]

    @pl.when(i > 0)
    def _():
        ylf_ref[0] = ys[0]
        ylb_ref[0] = ys[1]


def _s5_call(u_c, u_l, wf, wb, nb):
    rows = u_c.shape[1]
    n_lat = u_l.shape[1] // rows
    nblk = rows // nb
    coef = jnp.concatenate([wf[3], wb[3], jnp.zeros((N_SLAB, 4, S5_HALF), F32)], axis=1)
    tile = lambda fn: pl.BlockSpec((1, rows, S5_K), fn)
    wspec = lambda a: pl.BlockSpec((1,) + a.shape[1:], lambda j, i: (j, 0, 0))
    ctx_map = lambda j, i: (j, 0, 0)
    fwd_map = lambda j, i: (j, jnp.maximum(i - 1, 0), 0)
    bwd_map = lambda j, i: (j, jnp.minimum(n_lat - i, n_lat - 1), 0)
    yc = jax.ShapeDtypeStruct(u_c.shape, BF16)
    yl = jax.ShapeDtypeStruct(u_l.shape, BF16)
    return pl.pallas_call(
        functools.partial(_s5_kernel, nb=nb, nblk=nblk),
        grid=(N_SLAB, n_lat + 1),
        in_specs=[tile(ctx_map), tile(fwd_map), tile(bwd_map),
                  wspec(wf[0]), wspec(wb[0]), wspec(wf[1]), wspec(wb[1]), wspec(wf[2]), wspec(wb[2]), wspec(coef)],
        out_specs=[tile(ctx_map), tile(ctx_map), tile(fwd_map), tile(bwd_map)],
        out_shape=[yc, yc, yl, yl],
        scratch_shapes=[
            pltpu.VMEM((2, rows, S5_K), BF16),
            pltpu.VMEM((2, rows, 2 * S5_HALF), F32),
            pltpu.VMEM((2, rows, 2 * S5_HALF), F32),
            pltpu.VMEM((2, nb, 2 * S5_HALF), F32),
        ],
        compiler_params=_params(2),
        name="s5_scan",
    )(u_c, u_l, u_l, wf[0], wb[0], wf[1], wb[1], wf[2], wb[2], coef)


def _s5_out_kernel(x_ref, yf_ref, yb_ref, h_ref, d_ref, w_ref, mod_ref, o_ref, ys_s, *, nkb):
    for s in range(S5_L):
        for j in range(N_SLAB):
            lanes = slice(s * LANES, (s + 1) * LANES)
            blk = yf_ref[j, :, lanes].astype(F32) + yb_ref[j, :, lanes].astype(F32)
            ys_s[j, pl.ds(s, nkb, stride=S5_L), :] = blk
    y = jnp.concatenate([ys_s[j] for j in range(N_SLAB)], axis=1) + d_ref[...] * h_ref[...].astype(F32)
    g = 0.5 * y * (1.0 + jnp.tanh(math.sqrt(2.0 / math.pi) * (y + 0.044715 * (y * y * y))))
    z = jnp.dot(g.astype(BF16), w_ref[...], preferred_element_type=F32)
    out = z[:, :D_MODEL] * _sigmoid(z[:, D_MODEL:])
    o_ref[...] = x_ref[...] + mod_ref[0, 2:3, :] * out


def _s5_out_call(x2, y_f, y_b, h, d_skip, w_glu, mod, t):
    n = x2.shape[0]
    tm = _pick_tile(t, 512)
    tpb = t // tm
    nkb = tm // S5_L
    tile = pl.BlockSpec((tm, D_MODEL), lambda i: (i, 0))
    ytile = pl.BlockSpec((N_SLAB, nkb, S5_K), lambda i: (0, i % tpb, i // tpb))
    return pl.pallas_call(
        functools.partial(_s5_out_kernel, nkb=nkb),
        grid=(n // tm,),
        in_specs=[
            tile, ytile, ytile, tile,
            pl.BlockSpec((1, D_MODEL), lambda i: (0, 0)),
            pl.BlockSpec(w_glu.shape, lambda i: (0, 0)),
            pl.BlockSpec((1, ADA_CHUNKS, D_MODEL), lambda i: (i // tpb, 0, 0)),
        ],
        out_specs=tile,
        out_shape=jax.ShapeDtypeStruct((n, D_MODEL), F32),
        scratch_shapes=[pltpu.VMEM((N_SLAB, tm, LANES), F32)],
        compiler_params=_params(1),
        name="s5_out",
    )(x2, y_f, y_b, h, d_skip.reshape(1, D_MODEL).astype(F32), w_glu, mod)


def _even_layer(x2, xc2, mod_l, mod_c, b, t, tc, need_ctx, p):
    w_in = jnp.concatenate([p['w_in'][:, :MLA_IN], jnp.zeros((D_MODEL, MLA_IN_PAD - MLA_IN), F32),
                            p['w_in'][:, MLA_IN:]], axis=1).astype(BF16)
    z_c = _norm_mod_matmul_call(xc2, p['norm1_g'], mod_c, tc, w_in)
    z_l = _norm_mod_matmul_call(x2, p['norm1_g'], mod_l, t, w_in)
    wts = _mla_weights(p['q_norm_g'], p['w_uq'], p['kv_norm_g'], p['w_ukv'], p['qn_g'], p['kn_g'])
    q_c, k_c, v_c = _mla_prep_call(z_c, tc, wts, *_mla_tables(tc, False))
    q_l, k_l, v_l = _mla_prep_call(z_l, t, wts, *_mla_tables(t, True))
    a_l = _attn_call(q_l, [k_c, k_l], [v_c, v_l], b, t, [tc, t])
    lg = jnp.stack([jnp.log1p(-jnp.exp2(p['lg_f'].astype(F32))), jnp.log1p(-jnp.exp2(p['lg_b'].astype(F32)))])
    zero = jnp.zeros((b, N_PAIR, LANES, LANES), F32)
    of_c, ob_c, s_cf, s_cb = _ret_call(z_c, lg, _ret_tables(tc, False), zero, zero, b, tc)
    of_l, ob_l, _, _ = _ret_call(z_l, lg, _ret_tables(t, True), s_cf, s_cb, b, t)
    w_out = p['w_out'].astype(BF16)
    x2 = _mix_out_call(x2, a_l, of_l, ob_l, z_l, w_out, mod_l, t)
    if need_ctx:
        a_c = _attn_call(q_c, [k_c], [v_c], b, tc, [tc])
        xc2 = _mix_out_call(xc2, a_c, of_c, ob_c, z_c, w_out, mod_c, tc)
    return x2, xc2


def _odd_layer(x2, xc2, mod_l, mod_c, b, t, tc, need_ctx, p):
    h_c, u_c = _norm_mod_s5_call(xc2, p['norm1_g'], mod_c, b, tc)
    h_l, u_l = _norm_mod_s5_call(x2, p['norm1_g'], mod_l, b, t)
    wf = _s5_weights(*p['s5_f'], backward=False)
    wb = _s5_weights(*p['s5_b'], backward=True)
    nk_c, nk_l = tc // S5_L, t // S5_L
    ycf, ycb, ylf, ylb = _s5_call(u_c.reshape(N_SLAB, nk_c * b, S5_K), u_l.reshape(N_SLAB, nk_l * b, S5_K), wf, wb, b)
    w_glu = p['w_glu'].astype(BF16)
    block_major = lambda y, nk: y.reshape(N_SLAB, nk, b * S5_K)
    x2 = _s5_out_call(x2, block_major(ylf, nk_l), block_major(ylb, nk_l), h_l, p['d_skip'], w_glu, mod_l, t)
    if need_ctx:
        xc2 = _s5_out_call(xc2, block_major(ycf, nk_c), block_major(ycb, nk_c), h_c, p['d_skip'], w_glu, mod_c, tc)
    return x2, xc2


def kernel(x, c, ctx, c_ctx, ada_w, ada_b, norm1_g, norm2_g, mlp_w1, mlp_w2, w_in, mla_q_norm_g, mla_w_uq, mla_kv_norm_g, mla_w_ukv, mla_qn_g, mla_kn_g, ret_lg_f, ret_lg_b, w_out, s5_a_re_f, s5_a_im_f, s5_b_re_f, s5_b_im_f, s5_c_re_f, s5_c_im_f, s5_log_dt_f, s5_a_re_b, s5_a_im_b, s5_b_re_b, s5_b_im_b, s5_c_re_b, s5_c_im_b, s5_log_dt_b, s5_d, s5_w_glu):
    b, t, _ = x.shape
    tc = ctx.shape[1]
    depth = ada_w.shape[0]
    assert b % 8 == 0 and tc % (16 * S5_L) == 0 and t % tc == 0
    rows = -(-(b + 1) // 8) * 8
    cc = jnp.zeros((rows, D_MODEL), F32).at[:b].set(c.astype(F32)).at[b].set(c_ctx.astype(F32))
    mod = _ada_all(cc, ada_w.astype(F32), ada_b.astype(F32))
    x2 = x.reshape(b * t, D_MODEL).astype(F32)
    xc2 = ctx.reshape(b * tc, D_MODEL).astype(F32)
    for l in range(depth):
        need_ctx = l < depth - 1
        mod_l = mod[l, :b].reshape(b, ADA_CHUNKS, D_MODEL)
        mod_c = jnp.broadcast_to(mod[l, b].reshape(1, ADA_CHUNKS, D_MODEL), (b, ADA_CHUNKS, D_MODEL))
        if l % 2 == 0:
            e = l // 2
            p = dict(norm1_g=norm1_g[l], w_in=w_in[e], q_norm_g=mla_q_norm_g[e], w_uq=mla_w_uq[e],
                     kv_norm_g=mla_kv_norm_g[e], w_ukv=mla_w_ukv[e], qn_g=mla_qn_g[e], kn_g=mla_kn_g[e],
                     lg_f=ret_lg_f[e], lg_b=ret_lg_b[e], w_out=w_out[e])
            x2, xc2 = _even_layer(x2, xc2, mod_l, mod_c, b, t, tc, need_ctx, p)
        else:
            o = l // 2
            p = dict(norm1_g=norm1_g[l], d_skip=s5_d[o], w_glu=s5_w_glu[o],
                     s5_f=(s5_a_re_f[o], s5_a_im_f[o], s5_b_re_f[o], s5_b_im_f[o], s5_c_re_f[o], s5_c_im_f[o], s5_log_dt_f[o]),
                     s5_b=(s5_a_re_b[o], s5_a_im_b[o], s5_b_re_b[o], s5_b_im_b[o], s5_c_re_b[o], s5_c_im_b[o], s5_log_dt_b[o]))
            x2, xc2 = _odd_layer(x2, xc2, mod_l, mod_c, b, t, tc, need_ctx, p)
        w1 = mlp_w1[l].astype(BF16)
        w2 = mlp_w2[l].astype(BF16)
        x2 = _mlp_call(x2, norm2_g[l], mod_l, w1, w2, t)
        if need_ctx:
            xc2 = _mlp_call(xc2, norm2_g[l], mod_c, w1, w2, tc)
    return x2.reshape(b, t, D_MODEL).astype(x.dtype)
```

```python
import functools
import math

import jax
import jax.numpy as jnp
import numpy as np
from jax import lax
from jax.experimental import pallas as pl
from jax.experimental.pallas import tpu as pltpu

F32 = jnp.float32
BF16 = jnp.bfloat16

D_MODEL = 1024
EPS = 1e-6
ADA_CHUNKS = 6
GRID_W = 64
ROPE_BASE = 10000.0
LANES = 128
N_SLAB = D_MODEL // LANES

H_A = 8
Q_LORA = 256
KV_LORA = 128
NOPE_DIM = 64
ROPE_DIM = 32
QK_DIM = NOPE_DIM + ROPE_DIM
V_DIM = 64
HEAD_SLOT = LANES
MLA_IN = Q_LORA + KV_LORA + ROPE_DIM
MLA_IN_PAD = 512
KEY_CHUNK = 256

H_R = 8
DK_R = 64
DV_R = 64
RET_W = H_R * DK_R
N_PAIR = H_R // 2

S5_GROUP = 16
S5_GROUPS = D_MODEL // S5_GROUP
S5_STATE = 64
S5_L = 8
S5_K = S5_L * LANES
S5_GPS = LANES // S5_GROUP
S5_HALF = S5_GPS * S5_STATE
S5_TOK = 32

D_FF = 4 * D_MODEL
Z_WIDTH = MLA_IN_PAD + 4 * RET_W

VMEM_LIMIT = 56 * 1024 * 1024

_NT = (((1,), (1,)), ((), ()))
_HP = lax.Precision.HIGHEST


def _params(n_grid):
    return pltpu.CompilerParams(
        dimension_semantics=("arbitrary",) * n_grid, vmem_limit_bytes=VMEM_LIMIT)


def _sigmoid(x):
    return 1.0 / (1.0 + jnp.exp(-x))


def _pick_tile(t, pref):
    tile = min(t, pref)
    while t % tile:
        tile //= 2
    return tile


def _ada_kernel(c_ref, w_ref, b_ref, o_ref):
    cc = c_ref[...]
    s = cc * _sigmoid(cc)
    o_ref[0] = jnp.dot(s, w_ref[0], preferred_element_type=F32, precision=_HP) + b_ref[0]


def _ada_all(cc, ada_w, ada_b):
    depth, _, width = ada_w.shape
    r = cc.shape[0]
    tn = 1536
    return pl.pallas_call(
        _ada_kernel,
        grid=(depth, width // tn),
        in_specs=[
            pl.BlockSpec((r, D_MODEL), lambda l, j: (0, 0)),
            pl.BlockSpec((1, D_MODEL, tn), lambda l, j: (l, 0, j)),
            pl.BlockSpec((1, 1, tn), lambda l, j: (l, 0, j)),
        ],
        out_specs=pl.BlockSpec((1, r, tn), lambda l, j: (l, 0, j)),
        out_shape=jax.ShapeDtypeStruct((depth, r, width), F32),
        compiler_params=_params(2),
        name="ada_mod",
    )(cc, ada_w, ada_b.reshape(depth, 1, width))


def _norm_mod(x, g_ref, mod_ref, shift_row, scale_row):
    ms = jnp.mean(x * x, axis=-1, keepdims=True)
    y = x * lax.rsqrt(ms + EPS) * g_ref[...]
    return y * (1.0 + mod_ref[0, scale_row:scale_row + 1, :]) + mod_ref[0, shift_row:shift_row + 1, :]


def _nmm_kernel(x_ref, g_ref, mod_ref, w_ref, o_ref):
    h = _norm_mod(x_ref[...], g_ref, mod_ref, 0, 1).astype(BF16)
    o_ref[...] = jnp.dot(h, w_ref[...], preferred_element_type=F32).astype(o_ref.dtype)


def _norm_mod_matmul_call(x2, g, mod, t, w):
    n = x2.shape[0]
    tm = _pick_tile(t, 512)
    tpb = t // tm
    n_out = w.shape[1]
    return pl.pallas_call(
        _nmm_kernel,
        grid=(n // tm,),
        in_specs=[
            pl.BlockSpec((tm, D_MODEL), lambda i: (i, 0)),
            pl.BlockSpec((1, D_MODEL), lambda i: (0, 0)),
            pl.BlockSpec((1, ADA_CHUNKS, D_MODEL), lambda i: (i // tpb, 0, 0)),
            pl.BlockSpec((D_MODEL, n_out), lambda i: (0, 0)),
        ],
        out_specs=pl.BlockSpec((tm, n_out), lambda i: (i, 0)),
        out_shape=jax.ShapeDtypeStruct((n, n_out), BF16),
        compiler_params=_params(1),
        name="norm_mod_w_in",
    )(x2, g.reshape(1, D_MODEL), mod, w)


def _mla_prep_kernel(z_ref, gq_ref, gkv_ref, wq_ref, wqs_ref, wkc_ref, wkcs_ref, wkr_ref, wkrs_ref,
                     wvl_ref, wvh_ref, gqn_ref, gqns_ref, gkn_ref, gkns_ref, cos_ref, sin_ref,
                     q_ref, k_ref, vl_ref, vh_ref):
    cq = z_ref[:, 0:Q_LORA].astype(F32)
    ckv = z_ref[:, Q_LORA:Q_LORA + KV_LORA].astype(F32)
    kr = z_ref[:, Q_LORA + KV_LORA:MLA_IN_PAD]
    cqn = (cq * lax.rsqrt(jnp.mean(cq * cq, axis=-1, keepdims=True) + EPS) * gq_ref[...]).astype(BF16)
    ckn = (ckv * lax.rsqrt(jnp.mean(ckv * ckv, axis=-1, keepdims=True) + EPS) * gkv_ref[...]).astype(BF16)
    q = jnp.dot(cqn, wq_ref[...], preferred_element_type=F32)
    qs = jnp.dot(cqn, wqs_ref[...], preferred_element_type=F32)
    k = jnp.dot(ckn, wkc_ref[...], preferred_element_type=F32) + jnp.dot(kr, wkr_ref[...], preferred_element_type=F32)
    ks = jnp.dot(ckn, wkcs_ref[...], preferred_element_type=F32) + jnp.dot(kr, wkrs_ref[...], preferred_element_type=F32)
    vl_ref[...] = jnp.dot(ckn, wvl_ref[...], preferred_element_type=F32).astype(vl_ref.dtype)
    vh_ref[...] = jnp.dot(ckn, wvh_ref[...], preferred_element_type=F32).astype(vh_ref.dtype)
    cos = cos_ref[...]
    sin = sin_ref[...]
    q_scale = QK_DIM ** -0.5 * math.log2(math.e)
    for h in range(H_A):
        sl = slice(h * HEAD_SLOT, (h + 1) * HEAD_SLOT)
        qh = q[:, sl]
        rq = lax.rsqrt(jnp.sum(qh * qh, axis=-1, keepdims=True) * (1.0 / QK_DIM) + EPS)
        q_rot = (qh * gqn_ref[...] * cos + qs[:, sl] * gqns_ref[...] * sin) * (rq * q_scale)
        q_ref[:, sl] = q_rot.astype(q_ref.dtype)
        kh = k[:, sl]
        rk = lax.rsqrt(jnp.sum(kh * kh, axis=-1, keepdims=True) * (1.0 / QK_DIM) + EPS)
        k_rot = (kh * gkn_ref[...] * cos + ks[:, sl] * gkns_ref[...] * sin) * rk
        k_ref[:, sl] = k_rot.astype(k_ref.dtype)


def _mla_prep_call(z, t, wts, cos_t, sin_t):
    n = z.shape[0]
    tm = _pick_tile(t, 512)
    tpb = t // tm
    full = lambda a: pl.BlockSpec(a.shape, lambda i: (0,) * a.ndim)
    in_specs = [pl.BlockSpec((tm, MLA_IN_PAD), lambda i: (i, 0))] + [full(a) for a in wts] + [
        pl.BlockSpec((tm, HEAD_SLOT), lambda i: (i % tpb, 0)),
        pl.BlockSpec((tm, HEAD_SLOT), lambda i: (i % tpb, 0)),
    ]
    hw = H_A * HEAD_SLOT
    vw = H_A * V_DIM
    return pl.pallas_call(
        _mla_prep_kernel,
        grid=(n // tm,),
        in_specs=in_specs,
        out_specs=[
            pl.BlockSpec((tm, hw), lambda i: (i, 0)),
            pl.BlockSpec((tm, hw), lambda i: (i, 0)),
            pl.BlockSpec((tm, vw), lambda i: (i, 0)),
            pl.BlockSpec((tm, vw), lambda i: (i, 0)),
        ],
        out_shape=[
            jax.ShapeDtypeStruct((n, hw), BF16),
            jax.ShapeDtypeStruct((n, hw), BF16),
            jax.ShapeDtypeStruct((n, vw), BF16),
            jax.ShapeDtypeStruct((n, vw), BF16),
        ],
        compiler_params=_params(1),
        name="mla_prep",
    )(z, *wts, cos_t, sin_t)


def _mla_weights(q_norm_g, w_uq, kv_norm_g, w_ukv, qn_g, kn_g):
    pad = HEAD_SLOT - QK_DIM
    half = ROPE_DIM // 2
    perm = jnp.arange(HEAD_SLOT)
    perm = perm.at[NOPE_DIM:NOPE_DIM + half].set(jnp.arange(NOPE_DIM + half, NOPE_DIM + ROPE_DIM))
    perm = perm.at[NOPE_DIM + half:NOPE_DIM + ROPE_DIM].set(jnp.arange(NOPE_DIM, NOPE_DIM + half))

    def slots(w):
        wp = jnp.pad(w, ((0, 0), (0, 0), (0, pad)))
        return wp, wp[:, :, perm]

    def flat(w):
        return w.reshape(w.shape[0], H_A * HEAD_SLOT).astype(BF16)

    wq, wqs = slots(w_uq.reshape(Q_LORA, H_A, QK_DIM))
    w_kv = w_ukv.reshape(KV_LORA, H_A, NOPE_DIM + V_DIM)
    wkc, wkcs = slots(jnp.pad(w_kv[:, :, :NOPE_DIM], ((0, 0), (0, 0), (0, ROPE_DIM))))
    eye = jnp.zeros((HEAD_SLOT, QK_DIM), F32).at[jnp.arange(ROPE_DIM), NOPE_DIM + jnp.arange(ROPE_DIM)].set(1.0)
    wkr, wkrs = slots(jnp.broadcast_to(eye[:, None, :], (HEAD_SLOT, H_A, QK_DIM)))
    wv = w_kv[:, :, NOPE_DIM:]
    even = (jnp.arange(H_A) % 2 == 0)[None, :, None]
    wvl = jnp.where(even, wv, 0.0).reshape(KV_LORA, H_A * V_DIM).astype(BF16)
    wvh = jnp.where(even, 0.0, wv).reshape(KV_LORA, H_A * V_DIM).astype(BF16)

    def gains(g):
        gp = jnp.pad(g.astype(F32), (0, pad))
        return gp.reshape(1, HEAD_SLOT), gp[perm].reshape(1, HEAD_SLOT)

    gqn, gqns = gains(qn_g)
    gkn, gkns = gains(kn_g)
    return [q_norm_g.reshape(1, Q_LORA).astype(F32), kv_norm_g.reshape(1, KV_LORA).astype(F32),
            flat(wq), flat(wqs), flat(wkc), flat(wkcs), flat(wkr), flat(wkrs), wvl, wvh, gqn, gqns, gkn, gkns]


def _mla_tables(n_tok, rotary):
    cos = jnp.ones((n_tok, HEAD_SLOT), F32)
    sin = jnp.zeros((n_tok, HEAD_SLOT), F32)
    if rotary:
        rows = n_tok // GRID_W
        r = jnp.repeat(jnp.arange(rows, dtype=F32), GRID_W)
        col = jnp.tile(jnp.arange(GRID_W, dtype=F32), rows)
        nf = ROPE_DIM // 4
        f = ROPE_BASE ** (-jnp.arange(nf, dtype=F32) / nf)
        ang = jnp.concatenate([r[:, None] * f, col[:, None] * f], axis=-1)
        c, s = jnp.cos(ang), jnp.sin(ang)
        half = ROPE_DIM // 2
        cos = cos.at[:, NOPE_DIM:NOPE_DIM + half].set(c).at[:, NOPE_DIM + half:NOPE_DIM + ROPE_DIM].set(c)
        sin = sin.at[:, NOPE_DIM:NOPE_DIM + half].set(-s).at[:, NOPE_DIM + half:NOPE_DIM + ROPE_DIM].set(s)
    return cos, sin


def _attn_kernel(*refs, seg_lens):
    nseg = len(seg_lens)
    q_ref = refs[0]
    k_refs = refs[1:1 + nseg]
    vl_refs = refs[1 + nseg:1 + 2 * nseg]
    vh_refs = refs[1 + 2 * nseg:1 + 3 * nseg]
    o_ref = refs[1 + 3 * nseg]
    s_s = refs[2 + 3 * nseg]
    chunks = [(g, r) for g, n in enumerate(seg_lens) for r in range(0, n, KEY_CHUNK)]
    halves = KEY_CHUNK // LANES

    def lane_fold(x, op):
        out = x[:, 0:LANES]
        for i in range(1, halves):
            out = op(out, x[:, i * LANES:(i + 1) * LANES])
        return out

    for j in range(H_A // 2):
        vsl = slice(j * LANES, (j + 1) * LANES)
        acc = None
        for e in range(2):
            h = 2 * j + e
            sl = slice(h * HEAD_SLOT, (h + 1) * HEAD_SLOT)
            v_refs = vl_refs if e == 0 else vh_refs
            qh = q_ref[:, sl]
            mx = None
            for c, (g, r) in enumerate(chunks):
                s = lax.dot_general(qh, k_refs[g][r:r + KEY_CHUNK, sl], _NT, preferred_element_type=F32)
                s_s[c] = s
                cm = lane_fold(s, jnp.maximum)
                mx = cm if mx is None else jnp.maximum(mx, cm)
            m = jnp.max(mx, axis=-1, keepdims=True)
            lsum = None
            o = None
            for c, (g, r) in enumerate(chunks):
                p = jnp.exp2(s_s[c] - m)
                ps = lane_fold(p, jnp.add)
                lsum = ps if lsum is None else lsum + ps
                t = jnp.dot(p.astype(BF16), v_refs[g][r:r + KEY_CHUNK, vsl], preferred_element_type=F32)
                o = t if o is None else o + t
            o = o * (1.0 / jnp.sum(lsum, axis=-1, keepdims=True))
            acc = o if acc is None else acc + o
        o_ref[:, vsl] = acc.astype(o_ref.dtype)


def _attn_call(q, ks, vls, vhs, t, tks):
    n = q.shape[0]
    tq = _pick_tile(t, 256)
    tpb = t // tq
    hw = H_A * HEAD_SLOT
    vw = H_A * V_DIM
    n_chunks = sum(tk // KEY_CHUNK for tk in tks)
    in_specs = [pl.BlockSpec((tq, hw), lambda i: (i, 0))]
    once = pl.Buffered(1)
    in_specs += [pl.BlockSpec((tk, hw), lambda i: (i // tpb, 0), pipeline_mode=once) for tk in tks]
    in_specs += [pl.BlockSpec((tk, vw), lambda i: (i // tpb, 0), pipeline_mode=once) for tk in tks] * 2
    return pl.pallas_call(
        functools.partial(_attn_kernel, seg_lens=tuple(tks)),
        grid=(n // tq,),
        in_specs=in_specs,
        out_specs=pl.BlockSpec((tq, vw), lambda i: (i, 0)),
        out_shape=jax.ShapeDtypeStruct((n, vw), BF16),
        scratch_shapes=[pltpu.VMEM((n_chunks, tq, KEY_CHUNK), F32)],
        compiler_params=_params(1),
        name="attention_%dseg" % len(tks),
    )(q, *ks, *vls, *vhs)


def _rotate_pairs(x, cos, sin_lo, sin_hi):
    return x * cos + pltpu.roll(x, LANES - DK_R // 2, 1) * sin_lo + pltpu.roll(x, DK_R // 2, 1) * sin_hi


def _ret_kernel(lg_ref, qf_ref, kf_ref, vf_ref, cf_ref, slf_ref, shf_ref,
                qb_ref, kb_ref, vb_ref, cb_ref, slb_ref, shb_ref, s0f_ref, s0b_ref,
                of_ref, ob_ref, sff_ref, sfb_ref, df_s, db_s, sf_s, sb_s, *, chunk):
    b = pl.program_id(0)
    c = pl.program_id(1)
    nc = pl.num_programs(1)

    @pl.when(jnp.logical_and(b == 0, c == 0))
    def _():
        ii = lax.broadcasted_iota(jnp.int32, (chunk, chunk), 0)
        jj = lax.broadcasted_iota(jnp.int32, (chunk, chunk), 1)
        diff = (ii - jj).astype(F32)
        for h in range(H_R):
            df_s[h] = jnp.where(ii >= jj, jnp.exp(jnp.where(ii >= jj, diff, 0.0) * lg_ref[0, h]), 0.0)
            db_s[h] = jnp.where(jj > ii, jnp.exp(jnp.where(jj > ii, -diff, 0.0) * lg_ref[1, h]), 0.0)

    @pl.when(c == 0)
    def _():
        sf_s[...] = s0f_ref[0]
        sb_s[...] = s0b_ref[0]

    lane = lax.broadcasted_iota(jnp.int32, (1, LANES), 1)
    lo = lane < DK_R
    row = lax.broadcasted_iota(jnp.int32, (LANES, LANES), 0)
    colm = lax.broadcasted_iota(jnp.int32, (LANES, LANES), 1)
    blockdiag = (row < DK_R) == (colm < DK_R)
    pos = lax.broadcasted_iota(jnp.int32, (chunk, 1), 0).astype(F32)

    def one_direction(d, q_ref, k_ref, v_ref, cos_ref, sl_ref, sh_ref, d_s, s_s, o_ref):
        cos, sin_lo, sin_hi = cos_ref[...], sl_ref[...], sh_ref[...]
        for j in range(N_PAIR):
            sl = slice(j * LANES, (j + 1) * LANES)
            lg = jnp.where(lo, lg_ref[d, 2 * j], lg_ref[d, 2 * j + 1])
            q2 = _rotate_pairs(q_ref[:, sl].astype(F32), cos, sin_lo, sin_hi)
            k2 = _rotate_pairs(k_ref[:, sl].astype(F32), cos, sin_lo, sin_hi) * (DK_R ** -0.5)
            v2 = v_ref[:, sl]
            if d == 0:
                q_dec = jnp.exp((pos + 1.0) * lg)
                k_dec = jnp.exp((chunk - 1.0 - pos) * lg)
            else:
                q_dec = jnp.exp((chunk - pos) * lg)
                k_dec = jnp.exp(pos * lg)
            c_dec = jnp.exp(chunk * lg)
            s2 = s_s[j]
            o = jnp.dot((q2 * q_dec).astype(BF16), s2.astype(BF16), preferred_element_type=F32)
            k2b = k2.astype(BF16)
            for e in range(2):
                mask = lo if e == 0 else jnp.logical_not(lo)
                qe = jnp.where(mask, q2, 0.0).astype(BF16)
                sc = lax.dot_general(qe, k2b, _NT, preferred_element_type=F32) * d_s[2 * j + e]
                ve = jnp.where(mask, v2, jnp.zeros((), BF16))
                o = o + jnp.dot(sc.astype(BF16), ve, preferred_element_type=F32)
            o_ref[:, sl] = o.astype(o_ref.dtype)
            kd_t = (k2 * k_dec).T.astype(BF16)
            upd = jnp.dot(kd_t, v2, preferred_element_type=F32)
            s_s[j] = s2 * c_dec + jnp.where(blockdiag, upd, 0.0)

    one_direction(0, qf_ref, kf_ref, vf_ref, cf_ref, slf_ref, shf_ref, df_s, sf_s, of_ref)
    one_direction(1, qb_ref, kb_ref, vb_ref, cb_ref, slb_ref, shb_ref, db_s, sb_s, ob_ref)

    @pl.when(c == nc - 1)
    def _():
        sff_ref[0] = sf_s[...]
        sfb_ref[0] = sb_s[...]


def _ret_call(z, lg, tables, s0f, s0b, b, t):
    n = z.shape[0]
    chunk = _pick_tile(t, 256)
    nc = t // chunk
    cos_t, sin_lo_t, sin_hi_t = tables
    col0 = MLA_IN_PAD // RET_W

    def zspec(part, rev):
        if rev:
            return pl.BlockSpec((chunk, RET_W), lambda bi, ci: (bi * nc + nc - 1 - ci, col0 + part))
        return pl.BlockSpec((chunk, RET_W), lambda bi, ci: (bi * nc + ci, col0 + part))

    def tspec(rev):
        if rev:
            return pl.BlockSpec((chunk, LANES), lambda bi, ci: (nc - 1 - ci, 0))
        return pl.BlockSpec((chunk, LANES), lambda bi, ci: (ci, 0))

    st_spec = pl.BlockSpec((1, N_PAIR, LANES, LANES), lambda bi, ci: (bi, 0, 0, 0))
    in_specs = [pl.BlockSpec(memory_space=pltpu.SMEM)]
    in_specs += [zspec(0, False), zspec(1, False), zspec(2, False), tspec(False), tspec(False), tspec(False)]
    in_specs += [zspec(0, True), zspec(1, True), zspec(2, True), tspec(True), tspec(True), tspec(True)]
    in_specs += [st_spec, st_spec]
    out_specs = [
        pl.BlockSpec((chunk, RET_W), lambda bi, ci: (bi * nc + ci, 0)),
        pl.BlockSpec((chunk, RET_W), lambda bi, ci: (bi * nc + nc - 1 - ci, 0)),
        st_spec, st_spec,
    ]
    st_shape = jax.ShapeDtypeStruct((b, N_PAIR, LANES, LANES), F32)
    return pl.pallas_call(
        functools.partial(_ret_kernel, chunk=chunk),
        grid=(b, nc),
        in_specs=in_specs,
        out_specs=out_specs,
        out_shape=[jax.ShapeDtypeStruct((n, RET_W), BF16), jax.ShapeDtypeStruct((n, RET_W), BF16),
                   st_shape, st_shape],
        scratch_shapes=[
            pltpu.VMEM((H_R, chunk, chunk), F32), pltpu.VMEM((H_R, chunk, chunk), F32),
            pltpu.VMEM((N_PAIR, LANES, LANES), F32), pltpu.VMEM((N_PAIR, LANES, LANES), F32),
        ],
        compiler_params=_params(2),
        name="retention",
    )(lg, z, z, z, cos_t, sin_lo_t, sin_hi_t, z, z, z, cos_t, sin_lo_t, sin_hi_t, s0f, s0b)


def _ret_tables(n_tok, rotary):
    cos = jnp.ones((n_tok, LANES), F32)
    sin_lo = jnp.zeros((n_tok, LANES), F32)
    sin_hi = jnp.zeros((n_tok, LANES), F32)
    if rotary:
        nf = DK_R // 2
        theta = ROPE_BASE ** (-jnp.arange(nf, dtype=F32) / nf)
        ang = jnp.arange(n_tok, dtype=F32)[:, None] * theta
        c, s = jnp.cos(ang), jnp.sin(ang)
        z = jnp.zeros_like(s)
        cos = jnp.concatenate([c, c, c, c], axis=-1)
        sin_lo = jnp.concatenate([-s, z, -s, z], axis=-1)
        sin_hi = jnp.concatenate([z, s, z, s], axis=-1)
    return cos, sin_lo, sin_hi


def _mix_out_kernel(x_ref, a_ref, of_ref, ob_ref, g_ref, w_ref, mod_ref, o_ref):
    lane = lax.broadcasted_iota(jnp.int32, (1, LANES), 1)
    lo = lane < DV_R
    y = jnp.dot(a_ref[...], w_ref[0:H_A * V_DIM, :], preferred_element_type=F32)
    for j in range(N_PAIR):
        sl = slice(j * LANES, (j + 1) * LANES)
        o = of_ref[:, sl].astype(F32) + ob_ref[:, sl].astype(F32)
        s_lo = jnp.sum(jnp.where(lo, o, 0.0), axis=-1, keepdims=True)
        s_all = jnp.sum(o, axis=-1, keepdims=True)
        mu = jnp.where(lo, s_lo, s_all - s_lo) * (1.0 / DV_R)
        oc = o - mu
        q = oc * oc
        q_lo = jnp.sum(jnp.where(lo, q, 0.0), axis=-1, keepdims=True)
        q_all = jnp.sum(q, axis=-1, keepdims=True)
        var = jnp.where(lo, q_lo, q_all - q_lo) * (1.0 / DV_R)
        g = g_ref[:, sl].astype(F32)
        r = (oc * lax.rsqrt(var + EPS)) * (g * _sigmoid(g))
        row0 = H_A * V_DIM + j * LANES
        y = y + jnp.dot(r.astype(BF16), w_ref[row0:row0 + LANES, :], preferred_element_type=F32)
    o_ref[...] = x_ref[...] + mod_ref[0, 2:3, :] * y


def _mix_out_call(x2, a, o_f, o_b, z, w_out, mod, t):
    n = x2.shape[0]
    tm = _pick_tile(t, 512)
    tpb = t // tm
    gate_col = MLA_IN_PAD // RET_W + 3
    return pl.pallas_call(
        _mix_out_kernel,
        grid=(n // tm,),
        in_specs=[
            pl.BlockSpec((tm, D_MODEL), lambda i: (i, 0)),
            pl.BlockSpec((tm, H_A * V_DIM), lambda i: (i, 0)),
            pl.BlockSpec((tm, RET_W), lambda i: (i, 0)),
            pl.BlockSpec((tm, RET_W), lambda i: (i, 0)),
            pl.BlockSpec((tm, RET_W), lambda i: (i, gate_col)),
            pl.BlockSpec(w_out.shape, lambda i: (0, 0)),
            pl.BlockSpec((1, ADA_CHUNKS, D_MODEL), lambda i: (i // tpb, 0, 0)),
        ],
        out_specs=pl.BlockSpec((tm, D_MODEL), lambda i: (i, 0)),
        out_shape=jax.ShapeDtypeStruct((n, D_MODEL), F32),
        compiler_params=_params(1),
        name="mix_out",
    )(x2, a, o_f, o_b, z, w_out, mod)


def _mlp_kernel(x_ref, g_ref, mod_ref, w1_ref, w2_ref, o_ref, h_s, acc_s):
    j = pl.program_id(1)

    @pl.when(j == 0)
    def _():
        h_s[...] = _norm_mod(x_ref[...], g_ref, mod_ref, 3, 4).astype(BF16)
        acc_s[...] = jnp.zeros_like(acc_s)

    a = jnp.maximum(jnp.dot(h_s[...], w1_ref[...], preferred_element_type=F32), 0.0)
    acc_s[...] += jnp.dot((a * a).astype(BF16), w2_ref[...], preferred_element_type=F32)

    @pl.when(j == pl.num_programs(1) - 1)
    def _():
        o_ref[...] = x_ref[...] + mod_ref[0, 5:6, :] * acc_s[...]


def _mlp_call(x2, g, mod, w1, w2, t):
    n = x2.shape[0]
    tm = _pick_tile(t, 1024)
    tpb = t // tm
    tf = 512
    return pl.pallas_call(
        _mlp_kernel,
        grid=(n // tm, D_FF // tf),
        in_specs=[
            pl.BlockSpec((tm, D_MODEL), lambda i, j: (i, 0)),
            pl.BlockSpec((1, D_MODEL), lambda i, j: (0, 0)),
            pl.BlockSpec((1, ADA_CHUNKS, D_MODEL), lambda i, j: (i // tpb, 0, 0)),
            pl.BlockSpec((D_MODEL, tf), lambda i, j: (0, j)),
            pl.BlockSpec((tf, D_MODEL), lambda i, j: (j, 0)),
        ],
        out_specs=pl.BlockSpec((tm, D_MODEL), lambda i, j: (i, 0)),
        out_shape=jax.ShapeDtypeStruct((n, D_MODEL), F32),
        scratch_shapes=[pltpu.VMEM((tm, D_MODEL), BF16), pltpu.VMEM((tm, D_MODEL), F32)],
        compiler_params=_params(2),
        name="mlp",
    )(x2, g.reshape(1, D_MODEL), mod, w1, w2)


def _s5_compact_weights(p_f, p_b):
    st = lambda i: jnp.stack([p_f[i].astype(F32), p_b[i].astype(F32)])
    ar, ai, br, bi, cr, ci, log_dt = (st(i) for i in range(7))
    dt = jnp.exp(log_dt)[:, :, None]
    lr, li = dt * ar, dt * ai
    steps = np.arange(S5_L)

    def powers(taus):
        tau = jnp.asarray(np.asarray(taus, np.float32))[:, :, None, None]
        mag = jnp.exp(tau * lr[:, None])
        return mag * jnp.cos(tau * li[:, None]), mag * jnp.sin(tau * li[:, None])

    def c_times(pr, pi):
        c_r, c_i = cr[:, None], ci[:, None]
        pr, pi = pr[:, :, :, None, :], pi[:, :, :, None, :]
        return c_r * pr - c_i * pi, c_r * pi + c_i * pr

    a1r, a1i = powers([[1], [1]])
    nr, ni = a1r[:, 0] - 1.0, a1i[:, 0]
    den = ar * ar + ai * ai
    qr, qi = (nr * ar + ni * ai) / den, (ni * ar - nr * ai) / den
    bbr = qr[..., None] * br - qi[..., None] * bi
    bbi = qr[..., None] * bi + qi[..., None] * br
    dr, di = c_times(*powers([steps, steps]))
    kern = (jnp.einsum('dtghp,dgpk->dtghk', dr, bbr, precision=_HP)
            - jnp.einsum('dtghp,dgpk->dtghk', di, bbi, precision=_HP))
    lag_f = steps[None, :] - steps[:, None]
    toep = jnp.stack([kern[0][np.clip(lag_f, 0, S5_L - 1)], kern[1][np.clip(-lag_f, 0, S5_L - 1)]])
    valid = np.stack([lag_f >= 0, lag_f <= 0])[:, :, :, None, None, None]
    toep = jnp.where(valid, toep, 0.0).reshape(2, S5_L, S5_L, N_SLAB, S5_GPS, S5_GROUP, S5_GROUP)
    tt = toep.transpose(0, 3, 1, 6, 2, 4, 5).reshape(2, N_SLAB, S5_L * S5_GROUP, S5_K)
    wpr, wpi = powers([S5_L - 1 - steps, steps])
    wpr, wpi = wpr[..., None], wpi[..., None]
    w = jnp.stack([wpr * bbr[:, None] - wpi * bbi[:, None], wpr * bbi[:, None] + wpi * bbr[:, None]], axis=1)
    w = w.reshape(2, 2, S5_L, N_SLAB, S5_GPS, S5_STATE, S5_GROUP)
    wt = w.transpose(0, 3, 2, 6, 1, 4, 5).reshape(2, N_SLAB, S5_L * S5_GROUP, 2 * S5_HALF)
    vr, vi = c_times(*powers([steps + 1, S5_L - steps]))
    v = jnp.stack([vr, -vi], axis=1).reshape(2, 2, S5_L, N_SLAB, S5_GPS, S5_GROUP, S5_STATE)
    vt = v.transpose(0, 3, 1, 6, 2, 4, 5).reshape(2, N_SLAB, 2 * S5_STATE, S5_K)
    alr, ali = powers([[S5_L], [S5_L]])
    al = jnp.stack([alr[0, 0], ali[0, 0], alr[1, 0], ali[1, 0]]).reshape(4, N_SLAB, S5_HALF).transpose(1, 0, 2)
    coef = jnp.concatenate([al, jnp.zeros_like(al)], axis=1)
    return tt, wt, vt, coef


def _nm_s5_kernel(x_ref, g_ref, mod_ref, h_ref, u_ref, hs_s, *, nb, tk):
    x = x_ref[...]
    ms = jnp.mean(x * x, axis=-1, keepdims=True)
    y = x * lax.rsqrt(ms + EPS) * g_ref[...]
    h = y * (1.0 + mod_ref[:, 1:2, :]) + mod_ref[:, 0:1, :]
    h_ref[...] = h.astype(h_ref.dtype)
    h2 = h.reshape(nb * tk, D_MODEL)
    for j in range(N_SLAB):
        hs_s[j] = h2[:, j * LANES:(j + 1) * LANES]
    for kk in range(tk // S5_L):
        for s in range(S5_L):
            for j in range(N_SLAB):
                blk = hs_s[j, pl.ds(kk * S5_L + s, nb, stride=tk), :]
                u_ref[j, kk * nb:(kk + 1) * nb, s * LANES:(s + 1) * LANES] = blk.astype(u_ref.dtype)


def _norm_mod_s5_call(x3, g, mod):
    b, t, _ = x3.shape
    tk = S5_TOK
    rows = (tk // S5_L) * b
    return pl.pallas_call(
        functools.partial(_nm_s5_kernel, nb=b, tk=tk),
        grid=(t // tk,),
        in_specs=[
            pl.BlockSpec((b, tk, D_MODEL), lambda i: (0, i, 0)),
            pl.BlockSpec((1, D_MODEL), lambda i: (0, 0)),
            pl.BlockSpec((b, ADA_CHUNKS, D_MODEL), lambda i: (0, 0, 0)),
        ],
        out_specs=[
            pl.BlockSpec((b, tk, D_MODEL), lambda i: (0, i, 0)),
            pl.BlockSpec((N_SLAB, rows, S5_K), lambda i: (0, i, 0)),
        ],
        out_shape=[
            jax.ShapeDtypeStruct((b, t, D_MODEL), BF16),
            jax.ShapeDtypeStruct((N_SLAB, (t // S5_L) * b, S5_K), BF16),
        ],
        scratch_shapes=[pltpu.VMEM((N_SLAB, b * tk, LANES), F32)],
        compiler_params=_params(1),
        name="norm_mod_s5",
    )(x3, g.reshape(1, D_MODEL), mod)


def _s5_kernel(uc_ref, ulf_ref, ulb_ref, tt_ref, wt_ref, vt_ref, coef_ref,
               ycf_ref, ycb_ref, ylf_ref, ylb_ref, t_s, w_s, v_s, u_s, s_s, x_s, carry_s, *, nb, nblk):
    i = pl.program_id(1)

    @pl.when(i == 0)
    def _():
        lane = lax.broadcasted_iota(jnp.int32, (1, S5_K), 1)
        grp_out = (lane // S5_GROUP) % S5_GPS
        grp_state = (lane // S5_STATE) % S5_GPS
        rows = S5_GROUP
        for d in range(2):
            for s in range(S5_L):
                tc = tt_ref[d, 0, s * rows:(s + 1) * rows, :]
                wc = wt_ref[d, 0, s * rows:(s + 1) * rows, :]
                for gi in range(S5_GPS):
                    r0 = s * LANES + gi * rows
                    t_s[d, r0:r0 + rows, :] = jnp.where(grp_out == gi, tc, 0.0).astype(BF16)
                    w_s[d, r0:r0 + rows, :] = jnp.where(grp_state == gi, wc, 0.0).astype(BF16)
            for c in range(2):
                vc = vt_ref[d, 0, c * S5_STATE:(c + 1) * S5_STATE, :]
                for gi in range(S5_GPS):
                    r0 = c * S5_HALF + gi * S5_STATE
                    v_s[d, r0:r0 + S5_STATE, :] = jnp.where(grp_out == gi, vc, 0.0).astype(BF16)
        carry_s[...] = jnp.zeros_like(carry_s)
        u_s[0] = uc_ref[0]
        u_s[1] = uc_ref[0]

    @pl.when(i > 0)
    def _():
        u_s[0] = ulf_ref[0]
        u_s[1] = ulb_ref[0]

    ys = []
    for d in range(2):
        u = u_s[d]
        s_s[d] = jnp.dot(u, w_s[d], preferred_element_type=F32)
        a_r = coef_ref[0, 2 * d:2 * d + 1, :]
        a_i = coef_ref[0, 2 * d + 1:2 * d + 2, :]

        def body(k, carry, d=d, a_r=a_r, a_i=a_i):
            x_r, x_i = carry
            kk = k if d == 0 else nblk - 1 - k
            r = pl.multiple_of(kk * nb, nb)
            x_s[d, pl.ds(r, nb), 0:S5_HALF] = x_r
            x_s[d, pl.ds(r, nb), S5_HALF:2 * S5_HALF] = x_i
            s_r = s_s[d, pl.ds(r, nb), 0:S5_HALF]
            s_i = s_s[d, pl.ds(r, nb), S5_HALF:2 * S5_HALF]
            return a_r * x_r - a_i * x_i + s_r, a_r * x_i + a_i * x_r + s_i

        x_r, x_i = lax.fori_loop(0, nblk, body, (carry_s[d, :, 0:S5_HALF], carry_s[d, :, S5_HALF:2 * S5_HALF]))
        carry_s[d, :, 0:S5_HALF] = x_r
        carry_s[d, :, S5_HALF:2 * S5_HALF] = x_i
        y = jnp.dot(u, t_s[d], preferred_element_type=F32)
        y = y + jnp.dot(x_s[d].astype(BF16), v_s[d], preferred_element_type=F32)
        ys.append(y.astype(BF16))

    @pl.when(i == 0)
    def _():
        ycf_ref[0] = ys[0]
        ycb_ref[0] = ys[1]

    @pl.when(i > 0)
    def _():
        ylf_ref[0] = ys[0]
        ylb_ref[0] = ys[1]


def _s5_call(u_c, u_l, tt, wt, vt, coef, nb):
    rows = u_c.shape[1]
    n_lat = u_l.shape[1] // rows
    nblk = rows // nb
    tile = lambda fn: pl.BlockSpec((1, rows, S5_K), fn)
    wspec = lambda a: pl.BlockSpec((2, 1) + a.shape[2:], lambda j, i: (0, j, 0, 0))
    ctx_map = lambda j, i: (j, 0, 0)
    fwd_map = lambda j, i: (j, jnp.maximum(i - 1, 0), 0)
    bwd_map = lambda j, i: (j, jnp.minimum(n_lat - i, n_lat - 1), 0)
    yc = jax.ShapeDtypeStruct(u_c.shape, BF16)
    yl = jax.ShapeDtypeStruct(u_l.shape, BF16)
    return pl.pallas_call(
        functools.partial(_s5_kernel, nb=nb, nblk=nblk),
        grid=(N_SLAB, n_lat + 1),
        in_specs=[tile(ctx_map), tile(fwd_map), tile(bwd_map), wspec(tt), wspec(wt), wspec(vt),
                  pl.BlockSpec((1,) + coef.shape[1:], lambda j, i: (j, 0, 0))],
        out_specs=[tile(ctx_map), tile(ctx_map), tile(fwd_map), tile(bwd_map)],
        out_shape=[yc, yc, yl, yl],
        scratch_shapes=[
            pltpu.VMEM((2, S5_K, S5_K), BF16),
            pltpu.VMEM((2, S5_K, 2 * S5_HALF), BF16),
            pltpu.VMEM((2, 2 * S5_HALF, S5_K), BF16),
            pltpu.VMEM((2, rows, S5_K), BF16),
            pltpu.VMEM((2, rows, 2 * S5_HALF), F32),
            pltpu.VMEM((2, rows, 2 * S5_HALF), F32),
            pltpu.VMEM((2, nb, 2 * S5_HALF), F32),
        ],
        compiler_params=_params(2),
        name="s5_scan",
    )(u_c, u_l, u_l, tt, wt, vt, coef)


def _s5_out_kernel(x_ref, yf_ref, yb_ref, h_ref, d_ref, w_ref, mod_ref, o_ref, ys_s, *, nb, tk):
    for kk in range(tk // S5_L):
        rows = slice(kk * nb, (kk + 1) * nb)
        for s in range(S5_L):
            lanes = slice(s * LANES, (s + 1) * LANES)
            for j in range(N_SLAB):
                blk = yf_ref[j, rows, lanes].astype(F32) + yb_ref[j, rows, lanes].astype(F32)
                ys_s[j, pl.ds(kk * S5_L + s, nb, stride=tk), :] = blk
    y = jnp.concatenate([ys_s[j] for j in range(N_SLAB)], axis=1)
    y = y + d_ref[...] * h_ref[...].reshape(nb * tk, D_MODEL).astype(F32)
    g = 0.5 * y * (1.0 + jnp.tanh(math.sqrt(2.0 / math.pi) * (y + 0.044715 * (y * y * y))))
    z = jnp.dot(g.astype(BF16), w_ref[...], preferred_element_type=F32)
    out = z[:, :D_MODEL] * _sigmoid(z[:, D_MODEL:])
    o_ref[...] = x_ref[...] + mod_ref[:, 2:3, :] * out.reshape(nb, tk, D_MODEL)


def _s5_out_call(x3, y_f, y_b, h3, d_skip, w_glu, mod):
    b, t, _ = x3.shape
    tk = S5_TOK
    rows = (tk // S5_L) * b
    tile = pl.BlockSpec((b, tk, D_MODEL), lambda i: (0, i, 0))
    ytile = pl.BlockSpec((N_SLAB, rows, S5_K), lambda i: (0, i, 0))
    return pl.pallas_call(
        functools.partial(_s5_out_kernel, nb=b, tk=tk),
        grid=(t // tk,),
        in_specs=[
            tile, ytile, ytile, tile,
            pl.BlockSpec((1, D_MODEL), lambda i: (0, 0)),
            pl.BlockSpec(w_glu.shape, lambda i: (0, 0)),
            pl.BlockSpec((b, ADA_CHUNKS, D_MODEL), lambda i: (0, 0, 0)),
        ],
        out_specs=tile,
        out_shape=jax.ShapeDtypeStruct((b, t, D_MODEL), F32),
        scratch_shapes=[pltpu.VMEM((N_SLAB, b * tk, LANES), F32)],
        compiler_params=_params(1),
        name="s5_out",
    )(x3, y_f, y_b, h3, d_skip.reshape(1, D_MODEL).astype(F32), w_glu, mod)


def _even_layer(x2, xc2, mod_l, mod_c, b, t, tc, need_ctx, p):
    w_in = jnp.concatenate([p['w_in'][:, :MLA_IN], jnp.zeros((D_MODEL, MLA_IN_PAD - MLA_IN), F32),
                            p['w_in'][:, MLA_IN:]], axis=1).astype(BF16)
    z_c = _norm_mod_matmul_call(xc2, p['norm1_g'], mod_c, tc, w_in)
    z_l = _norm_mod_matmul_call(x2, p['norm1_g'], mod_l, t, w_in)
    wts = _mla_weights(p['q_norm_g'], p['w_uq'], p['kv_norm_g'], p['w_ukv'], p['qn_g'], p['kn_g'])
    q_c, k_c, vl_c, vh_c = _mla_prep_call(z_c, tc, wts, *_mla_tables(tc, False))
    q_l, k_l, vl_l, vh_l = _mla_prep_call(z_l, t, wts, *_mla_tables(t, True))
    a_l = _attn_call(q_l, [k_c, k_l], [vl_c, vl_l], [vh_c, vh_l], t, [tc, t])
    lg = jnp.stack([jnp.log1p(-jnp.exp2(p['lg_f'].astype(F32))), jnp.log1p(-jnp.exp2(p['lg_b'].astype(F32)))])
    zero = jnp.zeros((b, N_PAIR, LANES, LANES), F32)
    of_c, ob_c, s_cf, s_cb = _ret_call(z_c, lg, _ret_tables(tc, False), zero, zero, b, tc)
    of_l, ob_l, _, _ = _ret_call(z_l, lg, _ret_tables(t, True), s_cf, s_cb, b, t)
    w_out = p['w_out'].astype(BF16)
    x2 = _mix_out_call(x2, a_l, of_l, ob_l, z_l, w_out, mod_l, t)
    if need_ctx:
        a_c = _attn_call(q_c, [k_c], [vl_c], [vh_c], tc, [tc])
        xc2 = _mix_out_call(xc2, a_c, of_c, ob_c, z_c, w_out, mod_c, tc)
    return x2, xc2


def _odd_layer(x2, xc2, mod_l, mod_c, b, t, tc, need_ctx, p):
    x3, xc3 = x2.reshape(b, t, D_MODEL), xc2.reshape(b, tc, D_MODEL)
    h_c, u_c = _norm_mod_s5_call(xc3, p['norm1_g'], mod_c)
    h_l, u_l = _norm_mod_s5_call(x3, p['norm1_g'], mod_l)
    tt, wt, vt, coef = _s5_compact_weights(p['s5_f'], p['s5_b'])
    ycf, ycb, ylf, ylb = _s5_call(u_c, u_l, tt, wt, vt, coef, b)
    w_glu = p['w_glu'].astype(BF16)
    x2 = _s5_out_call(x3, ylf, ylb, h_l, p['d_skip'], w_glu, mod_l).reshape(b * t, D_MODEL)
    if need_ctx:
        xc2 = _s5_out_call(xc3, ycf, ycb, h_c, p['d_skip'], w_glu, mod_c).reshape(b * tc, D_MODEL)
    return x2, xc2


def kernel(x, c, ctx, c_ctx, ada_w, ada_b, norm1_g, norm2_g, mlp_w1, mlp_w2, w_in, mla_q_norm_g, mla_w_uq, mla_kv_norm_g, mla_w_ukv, mla_qn_g, mla_kn_g, ret_lg_f, ret_lg_b, w_out, s5_a_re_f, s5_a_im_f, s5_b_re_f, s5_b_im_f, s5_c_re_f, s5_c_im_f, s5_log_dt_f, s5_a_re_b, s5_a_im_b, s5_b_re_b, s5_b_im_b, s5_c_re_b, s5_c_im_b, s5_log_dt_b, s5_d, s5_w_glu):
    b, t, _ = x.shape
    tc = ctx.shape[1]
    depth = ada_w.shape[0]
    assert b % 8 == 0 and tc % KEY_CHUNK == 0 and t % tc == 0 and tc % S5_TOK == 0
    rows = -(-(b + 1) // 8) * 8
    cc = jnp.zeros((rows, D_MODEL), F32).at[:b].set(c.astype(F32)).at[b].set(c_ctx.astype(F32))
    mod = _ada_all(cc, ada_w.astype(F32), ada_b.astype(F32))
    x2 = x.reshape(b * t, D_MODEL).astype(F32)
    xc2 = ctx.reshape(b * tc, D_MODEL).astype(F32)
    for l in range(depth):
        need_ctx = l < depth - 1
        mod_l = mod[l, :b].reshape(b, ADA_CHUNKS, D_MODEL)
        mod_c = jnp.broadcast_to(mod[l, b].reshape(1, ADA_CHUNKS, D_MODEL), (b, ADA_CHUNKS, D_MODEL))
        if l % 2 == 0:
            e = l // 2
            p = dict(norm1_g=norm1_g[l], w_in=w_in[e], q_norm_g=mla_q_norm_g[e], w_uq=mla_w_uq[e],
                     kv_norm_g=mla_kv_norm_g[e], w_ukv=mla_w_ukv[e], qn_g=mla_qn_g[e], kn_g=mla_kn_g[e],
                     lg_f=ret_lg_f[e], lg_b=ret_lg_b[e], w_out=w_out[e])
            x2, xc2 = _even_layer(x2, xc2, mod_l, mod_c, b, t, tc, need_ctx, p)
        else:
            o = l // 2
            p = dict(norm1_g=norm1_g[l], d_skip=s5_d[o], w_glu=s5_w_glu[o],
                     s5_f=(s5_a_re_f[o], s5_a_im_f[o], s5_b_re_f[o], s5_b_im_f[o], s5_c_re_f[o], s5_c_im_f[o], s5_log_dt_f[o]),
                     s5_b=(s5_a_re_b[o], s5_a_im_b[o], s5_b_re_b[o], s5_b_im_b[o], s5_c_re_b[o], s5_c_im_b[o], s5_log_dt_b[o]))
            x2, xc2 = _odd_layer(x2, xc2, mod_l, mod_c, b, t, tc, need_ctx, p)
        w1 = mlp_w1[l].astype(BF16)
        w2 = mlp_w2[l].astype(BF16)
        x2 = _mlp_call(x2, norm2_g[l], mod_l, w1, w2, t)
        if need_ctx:
            xc2 = _mlp_call(xc2, norm2_g[l], mod_c, w1, w2, tc)
    return x2.reshape(b, t, D_MODEL).astype(x.dtype)
```

```python
import functools
import math

import jax
import jax.numpy as jnp
import numpy as np
from jax import lax
from jax.experimental import pallas as pl
from jax.experimental.pallas import tpu as pltpu

F32 = jnp.float32
BF16 = jnp.bfloat16

D_MODEL = 1024
EPS = 1e-6
ADA_CHUNKS = 6
GRID_W = 64
ROPE_BASE = 10000.0
LANES = 128
N_SLAB = D_MODEL // LANES

H_A = 8
Q_LORA = 256
KV_LORA = 128
NOPE_DIM = 64
ROPE_DIM = 32
QK_DIM = NOPE_DIM + ROPE_DIM
V_DIM = 64
HEAD_SLOT = LANES
MLA_IN = Q_LORA + KV_LORA + ROPE_DIM
MLA_IN_PAD = 512
KEY_CHUNK = 256
SUB_ROWS = 64

H_R = 8
DK_R = 64
DV_R = 64
RET_W = H_R * DK_R
N_PAIR = H_R // 2

S5_GROUP = 16
S5_GROUPS = D_MODEL // S5_GROUP
S5_STATE = 64
S5_L = 8
S5_K = S5_L * LANES
S5_GPS = LANES // S5_GROUP
S5_HALF = S5_GPS * S5_STATE
S5_TOK = 32

D_FF = 4 * D_MODEL
Z_WIDTH = MLA_IN_PAD + 4 * RET_W

VMEM_LIMIT = 56 * 1024 * 1024

_NT = (((1,), (1,)), ((), ()))
_HP = lax.Precision.HIGHEST


def _params(n_grid):
    return pltpu.CompilerParams(
        dimension_semantics=("arbitrary",) * n_grid, vmem_limit_bytes=VMEM_LIMIT)


def _sigmoid(x):
    return 1.0 / (1.0 + jnp.exp(-x))


def _pick_tile(t, pref):
    tile = min(t, pref)
    while t % tile:
        tile //= 2
    return tile


def _ada_kernel(c_ref, w_ref, b_ref, o_ref):
    cc = c_ref[...]
    s = cc * _sigmoid(cc)
    o_ref[0] = jnp.dot(s, w_ref[0], preferred_element_type=F32, precision=_HP) + b_ref[0]


def _ada_all(cc, ada_w, ada_b):
    depth, _, width = ada_w.shape
    r = cc.shape[0]
    tn = 1536
    return pl.pallas_call(
        _ada_kernel,
        grid=(depth, width // tn),
        in_specs=[
            pl.BlockSpec((r, D_MODEL), lambda l, j: (0, 0)),
            pl.BlockSpec((1, D_MODEL, tn), lambda l, j: (l, 0, j)),
            pl.BlockSpec((1, 1, tn), lambda l, j: (l, 0, j)),
        ],
        out_specs=pl.BlockSpec((1, r, tn), lambda l, j: (l, 0, j)),
        out_shape=jax.ShapeDtypeStruct((depth, r, width), F32),
        compiler_params=_params(2),
        name="ada_mod",
    )(cc, ada_w, ada_b.reshape(depth, 1, width))


def _norm_mod(x, g_ref, mod_ref, shift_row, scale_row):
    ms = jnp.mean(x * x, axis=-1, keepdims=True)
    y = x * lax.rsqrt(ms + EPS) * g_ref[...]
    return y * (1.0 + mod_ref[0, scale_row:scale_row + 1, :]) + mod_ref[0, shift_row:shift_row + 1, :]


def _nmm_kernel(x_ref, g_ref, mod_ref, w_ref, o_ref):
    h = _norm_mod(x_ref[...], g_ref, mod_ref, 0, 1).astype(BF16)
    o_ref[...] = jnp.dot(h, w_ref[...], preferred_element_type=F32).astype(o_ref.dtype)


def _norm_mod_matmul_call(x2, g, mod, t, w):
    n = x2.shape[0]
    tm = _pick_tile(t, 512)
    tpb = t // tm
    n_out = w.shape[1]
    return pl.pallas_call(
        _nmm_kernel,
        grid=(n // tm,),
        in_specs=[
            pl.BlockSpec((tm, D_MODEL), lambda i: (i, 0)),
            pl.BlockSpec((1, D_MODEL), lambda i: (0, 0)),
            pl.BlockSpec((1, ADA_CHUNKS, D_MODEL), lambda i: (i // tpb, 0, 0)),
            pl.BlockSpec((D_MODEL, n_out), lambda i: (0, 0)),
        ],
        out_specs=pl.BlockSpec((tm, n_out), lambda i: (i, 0)),
        out_shape=jax.ShapeDtypeStruct((n, n_out), BF16),
        compiler_params=_params(1),
        name="norm_mod_w_in",
    )(x2, g.reshape(1, D_MODEL), mod, w)


def _mla_prep_kernel(z_ref, gq_ref, gkv_ref, wq_ref, wqs_ref, wkc_ref, wkcs_ref, wkr_ref, wkrs_ref,
                     wvt_ref, gqn_ref, gqns_ref, gkn_ref, gkns_ref, cos_ref, sin_ref,
                     q_ref, k_ref, vt_ref):
    cq = z_ref[:, 0:Q_LORA].astype(F32)
    ckv = z_ref[:, Q_LORA:Q_LORA + KV_LORA].astype(F32)
    kr = z_ref[:, Q_LORA + KV_LORA:MLA_IN_PAD]
    cqn = (cq * lax.rsqrt(jnp.mean(cq * cq, axis=-1, keepdims=True) + EPS) * gq_ref[...]).astype(BF16)
    ckn = (ckv * lax.rsqrt(jnp.mean(ckv * ckv, axis=-1, keepdims=True) + EPS) * gkv_ref[...]).astype(BF16)
    q = jnp.dot(cqn, wq_ref[...], preferred_element_type=F32)
    qs = jnp.dot(cqn, wqs_ref[...], preferred_element_type=F32)
    k = jnp.dot(ckn, wkc_ref[...], preferred_element_type=F32) + jnp.dot(kr, wkr_ref[...], preferred_element_type=F32)
    ks = jnp.dot(ckn, wkcs_ref[...], preferred_element_type=F32) + jnp.dot(kr, wkrs_ref[...], preferred_element_type=F32)
    vt_ref[...] = lax.dot_general(wvt_ref[...], ckn, _NT, preferred_element_type=F32).astype(vt_ref.dtype)
    cos = cos_ref[...]
    sin = sin_ref[...]
    q_scale = QK_DIM ** -0.5 * math.log2(math.e)
    for h in range(H_A):
        sl = slice(h * HEAD_SLOT, (h + 1) * HEAD_SLOT)
        qh = q[:, sl]
        rq = lax.rsqrt(jnp.sum(qh * qh, axis=-1, keepdims=True) * (1.0 / QK_DIM) + EPS)
        q_rot = (qh * gqn_ref[...] * cos + qs[:, sl] * gqns_ref[...] * sin) * (rq * q_scale)
        q_ref[:, sl] = q_rot.astype(q_ref.dtype)
        kh = k[:, sl]
        rk = lax.rsqrt(jnp.sum(kh * kh, axis=-1, keepdims=True) * (1.0 / QK_DIM) + EPS)
        k_rot = (kh * gkn_ref[...] * cos + ks[:, sl] * gkns_ref[...] * sin) * rk
        k_ref[:, sl] = k_rot.astype(k_ref.dtype)


def _mla_prep_call(z, t, wts, cos_t, sin_t):
    n = z.shape[0]
    tm = _pick_tile(t, 512)
    tpb = t // tm
    full = lambda a: pl.BlockSpec(a.shape, lambda i: (0,) * a.ndim)
    in_specs = [pl.BlockSpec((tm, MLA_IN_PAD), lambda i: (i, 0))] + [full(a) for a in wts] + [
        pl.BlockSpec((tm, HEAD_SLOT), lambda i: (i % tpb, 0)),
        pl.BlockSpec((tm, HEAD_SLOT), lambda i: (i % tpb, 0)),
    ]
    hw = H_A * HEAD_SLOT
    vw = H_A * V_DIM
    return pl.pallas_call(
        _mla_prep_kernel,
        grid=(n // tm,),
        in_specs=in_specs,
        out_specs=[
            pl.BlockSpec((tm, hw), lambda i: (i, 0)),
            pl.BlockSpec((tm, hw), lambda i: (i, 0)),
            pl.BlockSpec((vw, tm), lambda i: (0, i)),
        ],
        out_shape=[
            jax.ShapeDtypeStruct((n, hw), BF16),
            jax.ShapeDtypeStruct((n, hw), BF16),
            jax.ShapeDtypeStruct((vw, n), BF16),
        ],
        compiler_params=_params(1),
        name="mla_prep",
    )(z, *wts, cos_t, sin_t)


def _mla_weights(q_norm_g, w_uq, kv_norm_g, w_ukv, qn_g, kn_g):
    pad = HEAD_SLOT - QK_DIM
    half = ROPE_DIM // 2
    perm = jnp.arange(HEAD_SLOT)
    perm = perm.at[NOPE_DIM:NOPE_DIM + half].set(jnp.arange(NOPE_DIM + half, NOPE_DIM + ROPE_DIM))
    perm = perm.at[NOPE_DIM + half:NOPE_DIM + ROPE_DIM].set(jnp.arange(NOPE_DIM, NOPE_DIM + half))

    def slots(w):
        wp = jnp.pad(w, ((0, 0), (0, 0), (0, pad)))
        return wp, wp[:, :, perm]

    def flat(w):
        return w.reshape(w.shape[0], H_A * HEAD_SLOT).astype(BF16)

    wq, wqs = slots(w_uq.reshape(Q_LORA, H_A, QK_DIM))
    w_kv = w_ukv.reshape(KV_LORA, H_A, NOPE_DIM + V_DIM)
    wkc, wkcs = slots(jnp.pad(w_kv[:, :, :NOPE_DIM], ((0, 0), (0, 0), (0, ROPE_DIM))))
    eye = jnp.zeros((HEAD_SLOT, QK_DIM), F32).at[jnp.arange(ROPE_DIM), NOPE_DIM + jnp.arange(ROPE_DIM)].set(1.0)
    wkr, wkrs = slots(jnp.broadcast_to(eye[:, None, :], (HEAD_SLOT, H_A, QK_DIM)))
    wvt = w_kv[:, :, NOPE_DIM:].reshape(KV_LORA, H_A * V_DIM).T.astype(BF16)

    def gains(g):
        gp = jnp.pad(g.astype(F32), (0, pad))
        return gp.reshape(1, HEAD_SLOT), gp[perm].reshape(1, HEAD_SLOT)

    gqn, gqns = gains(qn_g)
    gkn, gkns = gains(kn_g)
    return [q_norm_g.reshape(1, Q_LORA).astype(F32), kv_norm_g.reshape(1, KV_LORA).astype(F32),
            flat(wq), flat(wqs), flat(wkc), flat(wkcs), flat(wkr), flat(wkrs), wvt, gqn, gqns, gkn, gkns]


def _mla_tables(n_tok, rotary):
    cos = jnp.ones((n_tok, HEAD_SLOT), F32)
    sin = jnp.zeros((n_tok, HEAD_SLOT), F32)
    if rotary:
        rows = n_tok // GRID_W
        r = jnp.repeat(jnp.arange(rows, dtype=F32), GRID_W)
        col = jnp.tile(jnp.arange(GRID_W, dtype=F32), rows)
        nf = ROPE_DIM // 4
        f = ROPE_BASE ** (-jnp.arange(nf, dtype=F32) / nf)
        ang = jnp.concatenate([r[:, None] * f, col[:, None] * f], axis=-1)
        c, s = jnp.cos(ang), jnp.sin(ang)
        half = ROPE_DIM // 2
        cos = cos.at[:, NOPE_DIM:NOPE_DIM + half].set(c).at[:, NOPE_DIM + half:NOPE_DIM + ROPE_DIM].set(c)
        sin = sin.at[:, NOPE_DIM:NOPE_DIM + half].set(-s).at[:, NOPE_DIM + half:NOPE_DIM + ROPE_DIM].set(s)
    return cos, sin


def _attn_kernel(*refs, seg_lens):
    nseg = len(seg_lens)
    q_ref = refs[0]
    k_refs = refs[1:1 + nseg]
    vt_refs = refs[1 + nseg:1 + 2 * nseg]
    o_ref = refs[1 + 2 * nseg]
    s_s = refs[2 + 2 * nseg]
    chunks = [(g, r) for g, n in enumerate(seg_lens) for r in range(0, n, KEY_CHUNK)]
    tq = q_ref.shape[0]

    def fold(x, op):
        n = x.shape[0] // 8
        x = x.reshape(n, 8, tq)
        out = x[0]
        for i in range(1, n):
            out = op(out, x[i])
        return out

    def scores(h):
        sl = slice(h * HEAD_SLOT, (h + 1) * HEAD_SLOT)
        qh = q_ref[:, sl]
        mx = None
        for c, (g, r) in enumerate(chunks):
            s = lax.dot_general(k_refs[g][r:r + KEY_CHUNK, sl], qh, _NT, preferred_element_type=F32)
            s_s[h % 2, c] = s
            cm = fold(s, jnp.maximum)
            mx = cm if mx is None else jnp.maximum(mx, cm)
        return jnp.max(mx, axis=0, keepdims=True)

    def weighted_values(h, m):
        rows = slice(h * V_DIM, (h + 1) * V_DIM)
        lsum = None
        o_t = None
        for c, (g, r) in enumerate(chunks):
            pieces = []
            for r0 in range(0, KEY_CHUNK, SUB_ROWS):
                p = jnp.exp2(s_s[h % 2, c, r0:r0 + SUB_ROWS, :] - m)
                ps = fold(p, jnp.add)
                lsum = ps if lsum is None else lsum + ps
                pieces.append(p.astype(BF16))
            pb = jnp.concatenate(pieces, axis=0)
            t = jnp.dot(vt_refs[g][rows, r:r + KEY_CHUNK], pb, preferred_element_type=F32)
            o_t = t if o_t is None else o_t + t
        return o_t * (1.0 / jnp.sum(lsum, axis=0, keepdims=True))

    outs = []
    m_next = scores(0)
    for h in range(H_A):
        m_cur = m_next
        if h + 1 < H_A:
            m_next = scores(h + 1)
        outs.append(weighted_values(h, m_cur))
        if h % 2 == 1:
            j = h // 2
            o_ref[:, j * LANES:(j + 1) * LANES] = jnp.concatenate(outs[-2:], axis=0).T.astype(o_ref.dtype)


def _attn_call(q, ks, vts, t, tks):
    n = q.shape[0]
    tq = _pick_tile(t, 256)
    tpb = t // tq
    hw = H_A * HEAD_SLOT
    vw = H_A * V_DIM
    n_chunks = sum(tk // KEY_CHUNK for tk in tks)
    in_specs = [pl.BlockSpec((tq, hw), lambda i: (i, 0))]
    in_specs += [pl.BlockSpec((tk, hw), lambda i: (i // tpb, 0)) for tk in tks]
    in_specs += [pl.BlockSpec((vw, tk), lambda i: (0, i // tpb)) for tk in tks]
    return pl.pallas_call(
        functools.partial(_attn_kernel, seg_lens=tuple(tks)),
        grid=(n // tq,),
        in_specs=in_specs,
        out_specs=pl.BlockSpec((tq, vw), lambda i: (i, 0)),
        out_shape=jax.ShapeDtypeStruct((n, vw), BF16),
        scratch_shapes=[pltpu.VMEM((2, n_chunks, KEY_CHUNK, tq), F32)],
        compiler_params=_params(1),
        name="attention_%dseg" % len(tks),
    )(q, *ks, *vts)


def _rotate_pairs(x, cos, sin_lo, sin_hi):
    return x * cos + pltpu.roll(x, LANES - DK_R // 2, 1) * sin_lo + pltpu.roll(x, DK_R // 2, 1) * sin_hi


def _ret_kernel(lg_ref, qf_ref, kf_ref, vf_ref, cf_ref, slf_ref, shf_ref,
                qb_ref, kb_ref, vb_ref, cb_ref, slb_ref, shb_ref, s0f_ref, s0b_ref,
                of_ref, ob_ref, sff_ref, sfb_ref, df_s, db_s, sf_s, sb_s, *, chunk):
    b = pl.program_id(0)
    c = pl.program_id(1)
    nc = pl.num_programs(1)

    @pl.when(jnp.logical_and(b == 0, c == 0))
    def _():
        ii = lax.broadcasted_iota(jnp.int32, (chunk, chunk), 0)
        jj = lax.broadcasted_iota(jnp.int32, (chunk, chunk), 1)
        diff = (ii - jj).astype(F32)
        for h in range(H_R):
            df_s[h] = jnp.where(ii >= jj, jnp.exp(jnp.where(ii >= jj, diff, 0.0) * lg_ref[0, h]), 0.0)
            db_s[h] = jnp.where(jj > ii, jnp.exp(jnp.where(jj > ii, -diff, 0.0) * lg_ref[1, h]), 0.0)

    @pl.when(c == 0)
    def _():
        sf_s[...] = s0f_ref[0]
        sb_s[...] = s0b_ref[0]

    lane = lax.broadcasted_iota(jnp.int32, (1, LANES), 1)
    lo = lane < DK_R
    row = lax.broadcasted_iota(jnp.int32, (LANES, LANES), 0)
    colm = lax.broadcasted_iota(jnp.int32, (LANES, LANES), 1)
    blockdiag = (row < DK_R) == (colm < DK_R)
    pos = lax.broadcasted_iota(jnp.int32, (chunk, 1), 0).astype(F32)

    def one_direction(d, q_ref, k_ref, v_ref, cos_ref, sl_ref, sh_ref, d_s, s_s, o_ref):
        cos, sin_lo, sin_hi = cos_ref[...], sl_ref[...], sh_ref[...]
        for j in range(N_PAIR):
            sl = slice(j * LANES, (j + 1) * LANES)
            lg = jnp.where(lo, lg_ref[d, 2 * j], lg_ref[d, 2 * j + 1])
            q2 = _rotate_pairs(q_ref[:, sl].astype(F32), cos, sin_lo, sin_hi)
            k2 = _rotate_pairs(k_ref[:, sl].astype(F32), cos, sin_lo, sin_hi) * (DK_R ** -0.5)
            v2 = v_ref[:, sl]
            if d == 0:
                q_dec = jnp.exp((pos + 1.0) * lg)
                k_dec = jnp.exp((chunk - 1.0 - pos) * lg)
            else:
                q_dec = jnp.exp((chunk - pos) * lg)
                k_dec = jnp.exp(pos * lg)
            c_dec = jnp.exp(chunk * lg)
            s2 = s_s[j]
            o = jnp.dot((q2 * q_dec).astype(BF16), s2.astype(BF16), preferred_element_type=F32)
            k2b = k2.astype(BF16)
            for e in range(2):
                mask = lo if e == 0 else jnp.logical_not(lo)
                qe = jnp.where(mask, q2, 0.0).astype(BF16)
                sc = lax.dot_general(qe, k2b, _NT, preferred_element_type=F32) * d_s[2 * j + e]
                ve = jnp.where(mask, v2, jnp.zeros((), BF16))
                o = o + jnp.dot(sc.astype(BF16), ve, preferred_element_type=F32)
            o_ref[:, sl] = o.astype(o_ref.dtype)
            kd_t = (k2 * k_dec).T.astype(BF16)
            upd = jnp.dot(kd_t, v2, preferred_element_type=F32)
            s_s[j] = s2 * c_dec + jnp.where(blockdiag, upd, 0.0)

    one_direction(0, qf_ref, kf_ref, vf_ref, cf_ref, slf_ref, shf_ref, df_s, sf_s, of_ref)
    one_direction(1, qb_ref, kb_ref, vb_ref, cb_ref, slb_ref, shb_ref, db_s, sb_s, ob_ref)

    @pl.when(c == nc - 1)
    def _():
        sff_ref[0] = sf_s[...]
        sfb_ref[0] = sb_s[...]


def _ret_call(z, lg, tables, s0f, s0b, b, t):
    n = z.shape[0]
    chunk = _pick_tile(t, 256)
    nc = t // chunk
    cos_t, sin_lo_t, sin_hi_t = tables
    col0 = MLA_IN_PAD // RET_W

    def zspec(part, rev):
        if rev:
            return pl.BlockSpec((chunk, RET_W), lambda bi, ci: (bi * nc + nc - 1 - ci, col0 + part))
        return pl.BlockSpec((chunk, RET_W), lambda bi, ci: (bi * nc + ci, col0 + part))

    def tspec(rev):
        if rev:
            return pl.BlockSpec((chunk, LANES), lambda bi, ci: (nc - 1 - ci, 0))
        return pl.BlockSpec((chunk, LANES), lambda bi, ci: (ci, 0))

    st_spec = pl.BlockSpec((1, N_PAIR, LANES, LANES), lambda bi, ci: (bi, 0, 0, 0))
    in_specs = [pl.BlockSpec(memory_space=pltpu.SMEM)]
    in_specs += [zspec(0, False), zspec(1, False), zspec(2, False), tspec(False), tspec(False), tspec(False)]
    in_specs += [zspec(0, True), zspec(1, True), zspec(2, True), tspec(True), tspec(True), tspec(True)]
    in_specs += [st_spec, st_spec]
    out_specs = [
        pl.BlockSpec((chunk, RET_W), lambda bi, ci: (bi * nc + ci, 0)),
        pl.BlockSpec((chunk, RET_W), lambda bi, ci: (bi * nc + nc - 1 - ci, 0)),
        st_spec, st_spec,
    ]
    st_shape = jax.ShapeDtypeStruct((b, N_PAIR, LANES, LANES), F32)
    return pl.pallas_call(
        functools.partial(_ret_kernel, chunk=chunk),
        grid=(b, nc),
        in_specs=in_specs,
        out_specs=out_specs,
        out_shape=[jax.ShapeDtypeStruct((n, RET_W), BF16), jax.ShapeDtypeStruct((n, RET_W), BF16),
                   st_shape, st_shape],
        scratch_shapes=[
            pltpu.VMEM((H_R, chunk, chunk), F32), pltpu.VMEM((H_R, chunk, chunk), F32),
            pltpu.VMEM((N_PAIR, LANES, LANES), F32), pltpu.VMEM((N_PAIR, LANES, LANES), F32),
        ],
        compiler_params=_params(2),
        name="retention",
    )(lg, z, z, z, cos_t, sin_lo_t, sin_hi_t, z, z, z, cos_t, sin_lo_t, sin_hi_t, s0f, s0b)


def _ret_tables(n_tok, rotary):
    cos = jnp.ones((n_tok, LANES), F32)
    sin_lo = jnp.zeros((n_tok, LANES), F32)
    sin_hi = jnp.zeros((n_tok, LANES), F32)
    if rotary:
        nf = DK_R // 2
        theta = ROPE_BASE ** (-jnp.arange(nf, dtype=F32) / nf)
        ang = jnp.arange(n_tok, dtype=F32)[:, None] * theta
        c, s = jnp.cos(ang), jnp.sin(ang)
        z = jnp.zeros_like(s)
        cos = jnp.concatenate([c, c, c, c], axis=-1)
        sin_lo = jnp.concatenate([-s, z, -s, z], axis=-1)
        sin_hi = jnp.concatenate([z, s, z, s], axis=-1)
    return cos, sin_lo, sin_hi


def _mix_out_kernel(x_ref, a_ref, of_ref, ob_ref, g_ref, w_ref, mod_ref, o_ref):
    lane = lax.broadcasted_iota(jnp.int32, (1, LANES), 1)
    lo = lane < DV_R
    y = jnp.dot(a_ref[...], w_ref[0:H_A * V_DIM, :], preferred_element_type=F32)
    for j in range(N_PAIR):
        sl = slice(j * LANES, (j + 1) * LANES)
        o = of_ref[:, sl].astype(F32) + ob_ref[:, sl].astype(F32)
        s_lo = jnp.sum(jnp.where(lo, o, 0.0), axis=-1, keepdims=True)
        s_all = jnp.sum(o, axis=-1, keepdims=True)
        mu = jnp.where(lo, s_lo, s_all - s_lo) * (1.0 / DV_R)
        oc = o - mu
        q = oc * oc
        q_lo = jnp.sum(jnp.where(lo, q, 0.0), axis=-1, keepdims=True)
        q_all = jnp.sum(q, axis=-1, keepdims=True)
        var = jnp.where(lo, q_lo, q_all - q_lo) * (1.0 / DV_R)
        g = g_ref[:, sl].astype(F32)
        r = (oc * lax.rsqrt(var + EPS)) * (g * _sigmoid(g))
        row0 = H_A * V_DIM + j * LANES
        y = y + jnp.dot(r.astype(BF16), w_ref[row0:row0 + LANES, :], preferred_element_type=F32)
    o_ref[...] = x_ref[...] + mod_ref[0, 2:3, :] * y


def _mix_out_call(x2, a, o_f, o_b, z, w_out, mod, t):
    n = x2.shape[0]
    tm = _pick_tile(t, 512)
    tpb = t // tm
    gate_col = MLA_IN_PAD // RET_W + 3
    return pl.pallas_call(
        _mix_out_kernel,
        grid=(n // tm,),
        in_specs=[
            pl.BlockSpec((tm, D_MODEL), lambda i: (i, 0)),
            pl.BlockSpec((tm, H_A * V_DIM), lambda i: (i, 0)),
            pl.BlockSpec((tm, RET_W), lambda i: (i, 0)),
            pl.BlockSpec((tm, RET_W), lambda i: (i, 0)),
            pl.BlockSpec((tm, RET_W), lambda i: (i, gate_col)),
            pl.BlockSpec(w_out.shape, lambda i: (0, 0)),
            pl.BlockSpec((1, ADA_CHUNKS, D_MODEL), lambda i: (i // tpb, 0, 0)),
        ],
        out_specs=pl.BlockSpec((tm, D_MODEL), lambda i: (i, 0)),
        out_shape=jax.ShapeDtypeStruct((n, D_MODEL), F32),
        compiler_params=_params(1),
        name="mix_out",
    )(x2, a, o_f, o_b, z, w_out, mod)


def _mlp_kernel(x_ref, g_ref, mod_ref, w1_ref, w2_ref, o_ref, h_s, acc_s):
    j = pl.program_id(1)

    @pl.when(j == 0)
    def _():
        h_s[...] = _norm_mod(x_ref[...], g_ref, mod_ref, 3, 4).astype(BF16)
        acc_s[...] = jnp.zeros_like(acc_s)

    a = jnp.maximum(jnp.dot(h_s[...], w1_ref[...], preferred_element_type=F32), 0.0)
    acc_s[...] += jnp.dot((a * a).astype(BF16), w2_ref[...], preferred_element_type=F32)

    @pl.when(j == pl.num_programs(1) - 1)
    def _():
        o_ref[...] = x_ref[...] + mod_ref[0, 5:6, :] * acc_s[...]


def _mlp_call(x2, g, mod, w1, w2, t):
    n = x2.shape[0]
    tm = _pick_tile(t, 1024)
    tpb = t // tm
    tf = 512
    return pl.pallas_call(
        _mlp_kernel,
        grid=(n // tm, D_FF // tf),
        in_specs=[
            pl.BlockSpec((tm, D_MODEL), lambda i, j: (i, 0)),
            pl.BlockSpec((1, D_MODEL), lambda i, j: (0, 0)),
            pl.BlockSpec((1, ADA_CHUNKS, D_MODEL), lambda i, j: (i // tpb, 0, 0)),
            pl.BlockSpec((D_MODEL, tf), lambda i, j: (0, j)),
            pl.BlockSpec((tf, D_MODEL), lambda i, j: (j, 0)),
        ],
        out_specs=pl.BlockSpec((tm, D_MODEL), lambda i, j: (i, 0)),
        out_shape=jax.ShapeDtypeStruct((n, D_MODEL), F32),
        scratch_shapes=[pltpu.VMEM((tm, D_MODEL), BF16), pltpu.VMEM((tm, D_MODEL), F32)],
        compiler_params=_params(2),
        name="mlp",
    )(x2, g.reshape(1, D_MODEL), mod, w1, w2)


def _s5_compact_weights(p_f, p_b):
    st = lambda i: jnp.stack([p_f[i].astype(F32), p_b[i].astype(F32)])
    ar, ai, br, bi, cr, ci, log_dt = (st(i) for i in range(7))
    dt = jnp.exp(log_dt)[:, :, None]
    lr, li = dt * ar, dt * ai
    steps = np.arange(S5_L)

    def powers(taus):
        tau = jnp.asarray(np.asarray(taus, np.float32))[:, :, None, None]
        mag = jnp.exp(tau * lr[:, None])
        return mag * jnp.cos(tau * li[:, None]), mag * jnp.sin(tau * li[:, None])

    def c_times(pr, pi):
        c_r, c_i = cr[:, None], ci[:, None]
        pr, pi = pr[:, :, :, None, :], pi[:, :, :, None, :]
        return c_r * pr - c_i * pi, c_r * pi + c_i * pr

    a1r, a1i = powers([[1], [1]])
    nr, ni = a1r[:, 0] - 1.0, a1i[:, 0]
    den = ar * ar + ai * ai
    qr, qi = (nr * ar + ni * ai) / den, (ni * ar - nr * ai) / den
    bbr = qr[..., None] * br - qi[..., None] * bi
    bbi = qr[..., None] * bi + qi[..., None] * br
    dr, di = c_times(*powers([steps, steps]))
    kern = (jnp.einsum('dtghp,dgpk->dtghk', dr, bbr, precision=_HP)
            - jnp.einsum('dtghp,dgpk->dtghk', di, bbi, precision=_HP))
    lag_f = steps[None, :] - steps[:, None]
    toep = jnp.stack([kern[0][np.clip(lag_f, 0, S5_L - 1)], kern[1][np.clip(-lag_f, 0, S5_L - 1)]])
    valid = np.stack([lag_f >= 0, lag_f <= 0])[:, :, :, None, None, None]
    toep = jnp.where(valid, toep, 0.0).reshape(2, S5_L, S5_L, N_SLAB, S5_GPS, S5_GROUP, S5_GROUP)
    tt = toep.transpose(0, 3, 1, 6, 2, 4, 5).reshape(2, N_SLAB, S5_L * S5_GROUP, S5_K)
    wpr, wpi = powers([S5_L - 1 - steps, steps])
    wpr, wpi = wpr[..., None], wpi[..., None]
    w = jnp.stack([wpr * bbr[:, None] - wpi * bbi[:, None], wpr * bbi[:, None] + wpi * bbr[:, None]], axis=1)
    w = w.reshape(2, 2, S5_L, N_SLAB, S5_GPS, S5_STATE, S5_GROUP)
    wt = w.transpose(0, 3, 2, 6, 1, 4, 5).reshape(2, N_SLAB, S5_L * S5_GROUP, 2 * S5_HALF)
    vr, vi = c_times(*powers([steps + 1, S5_L - steps]))
    v = jnp.stack([vr, -vi], axis=1).reshape(2, 2, S5_L, N_SLAB, S5_GPS, S5_GROUP, S5_STATE)
    vt = v.transpose(0, 3, 1, 6, 2, 4, 5).reshape(2, N_SLAB, 2 * S5_STATE, S5_K)
    alr, ali = powers([[S5_L], [S5_L]])
    al = jnp.stack([alr[0, 0], ali[0, 0], alr[1, 0], ali[1, 0]]).reshape(4, N_SLAB, S5_HALF).transpose(1, 0, 2)
    coef = jnp.concatenate([al, jnp.zeros_like(al)], axis=1)
    return tt, wt, vt, coef


def _nm_s5_kernel(x_ref, g_ref, mod_ref, h_ref, u_ref, hs_s, *, nb, tk):
    x = x_ref[...]
    ms = jnp.mean(x * x, axis=-1, keepdims=True)
    y = x * lax.rsqrt(ms + EPS) * g_ref[...]
    h = y * (1.0 + mod_ref[:, 1:2, :]) + mod_ref[:, 0:1, :]
    h_ref[...] = h.astype(h_ref.dtype)
    h2 = h.reshape(nb * tk, D_MODEL)
    for j in range(N_SLAB):
        hs_s[j] = h2[:, j * LANES:(j + 1) * LANES]
    for kk in range(tk // S5_L):
        for s in range(S5_L):
            for j in range(N_SLAB):
                blk = hs_s[j, pl.ds(kk * S5_L + s, nb, stride=tk), :]
                u_ref[j, kk * nb:(kk + 1) * nb, s * LANES:(s + 1) * LANES] = blk.astype(u_ref.dtype)


def _norm_mod_s5_call(x3, g, mod):
    b, t, _ = x3.shape
    tk = S5_TOK
    rows = (tk // S5_L) * b
    return pl.pallas_call(
        functools.partial(_nm_s5_kernel, nb=b, tk=tk),
        grid=(t // tk,),
        in_specs=[
            pl.BlockSpec((b, tk, D_MODEL), lambda i: (0, i, 0)),
            pl.BlockSpec((1, D_MODEL), lambda i: (0, 0)),
            pl.BlockSpec((b, ADA_CHUNKS, D_MODEL), lambda i: (0, 0, 0)),
        ],
        out_specs=[
            pl.BlockSpec((b, tk, D_MODEL), lambda i: (0, i, 0)),
            pl.BlockSpec((N_SLAB, rows, S5_K), lambda i: (0, i, 0)),
        ],
        out_shape=[
            jax.ShapeDtypeStruct((b, t, D_MODEL), BF16),
            jax.ShapeDtypeStruct((N_SLAB, (t // S5_L) * b, S5_K), BF16),
        ],
        scratch_shapes=[pltpu.VMEM((N_SLAB, b * tk, LANES), F32)],
        compiler_params=_params(1),
        name="norm_mod_s5",
    )(x3, g.reshape(1, D_MODEL), mod)


def _s5_kernel(uc_ref, ulf_ref, ulb_ref, tt_ref, wt_ref, vt_ref, coef_ref,
               ycf_ref, ycb_ref, ylf_ref, ylb_ref, t_s, w_s, v_s, u_s, s_s, x_s, carry_s, *, nb, nblk):
    i = pl.program_id(1)

    @pl.when(i == 0)
    def _():
        lane = lax.broadcasted_iota(jnp.int32, (1, S5_K), 1)
        grp_out = (lane // S5_GROUP) % S5_GPS
        grp_state = (lane // S5_STATE) % S5_GPS
        rows = S5_GROUP
        for d in range(2):
            for s in range(S5_L):
                tc = tt_ref[d, 0, s * rows:(s + 1) * rows, :]
                wc = wt_ref[d, 0, s * rows:(s + 1) * rows, :]
                for gi in range(S5_GPS):
                    r0 = s * LANES + gi * rows
                    t_s[d, r0:r0 + rows, :] = jnp.where(grp_out == gi, tc, 0.0).astype(BF16)
                    w_s[d, r0:r0 + rows, :] = jnp.where(grp_state == gi, wc, 0.0).astype(BF16)
            for c in range(2):
                vc = vt_ref[d, 0, c * S5_STATE:(c + 1) * S5_STATE, :]
                for gi in range(S5_GPS):
                    r0 = c * S5_HALF + gi * S5_STATE
                    v_s[d, r0:r0 + S5_STATE, :] = jnp.where(grp_out == gi, vc, 0.0).astype(BF16)
        carry_s[...] = jnp.zeros_like(carry_s)
        u_s[0] = uc_ref[0]
        u_s[1] = uc_ref[0]

    @pl.when(i > 0)
    def _():
        u_s[0] = ulf_ref[0]
        u_s[1] = ulb_ref[0]

    ys = []
    for d in range(2):
        u = u_s[d]
        s_s[d] = jnp.dot(u, w_s[d], preferred_element_type=F32)
        a_r = coef_ref[0, 2 * d:2 * d + 1, :]
        a_i = coef_ref[0, 2 * d + 1:2 * d + 2, :]

        def body(k, carry, d=d, a_r=a_r, a_i=a_i):
            x_r, x_i = carry
            kk = k if d == 0 else nblk - 1 - k
            r = pl.multiple_of(kk * nb, nb)
            x_s[d, pl.ds(r, nb), 0:S5_HALF] = x_r
            x_s[d, pl.ds(r, nb), S5_HALF:2 * S5_HALF] = x_i
            s_r = s_s[d, pl.ds(r, nb), 0:S5_HALF]
            s_i = s_s[d, pl.ds(r, nb), S5_HALF:2 * S5_HALF]
            return a_r * x_r - a_i * x_i + s_r, a_r * x_i + a_i * x_r + s_i

        x_r, x_i = lax.fori_loop(0, nblk, body, (carry_s[d, :, 0:S5_HALF], carry_s[d, :, S5_HALF:2 * S5_HALF]))
        carry_s[d, :, 0:S5_HALF] = x_r
        carry_s[d, :, S5_HALF:2 * S5_HALF] = x_i
        y = jnp.dot(u, t_s[d], preferred_element_type=F32)
        y = y + jnp.dot(x_s[d].astype(BF16), v_s[d], preferred_element_type=F32)
        ys.append(y.astype(BF16))

    @pl.when(i == 0)
    def _():
        ycf_ref[0] = ys[0]
        ycb_ref[0] = ys[1]

    @pl.when(i > 0)
    def _():
        ylf_ref[0] = ys[0]
        ylb_ref[0] = ys[1]


def _s5_call(u_c, u_l, tt, wt, vt, coef, nb):
    rows = u_c.shape[1]
    n_lat = u_l.shape[1] // rows
    nblk = rows // nb
    tile = lambda fn: pl.BlockSpec((1, rows, S5_K), fn)
    wspec = lambda a: pl.BlockSpec((2, 1) + a.shape[2:], lambda j, i: (0, j, 0, 0))
    ctx_map = lambda j, i: (j, 0, 0)
    fwd_map = lambda j, i: (j, jnp.maximum(i - 1, 0), 0)
    bwd_map = lambda j, i: (j, jnp.minimum(n_lat - i, n_lat - 1), 0)
    yc = jax.ShapeDtypeStruct(u_c.shape, BF16)
    yl = jax.ShapeDtypeStruct(u_l.shape, BF16)
    return pl.pallas_call(
        functools.partial(_s5_kernel, nb=nb, nblk=nblk),
        grid=(N_SLAB, n_lat + 1),
        in_specs=[tile(ctx_map), tile(fwd_map), tile(bwd_map), wspec(tt), wspec(wt), wspec(vt),
                  pl.BlockSpec((1,) + coef.shape[1:], lambda j, i: (j, 0, 0))],
        out_specs=[tile(ctx_map), tile(ctx_map), tile(fwd_map), tile(bwd_map)],
        out_shape=[yc, yc, yl, yl],
        scratch_shapes=[
            pltpu.VMEM((2, S5_K, S5_K), BF16),
            pltpu.VMEM((2, S5_K, 2 * S5_HALF), BF16),
            pltpu.VMEM((2, 2 * S5_HALF, S5_K), BF16),
            pltpu.VMEM((2, rows, S5_K), BF16),
            pltpu.VMEM((2, rows, 2 * S5_HALF), F32),
            pltpu.VMEM((2, rows, 2 * S5_HALF), F32),
            pltpu.VMEM((2, nb, 2 * S5_HALF), F32),
        ],
        compiler_params=_params(2),
        name="s5_scan",
    )(u_c, u_l, u_l, tt, wt, vt, coef)


def _s5_out_kernel(x_ref, yf_ref, yb_ref, h_ref, d_ref, w_ref, mod_ref, o_ref, ys_s, *, nb, tk):
    for kk in range(tk // S5_L):
        rows = slice(kk * nb, (kk + 1) * nb)
        for s in range(S5_L):
            lanes = slice(s * LANES, (s + 1) * LANES)
            for j in range(N_SLAB):
                blk = yf_ref[j, rows, lanes].astype(F32) + yb_ref[j, rows, lanes].astype(F32)
                ys_s[j, pl.ds(kk * S5_L + s, nb, stride=tk), :] = blk
    y = jnp.concatenate([ys_s[j] for j in range(N_SLAB)], axis=1)
    y = y + d_ref[...] * h_ref[...].reshape(nb * tk, D_MODEL).astype(F32)
    g = 0.5 * y * (1.0 + jnp.tanh(math.sqrt(2.0 / math.pi) * (y + 0.044715 * (y * y * y))))
    z = jnp.dot(g.astype(BF16), w_ref[...], preferred_element_type=F32)
    out = z[:, :D_MODEL] * _sigmoid(z[:, D_MODEL:])
    o_ref[...] = x_ref[...] + mod_ref[:, 2:3, :] * out.reshape(nb, tk, D_MODEL)


def _s5_out_call(x3, y_f, y_b, h3, d_skip, w_glu, mod):
    b, t, _ = x3.shape
    tk = S5_TOK
    rows = (tk // S5_L) * b
    tile = pl.BlockSpec((b, tk, D_MODEL), lambda i: (0, i, 0))
    ytile = pl.BlockSpec((N_SLAB, rows, S5_K), lambda i: (0, i, 0))
    return pl.pallas_call(
        functools.partial(_s5_out_kernel, nb=b, tk=tk),
        grid=(t // tk,),
        in_specs=[
            tile, ytile, ytile, tile,
            pl.BlockSpec((1, D_MODEL), lambda i: (0, 0)),
            pl.BlockSpec(w_glu.shape, lambda i: (0, 0)),
            pl.BlockSpec((b, ADA_CHUNKS, D_MODEL), lambda i: (0, 0, 0)),
        ],
        out_specs=tile,
        out_shape=jax.ShapeDtypeStruct((b, t, D_MODEL), F32),
        scratch_shapes=[pltpu.VMEM((N_SLAB, b * tk, LANES), F32)],
        compiler_params=_params(1),
        name="s5_out",
    )(x3, y_f, y_b, h3, d_skip.reshape(1, D_MODEL).astype(F32), w_glu, mod)


def _even_layer(x2, xc2, mod_l, mod_c, b, t, tc, need_ctx, p):
    n_c = b * tc
    z_c = _norm_mod_matmul_call(xc2, p['norm1_g'], mod_c, n_c, p['w_in'])
    z_l = _norm_mod_matmul_call(x2, p['norm1_g'], mod_l, t, p['w_in'])
    q_c, k_c, vt_c = _mla_prep_call(z_c, n_c, p['mla'], *_mla_tables(n_c, False))
    q_l, k_l, vt_l = _mla_prep_call(z_l, t, p['mla'], *_mla_tables(t, True))
    a_l = _attn_call(q_l, [k_c, k_l], [vt_c, vt_l], t, [tc, t])
    zero = jnp.zeros((b, N_PAIR, LANES, LANES), F32)
    of_c, ob_c, s_cf, s_cb = _ret_call(z_c, p['lg'], _ret_tables(tc, False), zero, zero, b, tc)
    of_l, ob_l, _, _ = _ret_call(z_l, p['lg'], _ret_tables(t, True), s_cf, s_cb, b, t)
    x2 = _mix_out_call(x2, a_l, of_l, ob_l, z_l, p['w_out'], mod_l, t)
    if need_ctx:
        a_c = _attn_call(q_c, [k_c], [vt_c], tc, [tc])
        xc2 = _mix_out_call(xc2, a_c, of_c, ob_c, z_c, p['w_out'], mod_c, n_c)
    return x2, xc2


def _odd_layer(x2, xc2, mod_l, mod_c, b, t, tc, need_ctx, p):
    x3, xc3 = x2.reshape(b, t, D_MODEL), xc2.reshape(b, tc, D_MODEL)
    h_c, u_c = _norm_mod_s5_call(xc3, p['norm1_g'], mod_c)
    h_l, u_l = _norm_mod_s5_call(x3, p['norm1_g'], mod_l)
    tt, wt, vt, coef = p['s5']
    ycf, ycb, ylf, ylb = _s5_call(u_c, u_l, tt, wt, vt, coef, b)
    x2 = _s5_out_call(x3, ylf, ylb, h_l, p['d_skip'], p['w_glu'], mod_l).reshape(b * t, D_MODEL)
    if need_ctx:
        xc2 = _s5_out_call(xc3, ycf, ycb, h_c, p['d_skip'], p['w_glu'], mod_c).reshape(b * tc, D_MODEL)
    return x2, xc2


def kernel(x, c, ctx, c_ctx, ada_w, ada_b, norm1_g, norm2_g, mlp_w1, mlp_w2, w_in, mla_q_norm_g, mla_w_uq, mla_kv_norm_g, mla_w_ukv, mla_qn_g, mla_kn_g, ret_lg_f, ret_lg_b, w_out, s5_a_re_f, s5_a_im_f, s5_b_re_f, s5_b_im_f, s5_c_re_f, s5_c_im_f, s5_log_dt_f, s5_a_re_b, s5_a_im_b, s5_b_re_b, s5_b_im_b, s5_c_re_b, s5_c_im_b, s5_log_dt_b, s5_d, s5_w_glu):
    b, t, _ = x.shape
    tc = ctx.shape[1]
    depth = ada_w.shape[0]
    assert b % 8 == 0 and tc % KEY_CHUNK == 0 and t % tc == 0 and tc % S5_TOK == 0
    rows = -(-(b + 1) // 8) * 8
    cc = jnp.zeros((rows, D_MODEL), F32).at[:b].set(c.astype(F32)).at[b].set(c_ctx.astype(F32))
    mod = _ada_all(cc, ada_w.astype(F32), ada_b.astype(F32))
    x2 = x.reshape(b * t, D_MODEL).astype(F32)
    xc2 = ctx.reshape(b * tc, D_MODEL).astype(F32)
    w1_all, w2_all = mlp_w1.astype(BF16), mlp_w2.astype(BF16)
    w_in_all = jnp.concatenate([w_in[:, :, :MLA_IN], jnp.zeros(w_in.shape[:2] + (MLA_IN_PAD - MLA_IN,), w_in.dtype),
                                w_in[:, :, MLA_IN:]], axis=2).astype(BF16)
    w_out_all, w_glu_all = w_out.astype(BF16), s5_w_glu.astype(BF16)
    mla_all = jax.vmap(_mla_weights)(mla_q_norm_g, mla_w_uq, mla_kv_norm_g, mla_w_ukv, mla_qn_g, mla_kn_g)
    lg_all = jnp.stack([jnp.log1p(-jnp.exp2(ret_lg_f.astype(F32))), jnp.log1p(-jnp.exp2(ret_lg_b.astype(F32)))], axis=1)
    s5_all = jax.vmap(_s5_compact_weights)(
        (s5_a_re_f, s5_a_im_f, s5_b_re_f, s5_b_im_f, s5_c_re_f, s5_c_im_f, s5_log_dt_f),
        (s5_a_re_b, s5_a_im_b, s5_b_re_b, s5_b_im_b, s5_c_re_b, s5_c_im_b, s5_log_dt_b))
    for l in range(depth):
        need_ctx = l < depth - 1
        mod_l = mod[l, :b].reshape(b, ADA_CHUNKS, D_MODEL)
        mod_c = jnp.broadcast_to(mod[l, b].reshape(1, ADA_CHUNKS, D_MODEL), (b, ADA_CHUNKS, D_MODEL))
        if l % 2 == 0:
            e = l // 2
            p = dict(norm1_g=norm1_g[l], w_in=w_in_all[e], mla=[a[e] for a in mla_all], lg=lg_all[e], w_out=w_out_all[e])
            x2, xc2 = _even_layer(x2, xc2, mod_l, mod_c, b, t, tc, need_ctx, p)
        else:
            o = l // 2
            p = dict(norm1_g=norm1_g[l], d_skip=s5_d[o], w_glu=w_glu_all[o], s5=[a[o] for a in s5_all])
            x2, xc2 = _odd_layer(x2, xc2, mod_l, mod_c, b, t, tc, need_ctx, p)
        x2 = _mlp_call(x2, norm2_g[l], mod_l, w1_all[l], w2_all[l], t)
        if need_ctx:
            xc2 = _mlp_call(xc2, norm2_g[l], mod_c, w1_all[l], w2_all[l], b * tc)
    return x2.reshape(b, t, D_MODEL).astype(x.dtype)
```

```python
import functools
import math

import jax
import jax.numpy as jnp
import numpy as np
from jax import lax
from jax.experimental import pallas as pl
from jax.experimental.pallas import tpu as pltpu

F32 = jnp.float32
BF16 = jnp.bfloat16

D_MODEL = 1024
EPS = 1e-6
ADA_CHUNKS = 6
GRID_W = 64
ROPE_BASE = 10000.0
LANES = 128
N_SLAB = D_MODEL // LANES

H_A = 8
Q_LORA = 256
KV_LORA = 128
NOPE_DIM = 64
ROPE_DIM = 32
QK_DIM = NOPE_DIM + ROPE_DIM
V_DIM = 64
HEAD_SLOT = LANES
MLA_IN = Q_LORA + KV_LORA + ROPE_DIM
MLA_IN_PAD = 512
KEY_CHUNK = 256
SUB_ROWS = 64

H_R = 8
DK_R = 64
DV_R = 64
RET_W = H_R * DK_R
N_PAIR = H_R // 2

S5_GROUP = 16
S5_GROUPS = D_MODEL // S5_GROUP
S5_STATE = 64
S5_L = 8
S5_K = S5_L * LANES
S5_GPS = LANES // S5_GROUP
S5_HALF = S5_GPS * S5_STATE
S5_TOK = 32

D_FF = 4 * D_MODEL
MLP_FF_CHUNK = 512
Z_WIDTH = MLA_IN_PAD + 4 * RET_W

VMEM_LIMIT = 56 * 1024 * 1024

_NT = (((1,), (1,)), ((), ()))
_HP = lax.Precision.HIGHEST


def _params(n_grid):
    return pltpu.CompilerParams(
        dimension_semantics=("arbitrary",) * n_grid, vmem_limit_bytes=VMEM_LIMIT)


def _sigmoid(x):
    return 1.0 / (1.0 + jnp.exp(-x))


def _pick_tile(t, pref):
    tile = min(t, pref)
    while t % tile:
        tile //= 2
    return tile


def _ada_kernel(c_ref, w_ref, b_ref, o_ref):
    cc = c_ref[...]
    s = cc * _sigmoid(cc)
    o_ref[0] = jnp.dot(s, w_ref[0], preferred_element_type=F32, precision=_HP) + b_ref[0]


def _ada_all(cc, ada_w, ada_b):
    depth, _, width = ada_w.shape
    r = cc.shape[0]
    tn = 1536
    return pl.pallas_call(
        _ada_kernel,
        grid=(depth, width // tn),
        in_specs=[
            pl.BlockSpec((r, D_MODEL), lambda l, j: (0, 0)),
            pl.BlockSpec((1, D_MODEL, tn), lambda l, j: (l, 0, j)),
            pl.BlockSpec((1, 1, tn), lambda l, j: (l, 0, j)),
        ],
        out_specs=pl.BlockSpec((1, r, tn), lambda l, j: (l, 0, j)),
        out_shape=jax.ShapeDtypeStruct((depth, r, width), F32),
        compiler_params=_params(2),
        name="ada_mod",
    )(cc, ada_w, ada_b.reshape(depth, 1, width))


def _norm_mod(x, g_ref, mod_ref, shift_row, scale_row):
    ms = jnp.mean(x * x, axis=-1, keepdims=True)
    y = x * lax.rsqrt(ms + EPS) * g_ref[...]
    return y * (1.0 + mod_ref[0, scale_row:scale_row + 1, :]) + mod_ref[0, shift_row:shift_row + 1, :]


def _nmm_kernel(x_ref, g_ref, mod_ref, w_ref, o_ref):
    h = _norm_mod(x_ref[...], g_ref, mod_ref, 0, 1).astype(BF16)
    o_ref[...] = jnp.dot(h, w_ref[...], preferred_element_type=F32).astype(o_ref.dtype)


def _norm_mod_matmul_call(x2, g, mod, t, w):
    n = x2.shape[0]
    tm = _pick_tile(t, 512)
    tpb = t // tm
    n_out = w.shape[1]
    return pl.pallas_call(
        _nmm_kernel,
        grid=(n // tm,),
        in_specs=[
            pl.BlockSpec((tm, D_MODEL), lambda i: (i, 0)),
            pl.BlockSpec((1, D_MODEL), lambda i: (0, 0)),
            pl.BlockSpec((1, ADA_CHUNKS, D_MODEL), lambda i: (i // tpb, 0, 0)),
            pl.BlockSpec((D_MODEL, n_out), lambda i: (0, 0)),
        ],
        out_specs=pl.BlockSpec((tm, n_out), lambda i: (i, 0)),
        out_shape=jax.ShapeDtypeStruct((n, n_out), BF16),
        compiler_params=_params(1),
        name="norm_mod_w_in",
    )(x2, g.reshape(1, D_MODEL), mod, w)


def _mla_prep_kernel(z_ref, gq_ref, gkv_ref, wq_ref, wqs_ref, wkc_ref, wkcs_ref, wkr_ref, wkrs_ref,
                     wvt_ref, gqn_ref, gqns_ref, gkn_ref, gkns_ref, cos_ref, sin_ref,
                     q_ref, k_ref, vt_ref):
    cq = z_ref[:, 0:Q_LORA].astype(F32)
    ckv = z_ref[:, Q_LORA:Q_LORA + KV_LORA].astype(F32)
    kr = z_ref[:, Q_LORA + KV_LORA:MLA_IN_PAD]
    cqn = (cq * lax.rsqrt(jnp.mean(cq * cq, axis=-1, keepdims=True) + EPS) * gq_ref[...]).astype(BF16)
    ckn = (ckv * lax.rsqrt(jnp.mean(ckv * ckv, axis=-1, keepdims=True) + EPS) * gkv_ref[...]).astype(BF16)
    q = jnp.dot(cqn, wq_ref[...], preferred_element_type=F32)
    qs = jnp.dot(cqn, wqs_ref[...], preferred_element_type=F32)
    k = jnp.dot(ckn, wkc_ref[...], preferred_element_type=F32) + jnp.dot(kr, wkr_ref[...], preferred_element_type=F32)
    ks = jnp.dot(ckn, wkcs_ref[...], preferred_element_type=F32) + jnp.dot(kr, wkrs_ref[...], preferred_element_type=F32)
    vt_ref[...] = lax.dot_general(wvt_ref[...], ckn, _NT, preferred_element_type=F32).astype(vt_ref.dtype)
    cos = cos_ref[...]
    sin = sin_ref[...]
    q_scale = QK_DIM ** -0.5 * math.log2(math.e)
    for h in range(H_A):
        sl = slice(h * HEAD_SLOT, (h + 1) * HEAD_SLOT)
        qh = q[:, sl]
        rq = lax.rsqrt(jnp.sum(qh * qh, axis=-1, keepdims=True) * (1.0 / QK_DIM) + EPS)
        q_rot = (qh * gqn_ref[...] * cos + qs[:, sl] * gqns_ref[...] * sin) * (rq * q_scale)
        q_ref[:, sl] = q_rot.astype(q_ref.dtype)
        kh = k[:, sl]
        rk = lax.rsqrt(jnp.sum(kh * kh, axis=-1, keepdims=True) * (1.0 / QK_DIM) + EPS)
        k_rot = (kh * gkn_ref[...] * cos + ks[:, sl] * gkns_ref[...] * sin) * rk
        k_ref[:, sl] = k_rot.astype(k_ref.dtype)


def _mla_prep_call(z, t, wts, cos_t, sin_t):
    n = z.shape[0]
    tm = _pick_tile(t, 512)
    tpb = t // tm
    full = lambda a: pl.BlockSpec(a.shape, lambda i: (0,) * a.ndim)
    in_specs = [pl.BlockSpec((tm, MLA_IN_PAD), lambda i: (i, 0))] + [full(a) for a in wts] + [
        pl.BlockSpec((tm, HEAD_SLOT), lambda i: (i % tpb, 0)),
        pl.BlockSpec((tm, HEAD_SLOT), lambda i: (i % tpb, 0)),
    ]
    hw = H_A * HEAD_SLOT
    vw = H_A * V_DIM
    return pl.pallas_call(
        _mla_prep_kernel,
        grid=(n // tm,),
        in_specs=in_specs,
        out_specs=[
            pl.BlockSpec((tm, hw), lambda i: (i, 0)),
            pl.BlockSpec((tm, hw), lambda i: (i, 0)),
            pl.BlockSpec((vw, tm), lambda i: (0, i)),
        ],
        out_shape=[
            jax.ShapeDtypeStruct((n, hw), BF16),
            jax.ShapeDtypeStruct((n, hw), BF16),
            jax.ShapeDtypeStruct((vw, n), BF16),
        ],
        compiler_params=_params(1),
        name="mla_prep",
    )(z, *wts, cos_t, sin_t)


def _mla_weights(q_norm_g, w_uq, kv_norm_g, w_ukv, qn_g, kn_g):
    pad = HEAD_SLOT - QK_DIM
    half = ROPE_DIM // 2
    perm = jnp.arange(HEAD_SLOT)
    perm = perm.at[NOPE_DIM:NOPE_DIM + half].set(jnp.arange(NOPE_DIM + half, NOPE_DIM + ROPE_DIM))
    perm = perm.at[NOPE_DIM + half:NOPE_DIM + ROPE_DIM].set(jnp.arange(NOPE_DIM, NOPE_DIM + half))

    def slots(w):
        wp = jnp.pad(w, ((0, 0), (0, 0), (0, pad)))
        return wp, wp[:, :, perm]

    def flat(w):
        return w.reshape(w.shape[0], H_A * HEAD_SLOT).astype(BF16)

    wq, wqs = slots(w_uq.reshape(Q_LORA, H_A, QK_DIM))
    w_kv = w_ukv.reshape(KV_LORA, H_A, NOPE_DIM + V_DIM)
    wkc, wkcs = slots(jnp.pad(w_kv[:, :, :NOPE_DIM], ((0, 0), (0, 0), (0, ROPE_DIM))))
    eye = jnp.zeros((HEAD_SLOT, QK_DIM), F32).at[jnp.arange(ROPE_DIM), NOPE_DIM + jnp.arange(ROPE_DIM)].set(1.0)
    wkr, wkrs = slots(jnp.broadcast_to(eye[:, None, :], (HEAD_SLOT, H_A, QK_DIM)))
    wvt = w_kv[:, :, NOPE_DIM:].reshape(KV_LORA, H_A * V_DIM).T.astype(BF16)

    def gains(g):
        gp = jnp.pad(g.astype(F32), (0, pad))
        return gp.reshape(1, HEAD_SLOT), gp[perm].reshape(1, HEAD_SLOT)

    gqn, gqns = gains(qn_g)
    gkn, gkns = gains(kn_g)
    return [q_norm_g.reshape(1, Q_LORA).astype(F32), kv_norm_g.reshape(1, KV_LORA).astype(F32),
            flat(wq), flat(wqs), flat(wkc), flat(wkcs), flat(wkr), flat(wkrs), wvt, gqn, gqns, gkn, gkns]


def _mla_tables(n_tok, rotary):
    cos = jnp.ones((n_tok, HEAD_SLOT), F32)
    sin = jnp.zeros((n_tok, HEAD_SLOT), F32)
    if rotary:
        rows = n_tok // GRID_W
        r = jnp.repeat(jnp.arange(rows, dtype=F32), GRID_W)
        col = jnp.tile(jnp.arange(GRID_W, dtype=F32), rows)
        nf = ROPE_DIM // 4
        f = ROPE_BASE ** (-jnp.arange(nf, dtype=F32) / nf)
        ang = jnp.concatenate([r[:, None] * f, col[:, None] * f], axis=-1)
        c, s = jnp.cos(ang), jnp.sin(ang)
        half = ROPE_DIM // 2
        cos = cos.at[:, NOPE_DIM:NOPE_DIM + half].set(c).at[:, NOPE_DIM + half:NOPE_DIM + ROPE_DIM].set(c)
        sin = sin.at[:, NOPE_DIM:NOPE_DIM + half].set(-s).at[:, NOPE_DIM + half:NOPE_DIM + ROPE_DIM].set(s)
    return cos, sin


def _attn_kernel(*refs, seg_lens):
    nseg = len(seg_lens)
    q_ref = refs[0]
    k_refs = refs[1:1 + nseg]
    vt_refs = refs[1 + nseg:1 + 2 * nseg]
    o_ref = refs[1 + 2 * nseg]
    s_s = refs[2 + 2 * nseg]
    chunks = [(g, r) for g, n in enumerate(seg_lens) for r in range(0, n, KEY_CHUNK)]
    tq = q_ref.shape[0]

    def fold(x, op):
        n = x.shape[0] // 8
        x = x.reshape(n, 8, tq)
        out = x[0]
        for i in range(1, n):
            out = op(out, x[i])
        return out

    def scores(h):
        sl = slice(h * HEAD_SLOT, (h + 1) * HEAD_SLOT)
        qh = q_ref[:, sl]
        mx = None
        for c, (g, r) in enumerate(chunks):
            s = lax.dot_general(k_refs[g][r:r + KEY_CHUNK, sl], qh, _NT, preferred_element_type=F32)
            s_s[h % 2, c] = s
            cm = fold(s, jnp.maximum)
            mx = cm if mx is None else jnp.maximum(mx, cm)
        return jnp.max(mx, axis=0, keepdims=True)

    def weighted_values(h, m):
        rows = slice(h * V_DIM, (h + 1) * V_DIM)
        lsum = None
        o_t = None
        for c, (g, r) in enumerate(chunks):
            pieces = []
            for r0 in range(0, KEY_CHUNK, SUB_ROWS):
                p = jnp.exp2(s_s[h % 2, c, r0:r0 + SUB_ROWS, :] - m)
                ps = fold(p, jnp.add)
                lsum = ps if lsum is None else lsum + ps
                pieces.append(p.astype(BF16))
            pb = jnp.concatenate(pieces, axis=0)
            t = jnp.dot(vt_refs[g][rows, r:r + KEY_CHUNK], pb, preferred_element_type=F32)
            o_t = t if o_t is None else o_t + t
        return o_t * (1.0 / jnp.sum(lsum, axis=0, keepdims=True))

    outs = []
    m_next = scores(0)
    for h in range(H_A):
        m_cur = m_next
        if h + 1 < H_A:
            m_next = scores(h + 1)
        outs.append(weighted_values(h, m_cur))
        if h % 2 == 1:
            j = h // 2
            o_ref[:, j * LANES:(j + 1) * LANES] = jnp.concatenate(outs[-2:], axis=0).T.astype(o_ref.dtype)


def _attn_call(q, ks, vts, t, tks):
    n = q.shape[0]
    tq = _pick_tile(t, 256)
    tpb = t // tq
    hw = H_A * HEAD_SLOT
    vw = H_A * V_DIM
    n_chunks = sum(tk // KEY_CHUNK for tk in tks)
    in_specs = [pl.BlockSpec((tq, hw), lambda i: (i, 0))]
    in_specs += [pl.BlockSpec((tk, hw), lambda i: (i // tpb, 0)) for tk in tks]
    in_specs += [pl.BlockSpec((vw, tk), lambda i: (0, i // tpb)) for tk in tks]
    return pl.pallas_call(
        functools.partial(_attn_kernel, seg_lens=tuple(tks)),
        grid=(n // tq,),
        in_specs=in_specs,
        out_specs=pl.BlockSpec((tq, vw), lambda i: (i, 0)),
        out_shape=jax.ShapeDtypeStruct((n, vw), BF16),
        scratch_shapes=[pltpu.VMEM((2, n_chunks, KEY_CHUNK, tq), F32)],
        compiler_params=_params(1),
        name="attention_%dseg" % len(tks),
    )(q, *ks, *vts)


def _rotate_pairs(x, cos, sin_lo, sin_hi):
    return x * cos + pltpu.roll(x, LANES - DK_R // 2, 1) * sin_lo + pltpu.roll(x, DK_R // 2, 1) * sin_hi


def _ret_kernel(lg_ref, qf_ref, kf_ref, vf_ref, cf_ref, slf_ref, shf_ref,
                qb_ref, kb_ref, vb_ref, cb_ref, slb_ref, shb_ref, s0f_ref, s0b_ref,
                of_ref, ob_ref, sff_ref, sfb_ref, df_s, db_s, sf_s, sb_s, *, chunk):
    b = pl.program_id(0)
    c = pl.program_id(1)
    nc = pl.num_programs(1)

    @pl.when(jnp.logical_and(b == 0, c == 0))
    def _():
        ii = lax.broadcasted_iota(jnp.int32, (chunk, chunk), 0)
        jj = lax.broadcasted_iota(jnp.int32, (chunk, chunk), 1)
        diff = (ii - jj).astype(F32)
        for h in range(H_R):
            df_s[h] = jnp.where(ii >= jj, jnp.exp(jnp.where(ii >= jj, diff, 0.0) * lg_ref[0, h]), 0.0)
            db_s[h] = jnp.where(jj > ii, jnp.exp(jnp.where(jj > ii, -diff, 0.0) * lg_ref[1, h]), 0.0)

    @pl.when(c == 0)
    def _():
        sf_s[...] = s0f_ref[0]
        sb_s[...] = s0b_ref[0]

    lane = lax.broadcasted_iota(jnp.int32, (1, LANES), 1)
    lo = lane < DK_R
    row = lax.broadcasted_iota(jnp.int32, (LANES, LANES), 0)
    colm = lax.broadcasted_iota(jnp.int32, (LANES, LANES), 1)
    blockdiag = (row < DK_R) == (colm < DK_R)
    pos = lax.broadcasted_iota(jnp.int32, (chunk, 1), 0).astype(F32)

    def one_direction(d, q_ref, k_ref, v_ref, cos_ref, sl_ref, sh_ref, d_s, s_s, o_ref):
        cos, sin_lo, sin_hi = cos_ref[...], sl_ref[...], sh_ref[...]
        for j in range(N_PAIR):
            sl = slice(j * LANES, (j + 1) * LANES)
            lg = jnp.where(lo, lg_ref[d, 2 * j], lg_ref[d, 2 * j + 1])
            q2 = _rotate_pairs(q_ref[:, sl].astype(F32), cos, sin_lo, sin_hi)
            k2 = _rotate_pairs(k_ref[:, sl].astype(F32), cos, sin_lo, sin_hi) * (DK_R ** -0.5)
            v2 = v_ref[:, sl]
            if d == 0:
                q_dec = jnp.exp((pos + 1.0) * lg)
                k_dec = jnp.exp((chunk - 1.0 - pos) * lg)
            else:
                q_dec = jnp.exp((chunk - pos) * lg)
                k_dec = jnp.exp(pos * lg)
            c_dec = jnp.exp(chunk * lg)
            s2 = s_s[j]
            o = jnp.dot((q2 * q_dec).astype(BF16), s2.astype(BF16), preferred_element_type=F32)
            k2b = k2.astype(BF16)
            for e in range(2):
                mask = lo if e == 0 else jnp.logical_not(lo)
                qe = jnp.where(mask, q2, 0.0).astype(BF16)
                sc = lax.dot_general(qe, k2b, _NT, preferred_element_type=F32) * d_s[2 * j + e]
                ve = jnp.where(mask, v2, jnp.zeros((), BF16))
                o = o + jnp.dot(sc.astype(BF16), ve, preferred_element_type=F32)
            o_ref[:, sl] = o.astype(o_ref.dtype)
            kd_t = (k2 * k_dec).T.astype(BF16)
            upd = jnp.dot(kd_t, v2, preferred_element_type=F32)
            s_s[j] = s2 * c_dec + jnp.where(blockdiag, upd, 0.0)

    one_direction(0, qf_ref, kf_ref, vf_ref, cf_ref, slf_ref, shf_ref, df_s, sf_s, of_ref)
    one_direction(1, qb_ref, kb_ref, vb_ref, cb_ref, slb_ref, shb_ref, db_s, sb_s, ob_ref)

    @pl.when(c == nc - 1)
    def _():
        sff_ref[0] = sf_s[...]
        sfb_ref[0] = sb_s[...]


def _ret_call(z, lg, tables, s0f, s0b, b, t):
    n = z.shape[0]
    chunk = _pick_tile(t, 256)
    nc = t // chunk
    cos_t, sin_lo_t, sin_hi_t = tables
    col0 = MLA_IN_PAD // RET_W

    def zspec(part, rev):
        if rev:
            return pl.BlockSpec((chunk, RET_W), lambda bi, ci: (bi * nc + nc - 1 - ci, col0 + part))
        return pl.BlockSpec((chunk, RET_W), lambda bi, ci: (bi * nc + ci, col0 + part))

    def tspec(rev):
        if rev:
            return pl.BlockSpec((chunk, LANES), lambda bi, ci: (nc - 1 - ci, 0))
        return pl.BlockSpec((chunk, LANES), lambda bi, ci: (ci, 0))

    st_spec = pl.BlockSpec((1, N_PAIR, LANES, LANES), lambda bi, ci: (bi, 0, 0, 0))
    in_specs = [pl.BlockSpec(memory_space=pltpu.SMEM)]
    in_specs += [zspec(0, False), zspec(1, False), zspec(2, False), tspec(False), tspec(False), tspec(False)]
    in_specs += [zspec(0, True), zspec(1, True), zspec(2, True), tspec(True), tspec(True), tspec(True)]
    in_specs += [st_spec, st_spec]
    out_specs = [
        pl.BlockSpec((chunk, RET_W), lambda bi, ci: (bi * nc + ci, 0)),
        pl.BlockSpec((chunk, RET_W), lambda bi, ci: (bi * nc + nc - 1 - ci, 0)),
        st_spec, st_spec,
    ]
    st_shape = jax.ShapeDtypeStruct((b, N_PAIR, LANES, LANES), F32)
    return pl.pallas_call(
        functools.partial(_ret_kernel, chunk=chunk),
        grid=(b, nc),
        in_specs=in_specs,
        out_specs=out_specs,
        out_shape=[jax.ShapeDtypeStruct((n, RET_W), BF16), jax.ShapeDtypeStruct((n, RET_W), BF16),
                   st_shape, st_shape],
        scratch_shapes=[
            pltpu.VMEM((H_R, chunk, chunk), F32), pltpu.VMEM((H_R, chunk, chunk), F32),
            pltpu.VMEM((N_PAIR, LANES, LANES), F32), pltpu.VMEM((N_PAIR, LANES, LANES), F32),
        ],
        compiler_params=_params(2),
        name="retention",
    )(lg, z, z, z, cos_t, sin_lo_t, sin_hi_t, z, z, z, cos_t, sin_lo_t, sin_hi_t, s0f, s0b)


def _ret_tables(n_tok, rotary):
    cos = jnp.ones((n_tok, LANES), F32)
    sin_lo = jnp.zeros((n_tok, LANES), F32)
    sin_hi = jnp.zeros((n_tok, LANES), F32)
    if rotary:
        nf = DK_R // 2
        theta = ROPE_BASE ** (-jnp.arange(nf, dtype=F32) / nf)
        ang = jnp.arange(n_tok, dtype=F32)[:, None] * theta
        c, s = jnp.cos(ang), jnp.sin(ang)
        z = jnp.zeros_like(s)
        cos = jnp.concatenate([c, c, c, c], axis=-1)
        sin_lo = jnp.concatenate([-s, z, -s, z], axis=-1)
        sin_hi = jnp.concatenate([z, s, z, s], axis=-1)
    return cos, sin_lo, sin_hi


def _mix_out_kernel(x_ref, a_ref, of_ref, ob_ref, g_ref, w_ref, mod_ref, o_ref):
    lane = lax.broadcasted_iota(jnp.int32, (1, LANES), 1)
    lo = lane < DV_R
    y = jnp.dot(a_ref[...], w_ref[0:H_A * V_DIM, :], preferred_element_type=F32)
    for j in range(N_PAIR):
        sl = slice(j * LANES, (j + 1) * LANES)
        o = of_ref[:, sl].astype(F32) + ob_ref[:, sl].astype(F32)
        s_lo = jnp.sum(jnp.where(lo, o, 0.0), axis=-1, keepdims=True)
        s_all = jnp.sum(o, axis=-1, keepdims=True)
        mu = jnp.where(lo, s_lo, s_all - s_lo) * (1.0 / DV_R)
        oc = o - mu
        q = oc * oc
        q_lo = jnp.sum(jnp.where(lo, q, 0.0), axis=-1, keepdims=True)
        q_all = jnp.sum(q, axis=-1, keepdims=True)
        var = jnp.where(lo, q_lo, q_all - q_lo) * (1.0 / DV_R)
        g = g_ref[:, sl].astype(F32)
        r = (oc * lax.rsqrt(var + EPS)) * (g * _sigmoid(g))
        row0 = H_A * V_DIM + j * LANES
        y = y + jnp.dot(r.astype(BF16), w_ref[row0:row0 + LANES, :], preferred_element_type=F32)
    o_ref[...] = x_ref[...] + mod_ref[0, 2:3, :] * y


def _mix_out_call(x2, a, o_f, o_b, z, w_out, mod, t):
    n = x2.shape[0]
    tm = _pick_tile(t, 512)
    tpb = t // tm
    gate_col = MLA_IN_PAD // RET_W + 3
    return pl.pallas_call(
        _mix_out_kernel,
        grid=(n // tm,),
        in_specs=[
            pl.BlockSpec((tm, D_MODEL), lambda i: (i, 0)),
            pl.BlockSpec((tm, H_A * V_DIM), lambda i: (i, 0)),
            pl.BlockSpec((tm, RET_W), lambda i: (i, 0)),
            pl.BlockSpec((tm, RET_W), lambda i: (i, 0)),
            pl.BlockSpec((tm, RET_W), lambda i: (i, gate_col)),
            pl.BlockSpec(w_out.shape, lambda i: (0, 0)),
            pl.BlockSpec((1, ADA_CHUNKS, D_MODEL), lambda i: (i // tpb, 0, 0)),
        ],
        out_specs=pl.BlockSpec((tm, D_MODEL), lambda i: (i, 0)),
        out_shape=jax.ShapeDtypeStruct((n, D_MODEL), F32),
        compiler_params=_params(1),
        name="mix_out",
    )(x2, a, o_f, o_b, z, w_out, mod)


def _mlp_kernel(x_ref, g_ref, mod_ref, w1_ref, w2_ref, o_ref):
    h = _norm_mod(x_ref[...], g_ref, mod_ref, 3, 4).astype(BF16)
    parts = []
    for k in range(0, D_FF, MLP_FF_CHUNK):
        a = jnp.maximum(jnp.dot(h, w1_ref[:, k:k + MLP_FF_CHUNK], preferred_element_type=F32), 0.0)
        parts.append((a * a).astype(BF16))
    y = jnp.dot(jnp.concatenate(parts, axis=1), w2_ref[...], preferred_element_type=F32)
    o_ref[...] = x_ref[...] + mod_ref[0, 5:6, :] * y


def _mlp_call(x2, g, mod, w1, w2, t):
    n = x2.shape[0]
    tm = _pick_tile(t, 512)
    tpb = t // tm
    once = pl.Buffered(1)
    return pl.pallas_call(
        _mlp_kernel,
        grid=(n // tm,),
        in_specs=[
            pl.BlockSpec((tm, D_MODEL), lambda i: (i, 0)),
            pl.BlockSpec((1, D_MODEL), lambda i: (0, 0)),
            pl.BlockSpec((1, ADA_CHUNKS, D_MODEL), lambda i: (i // tpb, 0, 0)),
            pl.BlockSpec((D_MODEL, D_FF), lambda i: (0, 0), pipeline_mode=once),
            pl.BlockSpec((D_FF, D_MODEL), lambda i: (0, 0), pipeline_mode=once),
        ],
        out_specs=pl.BlockSpec((tm, D_MODEL), lambda i: (i, 0)),
        out_shape=jax.ShapeDtypeStruct((n, D_MODEL), F32),
        compiler_params=_params(1),
        name="mlp",
    )(x2, g.reshape(1, D_MODEL), mod, w1, w2)


def _s5_compact_weights(p_f, p_b):
    st = lambda i: jnp.stack([p_f[i].astype(F32), p_b[i].astype(F32)])
    ar, ai, br, bi, cr, ci, log_dt = (st(i) for i in range(7))
    dt = jnp.exp(log_dt)[:, :, None]
    lr, li = dt * ar, dt * ai
    steps = np.arange(S5_L)

    def cexp(tau, x_r, x_i):
        mag = jnp.exp(tau * x_r)
        return mag * jnp.cos(tau * x_i), mag * jnp.sin(tau * x_i)

    def taus(table):
        return jnp.asarray(np.asarray(table, np.float32))[:, :, None, None, None]

    a1r, a1i = cexp(1.0, lr, li)
    nr, ni = a1r - 1.0, a1i
    den = ar * ar + ai * ai
    qr, qi = (nr * ar + ni * ai) / den, (ni * ar - nr * ai) / den
    bbr = qr[..., None] * br - qi[..., None] * bi
    bbi = qr[..., None] * bi + qi[..., None] * br

    slab = lambda x: x.reshape((2, N_SLAB, S5_GPS) + x.shape[2:])
    over_h = lambda x: jnp.repeat(x, S5_GROUP, axis=-1)
    lr_fp, li_fp = lr.reshape(2, N_SLAB, 1, S5_HALF), li.reshape(2, N_SLAB, 1, S5_HALF)
    lr_fh, li_fh = (over_h(slab(x).transpose(0, 1, 3, 2)) for x in (lr, li))
    cr_fh, ci_fh = (slab(x).transpose(0, 1, 4, 2, 3).reshape(2, N_SLAB, S5_STATE, LANES) for x in (cr, ci))
    bbr_fp, bbi_fp = (slab(x).transpose(0, 1, 4, 2, 3).reshape(2, N_SLAB, S5_GROUP, S5_HALF) for x in (bbr, bbi))
    bbr_fh, bbi_fh = (over_h(slab(x).transpose(0, 1, 4, 3, 2)) for x in (bbr, bbi))

    def c_pow(table):
        pr, pi = cexp(taus(table), lr_fh[:, None], li_fh[:, None])
        return cr_fh[:, None] * pr - ci_fh[:, None] * pi, cr_fh[:, None] * pi + ci_fh[:, None] * pr

    dr, di = c_pow([steps, steps])
    kern = jnp.sum(dr[:, :, :, None] * bbr_fh[:, None] - di[:, :, :, None] * bbi_fh[:, None], axis=4)
    lag_f = steps[None, :] - steps[:, None]
    toep = jnp.stack([kern[0][np.clip(lag_f, 0, S5_L - 1)], kern[1][np.clip(-lag_f, 0, S5_L - 1)]])
    valid = np.stack([lag_f >= 0, lag_f <= 0])[:, :, :, None, None, None]
    tt = jnp.where(valid, toep, 0.0).transpose(0, 3, 1, 4, 2, 5).reshape(2, N_SLAB, S5_L * S5_GROUP, S5_K)
    pr, pi = cexp(taus([S5_L - 1 - steps, steps]), lr_fp[:, None], li_fp[:, None])
    wr = pr * bbr_fp[:, None] - pi * bbi_fp[:, None]
    wi = pr * bbi_fp[:, None] + pi * bbr_fp[:, None]
    wt = jnp.stack([wr, wi], axis=4).transpose(0, 2, 1, 3, 4, 5).reshape(2, N_SLAB, S5_L * S5_GROUP, 2 * S5_HALF)
    vr, vi = c_pow([steps + 1, S5_L - steps])
    vt = jnp.stack([vr, -vi], axis=1).transpose(0, 3, 1, 4, 2, 5).reshape(2, N_SLAB, 2 * S5_STATE, S5_K)
    alr, ali = cexp(float(S5_L), lr, li)
    al = jnp.stack([alr[0], ali[0], alr[1], ali[1]]).reshape(4, N_SLAB, S5_HALF).transpose(1, 0, 2)
    coef = jnp.concatenate([al, jnp.zeros_like(al)], axis=1)
    return tt, wt, vt, coef


def _nm_s5_kernel(x_ref, g_ref, mod_ref, h_ref, u_ref, hs_s, *, nb, tk):
    x = x_ref[...]
    ms = jnp.mean(x * x, axis=-1, keepdims=True)
    y = x * lax.rsqrt(ms + EPS) * g_ref[...]
    h = y * (1.0 + mod_ref[:, 1:2, :]) + mod_ref[:, 0:1, :]
    h_ref[...] = h.astype(h_ref.dtype)
    h2 = h.reshape(nb * tk, D_MODEL)
    for j in range(N_SLAB):
        hs_s[j] = h2[:, j * LANES:(j + 1) * LANES]
    for kk in range(tk // S5_L):
        for s in range(S5_L):
            for j in range(N_SLAB):
                blk = hs_s[j, pl.ds(kk * S5_L + s, nb, stride=tk), :]
                u_ref[j, kk * nb:(kk + 1) * nb, s * LANES:(s + 1) * LANES] = blk.astype(u_ref.dtype)


def _norm_mod_s5_call(x3, g, mod):
    b, t, _ = x3.shape
    tk = S5_TOK
    rows = (tk // S5_L) * b
    return pl.pallas_call(
        functools.partial(_nm_s5_kernel, nb=b, tk=tk),
        grid=(t // tk,),
        in_specs=[
            pl.BlockSpec((b, tk, D_MODEL), lambda i: (0, i, 0)),
            pl.BlockSpec((1, D_MODEL), lambda i: (0, 0)),
            pl.BlockSpec((b, ADA_CHUNKS, D_MODEL), lambda i: (0, 0, 0)),
        ],
        out_specs=[
            pl.BlockSpec((b, tk, D_MODEL), lambda i: (0, i, 0)),
            pl.BlockSpec((N_SLAB, rows, S5_K), lambda i: (0, i, 0)),
        ],
        out_shape=[
            jax.ShapeDtypeStruct((b, t, D_MODEL), BF16),
            jax.ShapeDtypeStruct((N_SLAB, (t // S5_L) * b, S5_K), BF16),
        ],
        scratch_shapes=[pltpu.VMEM((N_SLAB, b * tk, LANES), F32)],
        compiler_params=_params(1),
        name="norm_mod_s5",
    )(x3, g.reshape(1, D_MODEL), mod)


def _s5_kernel(uc_ref, ulf_ref, ulb_ref, tt_ref, wt_ref, vt_ref, coef_ref,
               ycf_ref, ycb_ref, ylf_ref, ylb_ref, t_s, w_s, v_s, u_s, s_s, x_s, carry_s, *, nb, nblk):
    i = pl.program_id(1)

    @pl.when(i == 0)
    def _():
        lane = lax.broadcasted_iota(jnp.int32, (1, S5_K), 1)
        grp_out = (lane // S5_GROUP) % S5_GPS
        grp_state = (lane // S5_STATE) % S5_GPS
        rows = S5_GROUP
        for d in range(2):
            for s in range(S5_L):
                tc = tt_ref[d, 0, s * rows:(s + 1) * rows, :]
                wc = wt_ref[d, 0, s * rows:(s + 1) * rows, :]
                for gi in range(S5_GPS):
                    r0 = s * LANES + gi * rows
                    t_s[d, r0:r0 + rows, :] = jnp.where(grp_out == gi, tc, 0.0).astype(BF16)
                    w_s[d, r0:r0 + rows, :] = jnp.where(grp_state == gi, wc, 0.0).astype(BF16)
            for c in range(2):
                vc = vt_ref[d, 0, c * S5_STATE:(c + 1) * S5_STATE, :]
                for gi in range(S5_GPS):
                    r0 = c * S5_HALF + gi * S5_STATE
                    v_s[d, r0:r0 + S5_STATE, :] = jnp.where(grp_out == gi, vc, 0.0).astype(BF16)
        carry_s[...] = jnp.zeros_like(carry_s)
        u_s[0] = uc_ref[0]
        u_s[1] = uc_ref[0]

    @pl.when(i > 0)
    def _():
        u_s[0] = ulf_ref[0]
        u_s[1] = ulb_ref[0]

    ys = []
    for d in range(2):
        u = u_s[d]
        s_s[d] = jnp.dot(u, w_s[d], preferred_element_type=F32)
        a_r = coef_ref[0, 2 * d:2 * d + 1, :]
        a_i = coef_ref[0, 2 * d + 1:2 * d + 2, :]

        def body(k, carry, d=d, a_r=a_r, a_i=a_i):
            x_r, x_i = carry
            kk = k if d == 0 else nblk - 1 - k
            r = pl.multiple_of(kk * nb, nb)
            x_s[d, pl.ds(r, nb), 0:S5_HALF] = x_r
            x_s[d, pl.ds(r, nb), S5_HALF:2 * S5_HALF] = x_i
            s_r = s_s[d, pl.ds(r, nb), 0:S5_HALF]
            s_i = s_s[d, pl.ds(r, nb), S5_HALF:2 * S5_HALF]
            return a_r * x_r - a_i * x_i + s_r, a_r * x_i + a_i * x_r + s_i

        x_r, x_i = lax.fori_loop(0, nblk, body, (carry_s[d, :, 0:S5_HALF], carry_s[d, :, S5_HALF:2 * S5_HALF]))
        carry_s[d, :, 0:S5_HALF] = x_r
        carry_s[d, :, S5_HALF:2 * S5_HALF] = x_i
        y = jnp.dot(u, t_s[d], preferred_element_type=F32)
        y = y + jnp.dot(x_s[d].astype(BF16), v_s[d], preferred_element_type=F32)
        ys.append(y.astype(BF16))

    @pl.when(i == 0)
    def _():
        ycf_ref[0] = ys[0]
        ycb_ref[0] = ys[1]

    @pl.when(i > 0)
    def _():
        ylf_ref[0] = ys[0]
        ylb_ref[0] = ys[1]


def _s5_call(u_c, u_l, tt, wt, vt, coef, nb):
    rows = u_c.shape[1]
    n_lat = u_l.shape[1] // rows
    nblk = rows // nb
    tile = lambda fn: pl.BlockSpec((1, rows, S5_K), fn)
    wspec = lambda a: pl.BlockSpec((2, 1) + a.shape[2:], lambda j, i: (0, j, 0, 0))
    ctx_map = lambda j, i: (j, 0, 0)
    fwd_map = lambda j, i: (j, jnp.maximum(i - 1, 0), 0)
    bwd_map = lambda j, i: (j, jnp.minimum(n_lat - i, n_lat - 1), 0)
    yc = jax.ShapeDtypeStruct(u_c.shape, BF16)
    yl = jax.ShapeDtypeStruct(u_l.shape, BF16)
    return pl.pallas_call(
        functools.partial(_s5_kernel, nb=nb, nblk=nblk),
        grid=(N_SLAB, n_lat + 1),
        in_specs=[tile(ctx_map), tile(fwd_map), tile(bwd_map), wspec(tt), wspec(wt), wspec(vt),
                  pl.BlockSpec((1,) + coef.shape[1:], lambda j, i: (j, 0, 0))],
        out_specs=[tile(ctx_map), tile(ctx_map), tile(fwd_map), tile(bwd_map)],
        out_shape=[yc, yc, yl, yl],
        scratch_shapes=[
            pltpu.VMEM((2, S5_K, S5_K), BF16),
            pltpu.VMEM((2, S5_K, 2 * S5_HALF), BF16),
            pltpu.VMEM((2, 2 * S5_HALF, S5_K), BF16),
            pltpu.VMEM((2, rows, S5_K), BF16),
            pltpu.VMEM((2, rows, 2 * S5_HALF), F32),
            pltpu.VMEM((2, rows, 2 * S5_HALF), F32),
            pltpu.VMEM((2, nb, 2 * S5_HALF), F32),
        ],
        compiler_params=_params(2),
        name="s5_scan",
    )(u_c, u_l, u_l, tt, wt, vt, coef)


def _s5_out_kernel(x_ref, yf_ref, yb_ref, h_ref, d_ref, w_ref, mod_ref, o_ref, ys_s, *, nb, tk):
    for kk in range(tk // S5_L):
        rows = slice(kk * nb, (kk + 1) * nb)
        for s in range(S5_L):
            lanes = slice(s * LANES, (s + 1) * LANES)
            for j in range(N_SLAB):
                blk = yf_ref[j, rows, lanes].astype(F32) + yb_ref[j, rows, lanes].astype(F32)
                ys_s[j, pl.ds(kk * S5_L + s, nb, stride=tk), :] = blk
    y = jnp.concatenate([ys_s[j] for j in range(N_SLAB)], axis=1)
    y = y + d_ref[...] * h_ref[...].reshape(nb * tk, D_MODEL).astype(F32)
    g = 0.5 * y * (1.0 + jnp.tanh(math.sqrt(2.0 / math.pi) * (y + 0.044715 * (y * y * y))))
    z = jnp.dot(g.astype(BF16), w_ref[...], preferred_element_type=F32)
    out = z[:, :D_MODEL] * _sigmoid(z[:, D_MODEL:])
    o_ref[...] = x_ref[...] + mod_ref[:, 2:3, :] * out.reshape(nb, tk, D_MODEL)


def _s5_out_call(x3, y_f, y_b, h3, d_skip, w_glu, mod):
    b, t, _ = x3.shape
    tk = S5_TOK
    rows = (tk // S5_L) * b
    tile = pl.BlockSpec((b, tk, D_MODEL), lambda i: (0, i, 0))
    ytile = pl.BlockSpec((N_SLAB, rows, S5_K), lambda i: (0, i, 0))
    return pl.pallas_call(
        functools.partial(_s5_out_kernel, nb=b, tk=tk),
        grid=(t // tk,),
        in_specs=[
            tile, ytile, ytile, tile,
            pl.BlockSpec((1, D_MODEL), lambda i: (0, 0)),
            pl.BlockSpec(w_glu.shape, lambda i: (0, 0)),
            pl.BlockSpec((b, ADA_CHUNKS, D_MODEL), lambda i: (0, 0, 0)),
        ],
        out_specs=tile,
        out_shape=jax.ShapeDtypeStruct((b, t, D_MODEL), F32),
        scratch_shapes=[pltpu.VMEM((N_SLAB, b * tk, LANES), F32)],
        compiler_params=_params(1),
        name="s5_out",
    )(x3, y_f, y_b, h3, d_skip.reshape(1, D_MODEL).astype(F32), w_glu, mod)


def _even_layer(x2, xc2, mod_l, mod_c, b, t, tc, need_ctx, p):
    n_c = b * tc
    z_c = _norm_mod_matmul_call(xc2, p['norm1_g'], mod_c, n_c, p['w_in'])
    z_l = _norm_mod_matmul_call(x2, p['norm1_g'], mod_l, t, p['w_in'])
    q_c, k_c, vt_c = _mla_prep_call(z_c, n_c, p['mla'], *_mla_tables(n_c, False))
    q_l, k_l, vt_l = _mla_prep_call(z_l, t, p['mla'], *_mla_tables(t, True))
    a_l = _attn_call(q_l, [k_c, k_l], [vt_c, vt_l], t, [tc, t])
    zero = jnp.zeros((b, N_PAIR, LANES, LANES), F32)
    of_c, ob_c, s_cf, s_cb = _ret_call(z_c, p['lg'], _ret_tables(tc, False), zero, zero, b, tc)
    of_l, ob_l, _, _ = _ret_call(z_l, p['lg'], _ret_tables(t, True), s_cf, s_cb, b, t)
    x2 = _mix_out_call(x2, a_l, of_l, ob_l, z_l, p['w_out'], mod_l, t)
    if need_ctx:
        a_c = _attn_call(q_c, [k_c], [vt_c], tc, [tc])
        xc2 = _mix_out_call(xc2, a_c, of_c, ob_c, z_c, p['w_out'], mod_c, n_c)
    return x2, xc2


def _odd_layer(x2, xc2, mod_l, mod_c, b, t, tc, need_ctx, p):
    x3, xc3 = x2.reshape(b, t, D_MODEL), xc2.reshape(b, tc, D_MODEL)
    h_c, u_c = _norm_mod_s5_call(xc3, p['norm1_g'], mod_c)
    h_l, u_l = _norm_mod_s5_call(x3, p['norm1_g'], mod_l)
    tt, wt, vt, coef = p['s5']
    ycf, ycb, ylf, ylb = _s5_call(u_c, u_l, tt, wt, vt, coef, b)
    x2 = _s5_out_call(x3, ylf, ylb, h_l, p['d_skip'], p['w_glu'], mod_l).reshape(b * t, D_MODEL)
    if need_ctx:
        xc2 = _s5_out_call(xc3, ycf, ycb, h_c, p['d_skip'], p['w_glu'], mod_c).reshape(b * tc, D_MODEL)
    return x2, xc2


def kernel(x, c, ctx, c_ctx, ada_w, ada_b, norm1_g, norm2_g, mlp_w1, mlp_w2, w_in, mla_q_norm_g, mla_w_uq, mla_kv_norm_g, mla_w_ukv, mla_qn_g, mla_kn_g, ret_lg_f, ret_lg_b, w_out, s5_a_re_f, s5_a_im_f, s5_b_re_f, s5_b_im_f, s5_c_re_f, s5_c_im_f, s5_log_dt_f, s5_a_re_b, s5_a_im_b, s5_b_re_b, s5_b_im_b, s5_c_re_b, s5_c_im_b, s5_log_dt_b, s5_d, s5_w_glu):
    b, t, _ = x.shape
    tc = ctx.shape[1]
    depth = ada_w.shape[0]
    assert b % 8 == 0 and tc % KEY_CHUNK == 0 and t % tc == 0 and tc % S5_TOK == 0
    rows = -(-(b + 1) // 8) * 8
    cc = jnp.zeros((rows, D_MODEL), F32).at[:b].set(c.astype(F32)).at[b].set(c_ctx.astype(F32))
    mod = _ada_all(cc, ada_w.astype(F32), ada_b.astype(F32))
    x2 = x.reshape(b * t, D_MODEL).astype(F32)
    xc2 = ctx.reshape(b * tc, D_MODEL).astype(F32)
    w1_all, w2_all = mlp_w1.astype(BF16), mlp_w2.astype(BF16)
    w_in_all = jnp.concatenate([w_in[:, :, :MLA_IN], jnp.zeros(w_in.shape[:2] + (MLA_IN_PAD - MLA_IN,), w_in.dtype),
                                w_in[:, :, MLA_IN:]], axis=2).astype(BF16)
    w_out_all, w_glu_all = w_out.astype(BF16), s5_w_glu.astype(BF16)
    mla_all = jax.vmap(_mla_weights)(mla_q_norm_g, mla_w_uq, mla_kv_norm_g, mla_w_ukv, mla_qn_g, mla_kn_g)
    lg_all = jnp.stack([jnp.log1p(-jnp.exp2(ret_lg_f.astype(F32))), jnp.log1p(-jnp.exp2(ret_lg_b.astype(F32)))], axis=1)
    s5_all = jax.vmap(_s5_compact_weights)(
        (s5_a_re_f, s5_a_im_f, s5_b_re_f, s5_b_im_f, s5_c_re_f, s5_c_im_f, s5_log_dt_f),
        (s5_a_re_b, s5_a_im_b, s5_b_re_b, s5_b_im_b, s5_c_re_b, s5_c_im_b, s5_log_dt_b))
    for l in range(depth):
        need_ctx = l < depth - 1
        mod_l = mod[l, :b].reshape(b, ADA_CHUNKS, D_MODEL)
        mod_c = jnp.broadcast_to(mod[l, b].reshape(1, ADA_CHUNKS, D_MODEL), (b, ADA_CHUNKS, D_MODEL))
        if l % 2 == 0:
            e = l // 2
            p = dict(norm1_g=norm1_g[l], w_in=w_in_all[e], mla=[a[e] for a in mla_all], lg=lg_all[e], w_out=w_out_all[e])
            x2, xc2 = _even_layer(x2, xc2, mod_l, mod_c, b, t, tc, need_ctx, p)
        else:
            o = l // 2
            p = dict(norm1_g=norm1_g[l], d_skip=s5_d[o], w_glu=w_glu_all[o], s5=[a[o] for a in s5_all])
            x2, xc2 = _odd_layer(x2, xc2, mod_l, mod_c, b, t, tc, need_ctx, p)
        x2 = _mlp_call(x2, norm2_g[l], mod_l, w1_all[l], w2_all[l], t)
        if need_ctx:
            xc2 = _mlp_call(xc2, norm2_g[l], mod_c, w1_all[l], w2_all[l], b * tc)
    return x2.reshape(b, t, D_MODEL).astype(x.dtype)
```

```python
import functools
import math

import jax
import jax.numpy as jnp
import numpy as np
from jax import lax
from jax.experimental import pallas as pl
from jax.experimental.pallas import tpu as pltpu

F32 = jnp.float32
BF16 = jnp.bfloat16

D_MODEL = 1024
EPS = 1e-6
ADA_CHUNKS = 6
GRID_W = 64
ROPE_BASE = 10000.0
LANES = 128
N_SLAB = D_MODEL // LANES

H_A = 8
Q_LORA = 256
KV_LORA = 128
NOPE_DIM = 64
ROPE_DIM = 32
QK_DIM = NOPE_DIM + ROPE_DIM
V_DIM = 64
HEAD_SLOT = LANES
MLA_IN = Q_LORA + KV_LORA + ROPE_DIM
MLA_IN_PAD = 512
KEY_CHUNK = 256
SUB_ROWS = 64

H_R = 8
DK_R = 64
DV_R = 64
RET_W = H_R * DK_R
N_PAIR = H_R // 2

S5_GROUP = 16
S5_GROUPS = D_MODEL // S5_GROUP
S5_STATE = 64
S5_L = 8
S5_K = S5_L * LANES
S5_GPS = LANES // S5_GROUP
S5_HALF = S5_GPS * S5_STATE
S5_TOK = 32

D_FF = 4 * D_MODEL
MLP_FF_CHUNK = 512
Z_WIDTH = MLA_IN_PAD + 4 * RET_W

VMEM_LIMIT = 56 * 1024 * 1024

_NT = (((1,), (1,)), ((), ()))
_HP = lax.Precision.HIGHEST


def _params(n_grid):
    return pltpu.CompilerParams(
        dimension_semantics=("arbitrary",) * n_grid, vmem_limit_bytes=VMEM_LIMIT)


def _sigmoid(x):
    return 1.0 / (1.0 + jnp.exp(-x))


def _pick_tile(t, pref):
    tile = min(t, pref)
    while t % tile:
        tile //= 2
    return tile


def _ada_kernel(c_ref, w_ref, b_ref, o_ref):
    cc = c_ref[...]
    s = cc * _sigmoid(cc)
    o_ref[0] = jnp.dot(s, w_ref[0], preferred_element_type=F32, precision=_HP) + b_ref[0]


def _ada_all(cc, ada_w, ada_b):
    depth, _, width = ada_w.shape
    r = cc.shape[0]
    tn = 1536
    return pl.pallas_call(
        _ada_kernel,
        grid=(depth, width // tn),
        in_specs=[
            pl.BlockSpec((r, D_MODEL), lambda l, j: (0, 0)),
            pl.BlockSpec((1, D_MODEL, tn), lambda l, j: (l, 0, j)),
            pl.BlockSpec((1, 1, tn), lambda l, j: (l, 0, j)),
        ],
        out_specs=pl.BlockSpec((1, r, tn), lambda l, j: (l, 0, j)),
        out_shape=jax.ShapeDtypeStruct((depth, r, width), F32),
        compiler_params=_params(2),
        name="ada_mod",
    )(cc, ada_w, ada_b.reshape(depth, 1, width))


def _norm_mod(x, g_ref, mod_ref, shift_row, scale_row):
    ms = jnp.mean(x * x, axis=-1, keepdims=True)
    y = x * lax.rsqrt(ms + EPS) * g_ref[...]
    return y * (1.0 + mod_ref[0, scale_row:scale_row + 1, :]) + mod_ref[0, shift_row:shift_row + 1, :]


def _nmm_kernel(x_ref, g_ref, mod_ref, w_ref, o_ref):
    h = _norm_mod(x_ref[...], g_ref, mod_ref, 0, 1).astype(BF16)
    o_ref[...] = jnp.dot(h, w_ref[...], preferred_element_type=F32).astype(o_ref.dtype)


def _norm_mod_matmul_call(x2, g, mod, t, w):
    n = x2.shape[0]
    tm = _pick_tile(t, 512)
    tpb = t // tm
    n_out = w.shape[1]
    return pl.pallas_call(
        _nmm_kernel,
        grid=(n // tm,),
        in_specs=[
            pl.BlockSpec((tm, D_MODEL), lambda i: (i, 0)),
            pl.BlockSpec((1, D_MODEL), lambda i: (0, 0)),
            pl.BlockSpec((1, ADA_CHUNKS, D_MODEL), lambda i: (i // tpb, 0, 0)),
            pl.BlockSpec((D_MODEL, n_out), lambda i: (0, 0)),
        ],
        out_specs=pl.BlockSpec((tm, n_out), lambda i: (i, 0)),
        out_shape=jax.ShapeDtypeStruct((n, n_out), BF16),
        compiler_params=_params(1),
        name="norm_mod_w_in",
    )(x2, g.reshape(1, D_MODEL), mod, w)


def _mla_prep_kernel(z_ref, gq_ref, gkv_ref, wq_ref, wqs_ref, wkc_ref, wkcs_ref, wkr_ref, wkrs_ref,
                     wvt_ref, gqn_ref, gqns_ref, gkn_ref, gkns_ref, cos_ref, sin_ref,
                     q_ref, k_ref, vt_ref):
    cq = z_ref[:, 0:Q_LORA].astype(F32)
    ckv = z_ref[:, Q_LORA:Q_LORA + KV_LORA].astype(F32)
    kr = z_ref[:, Q_LORA + KV_LORA:MLA_IN_PAD]
    cqn = (cq * lax.rsqrt(jnp.mean(cq * cq, axis=-1, keepdims=True) + EPS) * gq_ref[...]).astype(BF16)
    ckn = (ckv * lax.rsqrt(jnp.mean(ckv * ckv, axis=-1, keepdims=True) + EPS) * gkv_ref[...]).astype(BF16)
    q = jnp.dot(cqn, wq_ref[...], preferred_element_type=F32)
    qs = jnp.dot(cqn, wqs_ref[...], preferred_element_type=F32)
    k = jnp.dot(ckn, wkc_ref[...], preferred_element_type=F32) + jnp.dot(kr, wkr_ref[...], preferred_element_type=F32)
    ks = jnp.dot(ckn, wkcs_ref[...], preferred_element_type=F32) + jnp.dot(kr, wkrs_ref[...], preferred_element_type=F32)
    vt_ref[...] = lax.dot_general(wvt_ref[...], ckn, _NT, preferred_element_type=F32).astype(vt_ref.dtype)
    cos = cos_ref[...]
    sin = sin_ref[...]
    q_scale = QK_DIM ** -0.5 * math.log2(math.e)
    for h in range(H_A):
        sl = slice(h * HEAD_SLOT, (h + 1) * HEAD_SLOT)
        qh = q[:, sl]
        rq = lax.rsqrt(jnp.sum(qh * qh, axis=-1, keepdims=True) * (1.0 / QK_DIM) + EPS)
        q_rot = (qh * gqn_ref[...] * cos + qs[:, sl] * gqns_ref[...] * sin) * (rq * q_scale)
        q_ref[:, sl] = q_rot.astype(q_ref.dtype)
        kh = k[:, sl]
        rk = lax.rsqrt(jnp.sum(kh * kh, axis=-1, keepdims=True) * (1.0 / QK_DIM) + EPS)
        k_rot = (kh * gkn_ref[...] * cos + ks[:, sl] * gkns_ref[...] * sin) * rk
        k_ref[:, sl] = k_rot.astype(k_ref.dtype)


def _mla_prep_call(z, t, wts, cos_t, sin_t):
    n = z.shape[0]
    tm = _pick_tile(t, 512)
    tpb = t // tm
    full = lambda a: pl.BlockSpec(a.shape, lambda i: (0,) * a.ndim)
    in_specs = [pl.BlockSpec((tm, MLA_IN_PAD), lambda i: (i, 0))] + [full(a) for a in wts] + [
        pl.BlockSpec((tm, HEAD_SLOT), lambda i: (i % tpb, 0)),
        pl.BlockSpec((tm, HEAD_SLOT), lambda i: (i % tpb, 0)),
    ]
    hw = H_A * HEAD_SLOT
    vw = H_A * V_DIM
    return pl.pallas_call(
        _mla_prep_kernel,
        grid=(n // tm,),
        in_specs=in_specs,
        out_specs=[
            pl.BlockSpec((tm, hw), lambda i: (i, 0)),
            pl.BlockSpec((tm, hw), lambda i: (i, 0)),
            pl.BlockSpec((vw, tm), lambda i: (0, i)),
        ],
        out_shape=[
            jax.ShapeDtypeStruct((n, hw), BF16),
            jax.ShapeDtypeStruct((n, hw), BF16),
            jax.ShapeDtypeStruct((vw, n), BF16),
        ],
        compiler_params=_params(1),
        name="mla_prep",
    )(z, *wts, cos_t, sin_t)


def _mla_weights(q_norm_g, w_uq, kv_norm_g, w_ukv, qn_g, kn_g):
    pad = HEAD_SLOT - QK_DIM
    half = ROPE_DIM // 2
    perm = jnp.arange(HEAD_SLOT)
    perm = perm.at[NOPE_DIM:NOPE_DIM + half].set(jnp.arange(NOPE_DIM + half, NOPE_DIM + ROPE_DIM))
    perm = perm.at[NOPE_DIM + half:NOPE_DIM + ROPE_DIM].set(jnp.arange(NOPE_DIM, NOPE_DIM + half))

    def slots(w):
        wp = jnp.pad(w, ((0, 0), (0, 0), (0, pad)))
        return wp, wp[:, :, perm]

    def flat(w):
        return w.reshape(w.shape[0], H_A * HEAD_SLOT).astype(BF16)

    wq, wqs = slots(w_uq.reshape(Q_LORA, H_A, QK_DIM))
    w_kv = w_ukv.reshape(KV_LORA, H_A, NOPE_DIM + V_DIM)
    wkc, wkcs = slots(jnp.pad(w_kv[:, :, :NOPE_DIM], ((0, 0), (0, 0), (0, ROPE_DIM))))
    eye = jnp.zeros((HEAD_SLOT, QK_DIM), F32).at[jnp.arange(ROPE_DIM), NOPE_DIM + jnp.arange(ROPE_DIM)].set(1.0)
    wkr, wkrs = slots(jnp.broadcast_to(eye[:, None, :], (HEAD_SLOT, H_A, QK_DIM)))
    wvt = w_kv[:, :, NOPE_DIM:].reshape(KV_LORA, H_A * V_DIM).T.astype(BF16)

    def gains(g):
        gp = jnp.pad(g.astype(F32), (0, pad))
        return gp.reshape(1, HEAD_SLOT), gp[perm].reshape(1, HEAD_SLOT)

    gqn, gqns = gains(qn_g)
    gkn, gkns = gains(kn_g)
    return [q_norm_g.reshape(1, Q_LORA).astype(F32), kv_norm_g.reshape(1, KV_LORA).astype(F32),
            flat(wq), flat(wqs), flat(wkc), flat(wkcs), flat(wkr), flat(wkrs), wvt, gqn, gqns, gkn, gkns]


def _mla_tables(n_tok, rotary):
    cos = jnp.ones((n_tok, HEAD_SLOT), F32)
    sin = jnp.zeros((n_tok, HEAD_SLOT), F32)
    if rotary:
        rows = n_tok // GRID_W
        r = jnp.repeat(jnp.arange(rows, dtype=F32), GRID_W)
        col = jnp.tile(jnp.arange(GRID_W, dtype=F32), rows)
        nf = ROPE_DIM // 4
        f = ROPE_BASE ** (-jnp.arange(nf, dtype=F32) / nf)
        ang = jnp.concatenate([r[:, None] * f, col[:, None] * f], axis=-1)
        c, s = jnp.cos(ang), jnp.sin(ang)
        half = ROPE_DIM // 2
        cos = cos.at[:, NOPE_DIM:NOPE_DIM + half].set(c).at[:, NOPE_DIM + half:NOPE_DIM + ROPE_DIM].set(c)
        sin = sin.at[:, NOPE_DIM:NOPE_DIM + half].set(-s).at[:, NOPE_DIM + half:NOPE_DIM + ROPE_DIM].set(s)
    return cos, sin


def _attn_kernel(*refs, seg_lens):
    nseg = len(seg_lens)
    q_ref = refs[0]
    k_refs = refs[1:1 + nseg]
    vt_refs = refs[1 + nseg:1 + 2 * nseg]
    o_ref = refs[1 + 2 * nseg]
    s_s = refs[2 + 2 * nseg]
    chunks = [(g, r) for g, n in enumerate(seg_lens) for r in range(0, n, KEY_CHUNK)]
    tq = q_ref.shape[0]

    def fold(x, op):
        n = x.shape[0] // 8
        x = x.reshape(n, 8, tq)
        out = x[0]
        for i in range(1, n):
            out = op(out, x[i])
        return out

    def scores(h):
        sl = slice(h * HEAD_SLOT, (h + 1) * HEAD_SLOT)
        qh = q_ref[:, sl]
        mx = None
        for c, (g, r) in enumerate(chunks):
            s = lax.dot_general(k_refs[g][r:r + KEY_CHUNK, sl], qh, _NT, preferred_element_type=F32)
            s_s[h % 2, c] = s
            cm = fold(s, jnp.maximum)
            mx = cm if mx is None else jnp.maximum(mx, cm)
        return jnp.max(mx, axis=0, keepdims=True)

    def weighted_values(h, m):
        rows = slice(h * V_DIM, (h + 1) * V_DIM)
        lsum = None
        o_t = None
        for c, (g, r) in enumerate(chunks):
            pieces = []
            for r0 in range(0, KEY_CHUNK, SUB_ROWS):
                p = jnp.exp2(s_s[h % 2, c, r0:r0 + SUB_ROWS, :] - m)
                ps = fold(p, jnp.add)
                lsum = ps if lsum is None else lsum + ps
                pieces.append(p.astype(BF16))
            pb = jnp.concatenate(pieces, axis=0)
            t = jnp.dot(vt_refs[g][rows, r:r + KEY_CHUNK], pb, preferred_element_type=F32)
            o_t = t if o_t is None else o_t + t
        return o_t * (1.0 / jnp.sum(lsum, axis=0, keepdims=True))

    outs = []
    m_next = scores(0)
    for h in range(H_A):
        m_cur = m_next
        if h + 1 < H_A:
            m_next = scores(h + 1)
        outs.append(weighted_values(h, m_cur))
        if h % 2 == 1:
            j = h // 2
            o_ref[:, j * LANES:(j + 1) * LANES] = jnp.concatenate(outs[-2:], axis=0).T.astype(o_ref.dtype)


def _attn_call(q, ks, vts, t, tks):
    n = q.shape[0]
    tq = _pick_tile(t, 256)
    tpb = t // tq
    hw = H_A * HEAD_SLOT
    vw = H_A * V_DIM
    n_chunks = sum(tk // KEY_CHUNK for tk in tks)
    in_specs = [pl.BlockSpec((tq, hw), lambda i: (i, 0))]
    in_specs += [pl.BlockSpec((tk, hw), lambda i: (i // tpb, 0)) for tk in tks]
    in_specs += [pl.BlockSpec((vw, tk), lambda i: (0, i // tpb)) for tk in tks]
    return pl.pallas_call(
        functools.partial(_attn_kernel, seg_lens=tuple(tks)),
        grid=(n // tq,),
        in_specs=in_specs,
        out_specs=pl.BlockSpec((tq, vw), lambda i: (i, 0)),
        out_shape=jax.ShapeDtypeStruct((n, vw), BF16),
        scratch_shapes=[pltpu.VMEM((2, n_chunks, KEY_CHUNK, tq), F32)],
        compiler_params=_params(1),
        name="attention_%dseg" % len(tks),
    )(q, *ks, *vts)


def _rotate_pairs(x, cos, sin_lo, sin_hi):
    return x * cos + pltpu.roll(x, LANES - DK_R // 2, 1) * sin_lo + pltpu.roll(x, DK_R // 2, 1) * sin_hi


def _ret_kernel(lg_ref, qf_ref, kf_ref, vf_ref, cf_ref, slf_ref, shf_ref,
                qb_ref, kb_ref, vb_ref, cb_ref, slb_ref, shb_ref, s0f_ref, s0b_ref,
                of_ref, ob_ref, sff_ref, sfb_ref, df_s, db_s, sf_s, sb_s, *, chunk):
    b = pl.program_id(0)
    c = pl.program_id(1)
    nc = pl.num_programs(1)

    @pl.when(jnp.logical_and(b == 0, c == 0))
    def _():
        ii = lax.broadcasted_iota(jnp.int32, (chunk, chunk), 0)
        jj = lax.broadcasted_iota(jnp.int32, (chunk, chunk), 1)
        diff = (ii - jj).astype(F32)
        for h in range(H_R):
            df_s[h] = jnp.where(ii >= jj, jnp.exp(jnp.where(ii >= jj, diff, 0.0) * lg_ref[0, h]), 0.0)
            db_s[h] = jnp.where(jj > ii, jnp.exp(jnp.where(jj > ii, -diff, 0.0) * lg_ref[1, h]), 0.0)

    @pl.when(c == 0)
    def _():
        sf_s[...] = s0f_ref[0]
        sb_s[...] = s0b_ref[0]

    lane = lax.broadcasted_iota(jnp.int32, (1, LANES), 1)
    lo = lane < DK_R
    row = lax.broadcasted_iota(jnp.int32, (LANES, LANES), 0)
    colm = lax.broadcasted_iota(jnp.int32, (LANES, LANES), 1)
    blockdiag = (row < DK_R) == (colm < DK_R)
    pos = lax.broadcasted_iota(jnp.int32, (chunk, 1), 0).astype(F32)
    dirs = ((qf_ref, kf_ref, vf_ref, cf_ref, slf_ref, shf_ref, df_s, sf_s, of_ref),
            (qb_ref, kb_ref, vb_ref, cb_ref, slb_ref, shb_ref, db_s, sb_s, ob_ref))
    units = [(d, j) for d in range(2) for j in range(N_PAIR)]

    ops = {}
    for d, j in units:
        q_ref, k_ref, v_ref, cos_ref, sl_ref, sh_ref = dirs[d][:6]
        sl = slice(j * LANES, (j + 1) * LANES)
        lg = jnp.where(lo, lg_ref[d, 2 * j], lg_ref[d, 2 * j + 1])
        q2 = _rotate_pairs(q_ref[:, sl].astype(F32), cos_ref[...], sl_ref[...], sh_ref[...])
        k2 = _rotate_pairs(k_ref[:, sl].astype(F32), cos_ref[...], sl_ref[...], sh_ref[...]) * (DK_R ** -0.5)
        if d == 0:
            q_dec = jnp.exp((pos + 1.0) * lg)
            k_dec = jnp.exp((chunk - 1.0 - pos) * lg)
        else:
            q_dec = jnp.exp((chunk - pos) * lg)
            k_dec = jnp.exp(pos * lg)
        ops[d, j] = dict(
            sl=sl, v2=v_ref[:, sl], c_dec=jnp.exp(chunk * lg), k2b=k2.astype(BF16),
            q_dec=(q2 * q_dec).astype(BF16), kd_t=(k2 * k_dec).T.astype(BF16),
            q_lo=jnp.where(lo, q2, 0.0).astype(BF16), q_hi=jnp.where(lo, 0.0, q2).astype(BF16))
    for d, j in units:
        u = ops[d, j]
        u['s2'] = dirs[d][7][j]
        u['o'] = jnp.dot(u['q_dec'], u['s2'].astype(BF16), preferred_element_type=F32)
        u['sc'] = [lax.dot_general(u[name], u['k2b'], _NT, preferred_element_type=F32) for name in ('q_lo', 'q_hi')]
    for d, j in units:
        u = ops[d, j]
        d_s, o_ref = dirs[d][6], dirs[d][8]
        o = u['o']
        for e in range(2):
            ve = jnp.where(lo if e == 0 else jnp.logical_not(lo), u['v2'], jnp.zeros((), BF16))
            o = o + jnp.dot((u['sc'][e] * d_s[2 * j + e]).astype(BF16), ve, preferred_element_type=F32)
        o_ref[:, u['sl']] = o.astype(o_ref.dtype)
    for d, j in units:
        u = ops[d, j]
        upd = jnp.dot(u['kd_t'], u['v2'], preferred_element_type=F32)
        dirs[d][7][j] = u['s2'] * u['c_dec'] + jnp.where(blockdiag, upd, 0.0)

    @pl.when(c == nc - 1)
    def _():
        sff_ref[0] = sf_s[...]
        sfb_ref[0] = sb_s[...]


def _ret_call(z, lg, tables, s0f, s0b, b, t):
    n = z.shape[0]
    chunk = _pick_tile(t, 256)
    nc = t // chunk
    cos_t, sin_lo_t, sin_hi_t = tables
    col0 = MLA_IN_PAD // RET_W

    def zspec(part, rev):
        if rev:
            return pl.BlockSpec((chunk, RET_W), lambda bi, ci: (bi * nc + nc - 1 - ci, col0 + part))
        return pl.BlockSpec((chunk, RET_W), lambda bi, ci: (bi * nc + ci, col0 + part))

    def tspec(rev):
        if rev:
            return pl.BlockSpec((chunk, LANES), lambda bi, ci: (nc - 1 - ci, 0))
        return pl.BlockSpec((chunk, LANES), lambda bi, ci: (ci, 0))

    st_spec = pl.BlockSpec((1, N_PAIR, LANES, LANES), lambda bi, ci: (bi, 0, 0, 0))
    in_specs = [pl.BlockSpec(memory_space=pltpu.SMEM)]
    in_specs += [zspec(0, False), zspec(1, False), zspec(2, False), tspec(False), tspec(False), tspec(False)]
    in_specs += [zspec(0, True), zspec(1, True), zspec(2, True), tspec(True), tspec(True), tspec(True)]
    in_specs += [st_spec, st_spec]
    out_specs = [
        pl.BlockSpec((chunk, RET_W), lambda bi, ci: (bi * nc + ci, 0)),
        pl.BlockSpec((chunk, RET_W), lambda bi, ci: (bi * nc + nc - 1 - ci, 0)),
        st_spec, st_spec,
    ]
    st_shape = jax.ShapeDtypeStruct((b, N_PAIR, LANES, LANES), F32)
    return pl.pallas_call(
        functools.partial(_ret_kernel, chunk=chunk),
        grid=(b, nc),
        in_specs=in_specs,
        out_specs=out_specs,
        out_shape=[jax.ShapeDtypeStruct((n, RET_W), BF16), jax.ShapeDtypeStruct((n, RET_W), BF16),
                   st_shape, st_shape],
        scratch_shapes=[
            pltpu.VMEM((H_R, chunk, chunk), F32), pltpu.VMEM((H_R, chunk, chunk), F32),
            pltpu.VMEM((N_PAIR, LANES, LANES), F32), pltpu.VMEM((N_PAIR, LANES, LANES), F32),
        ],
        compiler_params=_params(2),
        name="retention",
    )(lg, z, z, z, cos_t, sin_lo_t, sin_hi_t, z, z, z, cos_t, sin_lo_t, sin_hi_t, s0f, s0b)


def _ret_tables(n_tok, rotary):
    cos = jnp.ones((n_tok, LANES), F32)
    sin_lo = jnp.zeros((n_tok, LANES), F32)
    sin_hi = jnp.zeros((n_tok, LANES), F32)
    if rotary:
        nf = DK_R // 2
        theta = ROPE_BASE ** (-jnp.arange(nf, dtype=F32) / nf)
        ang = jnp.arange(n_tok, dtype=F32)[:, None] * theta
        c, s = jnp.cos(ang), jnp.sin(ang)
        z = jnp.zeros_like(s)
        cos = jnp.concatenate([c, c, c, c], axis=-1)
        sin_lo = jnp.concatenate([-s, z, -s, z], axis=-1)
        sin_hi = jnp.concatenate([z, s, z, s], axis=-1)
    return cos, sin_lo, sin_hi


def _mix_out_kernel(x_ref, a_ref, of_ref, ob_ref, g_ref, w_ref, mod_ref, o_ref):
    lane = lax.broadcasted_iota(jnp.int32, (1, LANES), 1)
    lo = lane < DV_R
    y = jnp.dot(a_ref[...], w_ref[0:H_A * V_DIM, :], preferred_element_type=F32)
    for j in range(N_PAIR):
        sl = slice(j * LANES, (j + 1) * LANES)
        o = of_ref[:, sl].astype(F32) + ob_ref[:, sl].astype(F32)
        s_lo = jnp.sum(jnp.where(lo, o, 0.0), axis=-1, keepdims=True)
        s_all = jnp.sum(o, axis=-1, keepdims=True)
        mu = jnp.where(lo, s_lo, s_all - s_lo) * (1.0 / DV_R)
        oc = o - mu
        q = oc * oc
        q_lo = jnp.sum(jnp.where(lo, q, 0.0), axis=-1, keepdims=True)
        q_all = jnp.sum(q, axis=-1, keepdims=True)
        var = jnp.where(lo, q_lo, q_all - q_lo) * (1.0 / DV_R)
        g = g_ref[:, sl].astype(F32)
        r = (oc * lax.rsqrt(var + EPS)) * (g * _sigmoid(g))
        row0 = H_A * V_DIM + j * LANES
        y = y + jnp.dot(r.astype(BF16), w_ref[row0:row0 + LANES, :], preferred_element_type=F32)
    o_ref[...] = x_ref[...] + mod_ref[0, 2:3, :] * y


def _mix_out_call(x2, a, o_f, o_b, z, w_out, mod, t):
    n = x2.shape[0]
    tm = _pick_tile(t, 512)
    tpb = t // tm
    gate_col = MLA_IN_PAD // RET_W + 3
    return pl.pallas_call(
        _mix_out_kernel,
        grid=(n // tm,),
        in_specs=[
            pl.BlockSpec((tm, D_MODEL), lambda i: (i, 0)),
            pl.BlockSpec((tm, H_A * V_DIM), lambda i: (i, 0)),
            pl.BlockSpec((tm, RET_W), lambda i: (i, 0)),
            pl.BlockSpec((tm, RET_W), lambda i: (i, 0)),
            pl.BlockSpec((tm, RET_W), lambda i: (i, gate_col)),
            pl.BlockSpec(w_out.shape, lambda i: (0, 0)),
            pl.BlockSpec((1, ADA_CHUNKS, D_MODEL), lambda i: (i // tpb, 0, 0)),
        ],
        out_specs=pl.BlockSpec((tm, D_MODEL), lambda i: (i, 0)),
        out_shape=jax.ShapeDtypeStruct((n, D_MODEL), F32),
        compiler_params=_params(1),
        name="mix_out",
    )(x2, a, o_f, o_b, z, w_out, mod)


def _mlp_kernel(x_ref, g_ref, mod_ref, w1_ref, w2_ref, o_ref):
    h = _norm_mod(x_ref[...], g_ref, mod_ref, 3, 4).astype(BF16)
    parts = []
    for k in range(0, D_FF, MLP_FF_CHUNK):
        a = jnp.maximum(jnp.dot(h, w1_ref[:, k:k + MLP_FF_CHUNK], preferred_element_type=F32), 0.0)
        parts.append((a * a).astype(BF16))
    y = jnp.dot(jnp.concatenate(parts, axis=1), w2_ref[...], preferred_element_type=F32)
    o_ref[...] = x_ref[...] + mod_ref[0, 5:6, :] * y


def _mlp_call(x2, g, mod, w1, w2, t):
    n = x2.shape[0]
    tm = _pick_tile(t, 512)
    tpb = t // tm
    once = pl.Buffered(1)
    return pl.pallas_call(
        _mlp_kernel,
        grid=(n // tm,),
        in_specs=[
            pl.BlockSpec((tm, D_MODEL), lambda i: (i, 0)),
            pl.BlockSpec((1, D_MODEL), lambda i: (0, 0)),
            pl.BlockSpec((1, ADA_CHUNKS, D_MODEL), lambda i: (i // tpb, 0, 0)),
            pl.BlockSpec((D_MODEL, D_FF), lambda i: (0, 0), pipeline_mode=once),
            pl.BlockSpec((D_FF, D_MODEL), lambda i: (0, 0), pipeline_mode=once),
        ],
        out_specs=pl.BlockSpec((tm, D_MODEL), lambda i: (i, 0)),
        out_shape=jax.ShapeDtypeStruct((n, D_MODEL), F32),
        compiler_params=_params(1),
        name="mlp",
    )(x2, g.reshape(1, D_MODEL), mod, w1, w2)


def _s5_compact_weights(p_f, p_b):
    st = lambda i: jnp.stack([p_f[i].astype(F32), p_b[i].astype(F32)])
    ar, ai, br, bi, cr, ci, log_dt = (st(i) for i in range(7))
    dt = jnp.exp(log_dt)[:, :, None]
    lr, li = dt * ar, dt * ai
    steps = np.arange(S5_L)

    def cexp(tau, x_r, x_i):
        mag = jnp.exp(tau * x_r)
        return mag * jnp.cos(tau * x_i), mag * jnp.sin(tau * x_i)

    def taus(table):
        return jnp.asarray(np.asarray(table, np.float32))[:, :, None, None, None]

    a1r, a1i = cexp(1.0, lr, li)
    nr, ni = a1r - 1.0, a1i
    den = ar * ar + ai * ai
    qr, qi = (nr * ar + ni * ai) / den, (ni * ar - nr * ai) / den
    bbr = qr[..., None] * br - qi[..., None] * bi
    bbi = qr[..., None] * bi + qi[..., None] * br

    slab = lambda x: x.reshape((2, N_SLAB, S5_GPS) + x.shape[2:])
    over_h = lambda x: jnp.repeat(x, S5_GROUP, axis=-1)
    lr_fp, li_fp = lr.reshape(2, N_SLAB, 1, S5_HALF), li.reshape(2, N_SLAB, 1, S5_HALF)
    lr_fh, li_fh = (over_h(slab(x).transpose(0, 1, 3, 2)) for x in (lr, li))
    cr_fh, ci_fh = (slab(x).transpose(0, 1, 4, 2, 3).reshape(2, N_SLAB, S5_STATE, LANES) for x in (cr, ci))
    bbr_fp, bbi_fp = (slab(x).transpose(0, 1, 4, 2, 3).reshape(2, N_SLAB, S5_GROUP, S5_HALF) for x in (bbr, bbi))
    bbr_fh, bbi_fh = (over_h(slab(x).transpose(0, 1, 4, 3, 2)) for x in (bbr, bbi))

    def c_pow(table):
        pr, pi = cexp(taus(table), lr_fh[:, None], li_fh[:, None])
        return cr_fh[:, None] * pr - ci_fh[:, None] * pi, cr_fh[:, None] * pi + ci_fh[:, None] * pr

    dr, di = c_pow([steps, steps])
    kern = jnp.sum(dr[:, :, :, None] * bbr_fh[:, None] - di[:, :, :, None] * bbi_fh[:, None], axis=4)
    lag_f = steps[None, :] - steps[:, None]
    toep = jnp.stack([kern[0][np.clip(lag_f, 0, S5_L - 1)], kern[1][np.clip(-lag_f, 0, S5_L - 1)]])
    valid = np.stack([lag_f >= 0, lag_f <= 0])[:, :, :, None, None, None]
    tt = jnp.where(valid, toep, 0.0).transpose(0, 3, 1, 4, 2, 5).reshape(2, N_SLAB, S5_L * S5_GROUP, S5_K)
    pr, pi = cexp(taus([S5_L - 1 - steps, steps]), lr_fp[:, None], li_fp[:, None])
    wr = pr * bbr_fp[:, None] - pi * bbi_fp[:, None]
    wi = pr * bbi_fp[:, None] + pi * bbr_fp[:, None]
    wt = jnp.stack([wr, wi], axis=4).transpose(0, 2, 1, 3, 4, 5).reshape(2, N_SLAB, S5_L * S5_GROUP, 2 * S5_HALF)
    vr, vi = c_pow([steps + 1, S5_L - steps])
    vt = jnp.stack([vr, -vi], axis=1).transpose(0, 3, 1, 4, 2, 5).reshape(2, N_SLAB, 2 * S5_STATE, S5_K)
    alr, ali = cexp(float(S5_L), lr, li)
    al = jnp.stack([alr[0], ali[0], alr[1], ali[1]]).reshape(4, N_SLAB, S5_HALF).transpose(1, 0, 2)
    coef = jnp.concatenate([al, jnp.zeros_like(al)], axis=1)
    return tt, wt, vt, coef


def _nm_s5_kernel(x_ref, g_ref, mod_ref, h_ref, u_ref, hs_s, *, nb, tk):
    x = x_ref[...]
    ms = jnp.mean(x * x, axis=-1, keepdims=True)
    y = x * lax.rsqrt(ms + EPS) * g_ref[...]
    h = y * (1.0 + mod_ref[:, 1:2, :]) + mod_ref[:, 0:1, :]
    h_ref[...] = h.astype(h_ref.dtype)
    h2 = h.reshape(nb * tk, D_MODEL)
    for j in range(N_SLAB):
        hs_s[j] = h2[:, j * LANES:(j + 1) * LANES]
    for kk in range(tk // S5_L):
        for s in range(S5_L):
            for j in range(N_SLAB):
                blk = hs_s[j, pl.ds(kk * S5_L + s, nb, stride=tk), :]
                u_ref[j, kk * nb:(kk + 1) * nb, s * LANES:(s + 1) * LANES] = blk.astype(u_ref.dtype)


def _norm_mod_s5_call(x3, g, mod):
    b, t, _ = x3.shape
    tk = S5_TOK
    rows = (tk // S5_L) * b
    return pl.pallas_call(
        functools.partial(_nm_s5_kernel, nb=b, tk=tk),
        grid=(t // tk,),
        in_specs=[
            pl.BlockSpec((b, tk, D_MODEL), lambda i: (0, i, 0)),
            pl.BlockSpec((1, D_MODEL), lambda i: (0, 0)),
            pl.BlockSpec((b, ADA_CHUNKS, D_MODEL), lambda i: (0, 0, 0)),
        ],
        out_specs=[
            pl.BlockSpec((b, tk, D_MODEL), lambda i: (0, i, 0)),
            pl.BlockSpec((N_SLAB, rows, S5_K), lambda i: (0, i, 0)),
        ],
        out_shape=[
            jax.ShapeDtypeStruct((b, t, D_MODEL), BF16),
            jax.ShapeDtypeStruct((N_SLAB, (t // S5_L) * b, S5_K), BF16),
        ],
        scratch_shapes=[pltpu.VMEM((N_SLAB, b * tk, LANES), F32)],
        compiler_params=_params(1),
        name="norm_mod_s5",
    )(x3, g.reshape(1, D_MODEL), mod)


def _s5_kernel(uc_ref, ulf_ref, ulb_ref, tt_ref, wt_ref, vt_ref, coef_ref,
               ycf_ref, ycb_ref, ylf_ref, ylb_ref, t_s, w_s, v_s, u_s, s_s, x_s, carry_s, *, nb, nblk):
    i = pl.program_id(1)

    @pl.when(i == 0)
    def _():
        lane = lax.broadcasted_iota(jnp.int32, (1, S5_K), 1)
        grp_out = (lane // S5_GROUP) % S5_GPS
        grp_state = (lane // S5_STATE) % S5_GPS
        rows = S5_GROUP
        for d in range(2):
            for s in range(S5_L):
                tc = tt_ref[d, 0, s * rows:(s + 1) * rows, :]
                wc = wt_ref[d, 0, s * rows:(s + 1) * rows, :]
                for gi in range(S5_GPS):
                    r0 = s * LANES + gi * rows
                    t_s[d, r0:r0 + rows, :] = jnp.where(grp_out == gi, tc, 0.0).astype(BF16)
                    w_s[d, r0:r0 + rows, :] = jnp.where(grp_state == gi, wc, 0.0).astype(BF16)
            for c in range(2):
                vc = vt_ref[d, 0, c * S5_STATE:(c + 1) * S5_STATE, :]
                for gi in range(S5_GPS):
                    r0 = c * S5_HALF + gi * S5_STATE
                    v_s[d, r0:r0 + S5_STATE, :] = jnp.where(grp_out == gi, vc, 0.0).astype(BF16)
        carry_s[...] = jnp.zeros_like(carry_s)
        u_s[0] = uc_ref[0]
        u_s[1] = uc_ref[0]

    @pl.when(i > 0)
    def _():
        u_s[0] = ulf_ref[0]
        u_s[1] = ulb_ref[0]

    for d in range(2):
        s_s[d] = jnp.dot(u_s[d], w_s[d], preferred_element_type=F32)
    ys = [jnp.dot(u_s[d], t_s[d], preferred_element_type=F32) for d in range(2)]
    for d in range(2):
        a_r = coef_ref[0, 2 * d:2 * d + 1, :]
        a_i = coef_ref[0, 2 * d + 1:2 * d + 2, :]
        x_r, x_i = carry_s[d, :, 0:S5_HALF], carry_s[d, :, S5_HALF:2 * S5_HALF]
        for k in range(nblk):
            rows = slice((k if d == 0 else nblk - 1 - k) * nb, (k if d == 0 else nblk - 1 - k) * nb + nb)
            x_s[d, rows, 0:S5_HALF] = x_r
            x_s[d, rows, S5_HALF:2 * S5_HALF] = x_i
            s_r = s_s[d, rows, 0:S5_HALF]
            s_i = s_s[d, rows, S5_HALF:2 * S5_HALF]
            x_r, x_i = a_r * x_r - a_i * x_i + s_r, a_r * x_i + a_i * x_r + s_i
        carry_s[d, :, 0:S5_HALF] = x_r
        carry_s[d, :, S5_HALF:2 * S5_HALF] = x_i
    for d in range(2):
        y = ys[d] + jnp.dot(x_s[d].astype(BF16), v_s[d], preferred_element_type=F32)
        ys[d] = y.astype(BF16)

    @pl.when(i == 0)
    def _():
        ycf_ref[0] = ys[0]
        ycb_ref[0] = ys[1]

    @pl.when(i > 0)
    def _():
        ylf_ref[0] = ys[0]
        ylb_ref[0] = ys[1]


def _s5_call(u_c, u_l, tt, wt, vt, coef, nb):
    rows = u_c.shape[1]
    n_lat = u_l.shape[1] // rows
    nblk = rows // nb
    tile = lambda fn: pl.BlockSpec((1, rows, S5_K), fn)
    wspec = lambda a: pl.BlockSpec((2, 1) + a.shape[2:], lambda j, i: (0, j, 0, 0))
    ctx_map = lambda j, i: (j, 0, 0)
    fwd_map = lambda j, i: (j, jnp.maximum(i - 1, 0), 0)
    bwd_map = lambda j, i: (j, jnp.minimum(n_lat - i, n_lat - 1), 0)
    yc = jax.ShapeDtypeStruct(u_c.shape, BF16)
    yl = jax.ShapeDtypeStruct(u_l.shape, BF16)
    return pl.pallas_call(
        functools.partial(_s5_kernel, nb=nb, nblk=nblk),
        grid=(N_SLAB, n_lat + 1),
        in_specs=[tile(ctx_map), tile(fwd_map), tile(bwd_map), wspec(tt), wspec(wt), wspec(vt),
                  pl.BlockSpec((1,) + coef.shape[1:], lambda j, i: (j, 0, 0))],
        out_specs=[tile(ctx_map), tile(ctx_map), tile(fwd_map), tile(bwd_map)],
        out_shape=[yc, yc, yl, yl],
        scratch_shapes=[
            pltpu.VMEM((2, S5_K, S5_K), BF16),
            pltpu.VMEM((2, S5_K, 2 * S5_HALF), BF16),
            pltpu.VMEM((2, 2 * S5_HALF, S5_K), BF16),
            pltpu.VMEM((2, rows, S5_K), BF16),
            pltpu.VMEM((2, rows, 2 * S5_HALF), F32),
            pltpu.VMEM((2, rows, 2 * S5_HALF), F32),
            pltpu.VMEM((2, nb, 2 * S5_HALF), F32),
        ],
        compiler_params=_params(2),
        name="s5_scan",
    )(u_c, u_l, u_l, tt, wt, vt, coef)


def _s5_out_kernel(x_ref, yf_ref, yb_ref, h_ref, d_ref, w_ref, mod_ref, o_ref, ys_s, *, nb, tk):
    for kk in range(tk // S5_L):
        rows = slice(kk * nb, (kk + 1) * nb)
        for s in range(S5_L):
            lanes = slice(s * LANES, (s + 1) * LANES)
            for j in range(N_SLAB):
                blk = yf_ref[j, rows, lanes].astype(F32) + yb_ref[j, rows, lanes].astype(F32)
                ys_s[j, pl.ds(kk * S5_L + s, nb, stride=tk), :] = blk
    y = jnp.concatenate([ys_s[j] for j in range(N_SLAB)], axis=1)
    y = y + d_ref[...] * h_ref[...].reshape(nb * tk, D_MODEL).astype(F32)
    g = 0.5 * y * (1.0 + jnp.tanh(math.sqrt(2.0 / math.pi) * (y + 0.044715 * (y * y * y))))
    z = jnp.dot(g.astype(BF16), w_ref[...], preferred_element_type=F32)
    out = z[:, :D_MODEL] * _sigmoid(z[:, D_MODEL:])
    o_ref[...] = x_ref[...] + mod_ref[:, 2:3, :] * out.reshape(nb, tk, D_MODEL)


def _s5_out_call(x3, y_f, y_b, h3, d_skip, w_glu, mod):
    b, t, _ = x3.shape
    tk = S5_TOK
    rows = (tk // S5_L) * b
    tile = pl.BlockSpec((b, tk, D_MODEL), lambda i: (0, i, 0))
    ytile = pl.BlockSpec((N_SLAB, rows, S5_K), lambda i: (0, i, 0))
    return pl.pallas_call(
        functools.partial(_s5_out_kernel, nb=b, tk=tk),
        grid=(t // tk,),
        in_specs=[
            tile, ytile, ytile, tile,
            pl.BlockSpec((1, D_MODEL), lambda i: (0, 0)),
            pl.BlockSpec(w_glu.shape, lambda i: (0, 0)),
            pl.BlockSpec((b, ADA_CHUNKS, D_MODEL), lambda i: (0, 0, 0)),
        ],
        out_specs=tile,
        out_shape=jax.ShapeDtypeStruct((b, t, D_MODEL), F32),
        scratch_shapes=[pltpu.VMEM((N_SLAB, b * tk, LANES), F32)],
        compiler_params=_params(1),
        name="s5_out",
    )(x3, y_f, y_b, h3, d_skip.reshape(1, D_MODEL).astype(F32), w_glu, mod)


def _even_layer(x2, xc2, mod_l, mod_c, b, t, tc, need_ctx, p):
    n_c = b * tc
    z_c = _norm_mod_matmul_call(xc2, p['norm1_g'], mod_c, n_c, p['w_in'])
    z_l = _norm_mod_matmul_call(x2, p['norm1_g'], mod_l, t, p['w_in'])
    q_c, k_c, vt_c = _mla_prep_call(z_c, n_c, p['mla'], *_mla_tables(n_c, False))
    q_l, k_l, vt_l = _mla_prep_call(z_l, t, p['mla'], *_mla_tables(t, True))
    a_l = _attn_call(q_l, [k_c, k_l], [vt_c, vt_l], t, [tc, t])
    zero = jnp.zeros((b, N_PAIR, LANES, LANES), F32)
    of_c, ob_c, s_cf, s_cb = _ret_call(z_c, p['lg'], _ret_tables(tc, False), zero, zero, b, tc)
    of_l, ob_l, _, _ = _ret_call(z_l, p['lg'], _ret_tables(t, True), s_cf, s_cb, b, t)
    x2 = _mix_out_call(x2, a_l, of_l, ob_l, z_l, p['w_out'], mod_l, t)
    if need_ctx:
        a_c = _attn_call(q_c, [k_c], [vt_c], tc, [tc])
        xc2 = _mix_out_call(xc2, a_c, of_c, ob_c, z_c, p['w_out'], mod_c, n_c)
    return x2, xc2


def _odd_layer(x2, xc2, mod_l, mod_c, b, t, tc, need_ctx, p):
    x3, xc3 = x2.reshape(b, t, D_MODEL), xc2.reshape(b, tc, D_MODEL)
    h_c, u_c = _norm_mod_s5_call(xc3, p['norm1_g'], mod_c)
    h_l, u_l = _norm_mod_s5_call(x3, p['norm1_g'], mod_l)
    tt, wt, vt, coef = p['s5']
    ycf, ycb, ylf, ylb = _s5_call(u_c, u_l, tt, wt, vt, coef, b)
    x2 = _s5_out_call(x3, ylf, ylb, h_l, p['d_skip'], p['w_glu'], mod_l).reshape(b * t, D_MODEL)
    if need_ctx:
        xc2 = _s5_out_call(xc3, ycf, ycb, h_c, p['d_skip'], p['w_glu'], mod_c).reshape(b * tc, D_MODEL)
    return x2, xc2


def kernel(x, c, ctx, c_ctx, ada_w, ada_b, norm1_g, norm2_g, mlp_w1, mlp_w2, w_in, mla_q_norm_g, mla_w_uq, mla_kv_norm_g, mla_w_ukv, mla_qn_g, mla_kn_g, ret_lg_f, ret_lg_b, w_out, s5_a_re_f, s5_a_im_f, s5_b_re_f, s5_b_im_f, s5_c_re_f, s5_c_im_f, s5_log_dt_f, s5_a_re_b, s5_a_im_b, s5_b_re_b, s5_b_im_b, s5_c_re_b, s5_c_im_b, s5_log_dt_b, s5_d, s5_w_glu):
    b, t, _ = x.shape
    tc = ctx.shape[1]
    depth = ada_w.shape[0]
    assert b % 8 == 0 and tc % KEY_CHUNK == 0 and t % tc == 0 and tc % S5_TOK == 0
    rows = -(-(b + 1) // 8) * 8
    cc = jnp.zeros((rows, D_MODEL), F32).at[:b].set(c.astype(F32)).at[b].set(c_ctx.astype(F32))
    mod = _ada_all(cc, ada_w.astype(F32), ada_b.astype(F32))
    x2 = x.reshape(b * t, D_MODEL).astype(F32)
    xc2 = ctx.reshape(b * tc, D_MODEL).astype(F32)
    w1_all, w2_all = mlp_w1.astype(BF16), mlp_w2.astype(BF16)
    w_in_all = jnp.concatenate([w_in[:, :, :MLA_IN], jnp.zeros(w_in.shape[:2] + (MLA_IN_PAD - MLA_IN,), w_in.dtype),
                                w_in[:, :, MLA_IN:]], axis=2).astype(BF16)
    w_out_all, w_glu_all = w_out.astype(BF16), s5_w_glu.astype(BF16)
    mla_all = jax.vmap(_mla_weights)(mla_q_norm_g, mla_w_uq, mla_kv_norm_g, mla_w_ukv, mla_qn_g, mla_kn_g)
    lg_all = jnp.stack([jnp.log1p(-jnp.exp2(ret_lg_f.astype(F32))), jnp.log1p(-jnp.exp2(ret_lg_b.astype(F32)))], axis=1)
    s5_all = jax.vmap(_s5_compact_weights)(
        (s5_a_re_f, s5_a_im_f, s5_b_re_f, s5_b_im_f, s5_c_re_f, s5_c_im_f, s5_log_dt_f),
        (s5_a_re_b, s5_a_im_b, s5_b_re_b, s5_b_im_b, s5_c_re_b, s5_c_im_b, s5_log_dt_b))
    for l in range(depth):
        need_ctx = l < depth - 1
        mod_l = mod[l, :b].reshape(b, ADA_CHUNKS, D_MODEL)
        mod_c = jnp.broadcast_to(mod[l, b].reshape(1, ADA_CHUNKS, D_MODEL), (b, ADA_CHUNKS, D_MODEL))
        if l % 2 == 0:
            e = l // 2
            p = dict(norm1_g=norm1_g[l], w_in=w_in_all[e], mla=[a[e] for a in mla_all], lg=lg_all[e], w_out=w_out_all[e])
            x2, xc2 = _even_layer(x2, xc2, mod_l, mod_c, b, t, tc, need_ctx, p)
        else:
            o = l // 2
            p = dict(norm1_g=norm1_g[l], d_skip=s5_d[o], w_glu=w_glu_all[o], s5=[a[o] for a in s5_all])
            x2, xc2 = _odd_layer(x2, xc2, mod_l, mod_c, b, t, tc, need_ctx, p)
        x2 = _mlp_call(x2, norm2_g[l], mod_l, w1_all[l], w2_all[l], t)
        if need_ctx:
            xc2 = _mlp_call(xc2, norm2_g[l], mod_c, w1_all[l], w2_all[l], b * tc)
    return x2.reshape(b, t, D_MODEL).astype(x.dtype)
```

```python
import functools
import math

import jax
import jax.numpy as jnp
import numpy as np
from jax import lax
from jax.experimental import pallas as pl
from jax.experimental.pallas import tpu as pltpu

F32 = jnp.float32
BF16 = jnp.bfloat16

D_MODEL = 1024
EPS = 1e-6
ADA_CHUNKS = 6
GRID_W = 64
ROPE_BASE = 10000.0
LANES = 128
N_SLAB = D_MODEL // LANES

H_A = 8
Q_LORA = 256
KV_LORA = 128
NOPE_DIM = 64
ROPE_DIM = 32
QK_DIM = NOPE_DIM + ROPE_DIM
V_DIM = 64
HEAD_SLOT = LANES
MLA_IN = Q_LORA + KV_LORA + ROPE_DIM
MLA_IN_PAD = 512
KEY_CHUNK = 256
SUB_ROWS = 64

H_R = 8
DK_R = 64
DV_R = 64
RET_W = H_R * DK_R
N_PAIR = H_R // 2

S5_GROUP = 16
S5_GROUPS = D_MODEL // S5_GROUP
S5_STATE = 64
S5_L = 8
S5_K = S5_L * LANES
S5_GPS = LANES // S5_GROUP
S5_HALF = S5_GPS * S5_STATE
S5_TOK = 32
S5_PITCH_PAD = 8

D_FF = 4 * D_MODEL
MLP_FF_CHUNK = 512
Z_WIDTH = MLA_IN_PAD + 4 * RET_W

VMEM_LIMIT = 56 * 1024 * 1024

_NT = (((1,), (1,)), ((), ()))
_HP = lax.Precision.HIGHEST


def _params(n_grid):
    return pltpu.CompilerParams(
        dimension_semantics=("arbitrary",) * n_grid, vmem_limit_bytes=VMEM_LIMIT)


def _sigmoid(x):
    return 1.0 / (1.0 + jnp.exp(-x))


def _pick_tile(t, pref):
    tile = min(t, pref)
    while t % tile:
        tile //= 2
    return tile


def _ada_kernel(c_ref, w_ref, b_ref, o_ref):
    cc = c_ref[...]
    s = cc * _sigmoid(cc)
    o_ref[0] = jnp.dot(s, w_ref[0], preferred_element_type=F32, precision=_HP) + b_ref[0]


def _ada_all(cc, ada_w, ada_b):
    depth, _, width = ada_w.shape
    r = cc.shape[0]
    tn = 1536
    return pl.pallas_call(
        _ada_kernel,
        grid=(depth, width // tn),
        in_specs=[
            pl.BlockSpec((r, D_MODEL), lambda l, j: (0, 0)),
            pl.BlockSpec((1, D_MODEL, tn), lambda l, j: (l, 0, j)),
            pl.BlockSpec((1, 1, tn), lambda l, j: (l, 0, j)),
        ],
        out_specs=pl.BlockSpec((1, r, tn), lambda l, j: (l, 0, j)),
        out_shape=jax.ShapeDtypeStruct((depth, r, width), F32),
        compiler_params=_params(2),
        name="ada_mod",
    )(cc, ada_w, ada_b.reshape(depth, 1, width))


def _norm_mod(x, g_ref, mod_ref, shift_row, scale_row):
    ms = jnp.mean(x * x, axis=-1, keepdims=True)
    y = x * lax.rsqrt(ms + EPS) * g_ref[...]
    return y * (1.0 + mod_ref[0, scale_row:scale_row + 1, :]) + mod_ref[0, shift_row:shift_row + 1, :]


def _nmm_kernel(x_ref, g_ref, mod_ref, w_ref, o_ref):
    h = _norm_mod(x_ref[...], g_ref, mod_ref, 0, 1).astype(BF16)
    o_ref[...] = jnp.dot(h, w_ref[...], preferred_element_type=F32).astype(o_ref.dtype)


def _norm_mod_matmul_call(x2, g, mod, t, w):
    n = x2.shape[0]
    tm = _pick_tile(t, 512)
    tpb = t // tm
    n_out = w.shape[1]
    return pl.pallas_call(
        _nmm_kernel,
        grid=(n // tm,),
        in_specs=[
            pl.BlockSpec((tm, D_MODEL), lambda i: (i, 0)),
            pl.BlockSpec((1, D_MODEL), lambda i: (0, 0)),
            pl.BlockSpec((1, ADA_CHUNKS, D_MODEL), lambda i: (i // tpb, 0, 0)),
            pl.BlockSpec((D_MODEL, n_out), lambda i: (0, 0)),
        ],
        out_specs=pl.BlockSpec((tm, n_out), lambda i: (i, 0)),
        out_shape=jax.ShapeDtypeStruct((n, n_out), BF16),
        compiler_params=_params(1),
        name="norm_mod_w_in",
    )(x2, g.reshape(1, D_MODEL), mod, w)


def _mla_prep_kernel(z_ref, gq_ref, gkv_ref, wq_ref, wqs_ref, wkc_ref, wkcs_ref, wkr_ref, wkrs_ref,
                     wvt_ref, gqn_ref, gqns_ref, gkn_ref, gkns_ref, cos_ref, sin_ref,
                     q_ref, k_ref, vt_ref):
    cq = z_ref[:, 0:Q_LORA].astype(F32)
    ckv = z_ref[:, Q_LORA:Q_LORA + KV_LORA].astype(F32)
    kr = z_ref[:, Q_LORA + KV_LORA:MLA_IN_PAD]
    cqn = (cq * lax.rsqrt(jnp.mean(cq * cq, axis=-1, keepdims=True) + EPS) * gq_ref[...]).astype(BF16)
    ckn = (ckv * lax.rsqrt(jnp.mean(ckv * ckv, axis=-1, keepdims=True) + EPS) * gkv_ref[...]).astype(BF16)
    q = jnp.dot(cqn, wq_ref[...], preferred_element_type=F32)
    qs = jnp.dot(cqn, wqs_ref[...], preferred_element_type=F32)
    k = jnp.dot(ckn, wkc_ref[...], preferred_element_type=F32) + jnp.dot(kr, wkr_ref[...], preferred_element_type=F32)
    ks = jnp.dot(ckn, wkcs_ref[...], preferred_element_type=F32) + jnp.dot(kr, wkrs_ref[...], preferred_element_type=F32)
    vt_ref[...] = lax.dot_general(wvt_ref[...], ckn, _NT, preferred_element_type=F32).astype(vt_ref.dtype)
    cos = cos_ref[...]
    sin = sin_ref[...]
    q_scale = QK_DIM ** -0.5 * math.log2(math.e)
    for h in range(H_A):
        sl = slice(h * HEAD_SLOT, (h + 1) * HEAD_SLOT)
        qh = q[:, sl]
        rq = lax.rsqrt(jnp.sum(qh * qh, axis=-1, keepdims=True) * (1.0 / QK_DIM) + EPS)
        q_rot = (qh * gqn_ref[...] * cos + qs[:, sl] * gqns_ref[...] * sin) * (rq * q_scale)
        q_ref[:, sl] = q_rot.astype(q_ref.dtype)
        kh = k[:, sl]
        rk = lax.rsqrt(jnp.sum(kh * kh, axis=-1, keepdims=True) * (1.0 / QK_DIM) + EPS)
        k_rot = (kh * gkn_ref[...] * cos + ks[:, sl] * gkns_ref[...] * sin) * rk
        k_ref[:, sl] = k_rot.astype(k_ref.dtype)


def _mla_prep_call(z, t, wts, cos_t, sin_t):
    n = z.shape[0]
    tm = _pick_tile(t, 512)
    tpb = t // tm
    full = lambda a: pl.BlockSpec(a.shape, lambda i: (0,) * a.ndim)
    in_specs = [pl.BlockSpec((tm, MLA_IN_PAD), lambda i: (i, 0))] + [full(a) for a in wts] + [
        pl.BlockSpec((tm, HEAD_SLOT), lambda i: (i % tpb, 0)),
        pl.BlockSpec((tm, HEAD_SLOT), lambda i: (i % tpb, 0)),
    ]
    hw = H_A * HEAD_SLOT
    vw = H_A * V_DIM
    return pl.pallas_call(
        _mla_prep_kernel,
        grid=(n // tm,),
        in_specs=in_specs,
        out_specs=[
            pl.BlockSpec((tm, hw), lambda i: (i, 0)),
            pl.BlockSpec((tm, hw), lambda i: (i, 0)),
            pl.BlockSpec((vw, tm), lambda i: (0, i)),
        ],
        out_shape=[
            jax.ShapeDtypeStruct((n, hw), BF16),
            jax.ShapeDtypeStruct((n, hw), BF16),
            jax.ShapeDtypeStruct((vw, n), BF16),
        ],
        compiler_params=_params(1),
        name="mla_prep",
    )(z, *wts, cos_t, sin_t)


def _mla_weights(q_norm_g, w_uq, kv_norm_g, w_ukv, qn_g, kn_g):
    pad = HEAD_SLOT - QK_DIM
    half = ROPE_DIM // 2
    perm = jnp.arange(HEAD_SLOT)
    perm = perm.at[NOPE_DIM:NOPE_DIM + half].set(jnp.arange(NOPE_DIM + half, NOPE_DIM + ROPE_DIM))
    perm = perm.at[NOPE_DIM + half:NOPE_DIM + ROPE_DIM].set(jnp.arange(NOPE_DIM, NOPE_DIM + half))

    def slots(w):
        wp = jnp.pad(w, ((0, 0), (0, 0), (0, pad)))
        return wp, wp[:, :, perm]

    def flat(w):
        return w.reshape(w.shape[0], H_A * HEAD_SLOT).astype(BF16)

    wq, wqs = slots(w_uq.reshape(Q_LORA, H_A, QK_DIM))
    w_kv = w_ukv.reshape(KV_LORA, H_A, NOPE_DIM + V_DIM)
    wkc, wkcs = slots(jnp.pad(w_kv[:, :, :NOPE_DIM], ((0, 0), (0, 0), (0, ROPE_DIM))))
    eye = jnp.zeros((HEAD_SLOT, QK_DIM), F32).at[jnp.arange(ROPE_DIM), NOPE_DIM + jnp.arange(ROPE_DIM)].set(1.0)
    wkr, wkrs = slots(jnp.broadcast_to(eye[:, None, :], (HEAD_SLOT, H_A, QK_DIM)))
    wvt = w_kv[:, :, NOPE_DIM:].reshape(KV_LORA, H_A * V_DIM).T.astype(BF16)

    def gains(g):
        gp = jnp.pad(g.astype(F32), (0, pad))
        return gp.reshape(1, HEAD_SLOT), gp[perm].reshape(1, HEAD_SLOT)

    gqn, gqns = gains(qn_g)
    gkn, gkns = gains(kn_g)
    return [q_norm_g.reshape(1, Q_LORA).astype(F32), kv_norm_g.reshape(1, KV_LORA).astype(F32),
            flat(wq), flat(wqs), flat(wkc), flat(wkcs), flat(wkr), flat(wkrs), wvt, gqn, gqns, gkn, gkns]


def _mla_tables(n_tok, rotary):
    cos = jnp.ones((n_tok, HEAD_SLOT), F32)
    sin = jnp.zeros((n_tok, HEAD_SLOT), F32)
    if rotary:
        rows = n_tok // GRID_W
        r = jnp.repeat(jnp.arange(rows, dtype=F32), GRID_W)
        col = jnp.tile(jnp.arange(GRID_W, dtype=F32), rows)
        nf = ROPE_DIM // 4
        f = ROPE_BASE ** (-jnp.arange(nf, dtype=F32) / nf)
        ang = jnp.concatenate([r[:, None] * f, col[:, None] * f], axis=-1)
        c, s = jnp.cos(ang), jnp.sin(ang)
        half = ROPE_DIM // 2
        cos = cos.at[:, NOPE_DIM:NOPE_DIM + half].set(c).at[:, NOPE_DIM + half:NOPE_DIM + ROPE_DIM].set(c)
        sin = sin.at[:, NOPE_DIM:NOPE_DIM + half].set(-s).at[:, NOPE_DIM + half:NOPE_DIM + ROPE_DIM].set(s)
    return cos, sin


def _attn_kernel(*refs, seg_lens):
    nseg = len(seg_lens)
    q_ref = refs[0]
    k_refs = refs[1:1 + nseg]
    vt_refs = refs[1 + nseg:1 + 2 * nseg]
    o_ref = refs[1 + 2 * nseg]
    s_s = refs[2 + 2 * nseg]
    chunks = [(g, r) for g, n in enumerate(seg_lens) for r in range(0, n, KEY_CHUNK)]
    tq = q_ref.shape[0]

    def fold(x, op):
        n = x.shape[0] // 8
        x = x.reshape(n, 8, tq)
        out = x[0]
        for i in range(1, n):
            out = op(out, x[i])
        return out

    def scores(h):
        sl = slice(h * HEAD_SLOT, (h + 1) * HEAD_SLOT)
        qh = q_ref[:, sl]
        mx = None
        for c, (g, r) in enumerate(chunks):
            s = lax.dot_general(k_refs[g][r:r + KEY_CHUNK, sl], qh, _NT, preferred_element_type=F32)
            s_s[h % 2, c] = s
            cm = fold(s, jnp.maximum)
            mx = cm if mx is None else jnp.maximum(mx, cm)
        return jnp.max(mx, axis=0, keepdims=True)

    def weighted_values(h, m):
        rows = slice(h * V_DIM, (h + 1) * V_DIM)
        lsum = None
        o_t = None
        for c, (g, r) in enumerate(chunks):
            pieces = []
            for r0 in range(0, KEY_CHUNK, SUB_ROWS):
                p = jnp.exp2(s_s[h % 2, c, r0:r0 + SUB_ROWS, :] - m)
                ps = fold(p, jnp.add)
                lsum = ps if lsum is None else lsum + ps
                pieces.append(p.astype(BF16))
            pb = jnp.concatenate(pieces, axis=0)
            t = jnp.dot(vt_refs[g][rows, r:r + KEY_CHUNK], pb, preferred_element_type=F32)
            o_t = t if o_t is None else o_t + t
        return o_t * (1.0 / jnp.sum(lsum, axis=0, keepdims=True))

    outs = []
    m_next = scores(0)
    for h in range(H_A):
        m_cur = m_next
        if h + 1 < H_A:
            m_next = scores(h + 1)
        outs.append(weighted_values(h, m_cur))
        if h % 2 == 1:
            j = h // 2
            o_ref[:, j * LANES:(j + 1) * LANES] = jnp.concatenate(outs[-2:], axis=0).T.astype(o_ref.dtype)


def _attn_call(q, ks, vts, t, tks):
    n = q.shape[0]
    tq = _pick_tile(t, 256)
    tpb = t // tq
    hw = H_A * HEAD_SLOT
    vw = H_A * V_DIM
    n_chunks = sum(tk // KEY_CHUNK for tk in tks)
    in_specs = [pl.BlockSpec((tq, hw), lambda i: (i, 0))]
    in_specs += [pl.BlockSpec((tk, hw), lambda i: (i // tpb, 0)) for tk in tks]
    in_specs += [pl.BlockSpec((vw, tk), lambda i: (0, i // tpb)) for tk in tks]
    return pl.pallas_call(
        functools.partial(_attn_kernel, seg_lens=tuple(tks)),
        grid=(n // tq,),
        in_specs=in_specs,
        out_specs=pl.BlockSpec((tq, vw), lambda i: (i, 0)),
        out_shape=jax.ShapeDtypeStruct((n, vw), BF16),
        scratch_shapes=[pltpu.VMEM((2, n_chunks, KEY_CHUNK, tq), F32)],
        compiler_params=_params(1),
        name="attention_%dseg" % len(tks),
    )(q, *ks, *vts)


def _rotate_pairs(x, cos, sin_lo, sin_hi):
    return x * cos + pltpu.roll(x, LANES - DK_R // 2, 1) * sin_lo + pltpu.roll(x, DK_R // 2, 1) * sin_hi


def _ret_kernel(lg_ref, qf_ref, kf_ref, vf_ref, cf_ref, slf_ref, shf_ref,
                qb_ref, kb_ref, vb_ref, cb_ref, slb_ref, shb_ref, s0f_ref, s0b_ref,
                of_ref, ob_ref, sff_ref, sfb_ref, df_s, db_s, sf_s, sb_s, *, chunk):
    b = pl.program_id(0)
    c = pl.program_id(1)
    nc = pl.num_programs(1)

    @pl.when(jnp.logical_and(b == 0, c == 0))
    def _():
        ii = lax.broadcasted_iota(jnp.int32, (chunk, chunk), 0)
        jj = lax.broadcasted_iota(jnp.int32, (chunk, chunk), 1)
        diff = (ii - jj).astype(F32)
        for h in range(H_R):
            df_s[h] = jnp.where(ii >= jj, jnp.exp(jnp.where(ii >= jj, diff, 0.0) * lg_ref[0, h]), 0.0)
            db_s[h] = jnp.where(jj > ii, jnp.exp(jnp.where(jj > ii, -diff, 0.0) * lg_ref[1, h]), 0.0)

    @pl.when(c == 0)
    def _():
        sf_s[...] = s0f_ref[0]
        sb_s[...] = s0b_ref[0]

    lane = lax.broadcasted_iota(jnp.int32, (1, LANES), 1)
    lo = lane < DK_R
    row = lax.broadcasted_iota(jnp.int32, (LANES, LANES), 0)
    colm = lax.broadcasted_iota(jnp.int32, (LANES, LANES), 1)
    blockdiag = (row < DK_R) == (colm < DK_R)
    pos = lax.broadcasted_iota(jnp.int32, (chunk, 1), 0).astype(F32)
    dirs = ((qf_ref, kf_ref, vf_ref, cf_ref, slf_ref, shf_ref, df_s, sf_s, of_ref),
            (qb_ref, kb_ref, vb_ref, cb_ref, slb_ref, shb_ref, db_s, sb_s, ob_ref))
    units = [(d, j) for d in range(2) for j in range(N_PAIR)]

    ops = {}
    for d, j in units:
        q_ref, k_ref, v_ref, cos_ref, sl_ref, sh_ref = dirs[d][:6]
        sl = slice(j * LANES, (j + 1) * LANES)
        lg = jnp.where(lo, lg_ref[d, 2 * j], lg_ref[d, 2 * j + 1])
        q2 = _rotate_pairs(q_ref[:, sl].astype(F32), cos_ref[...], sl_ref[...], sh_ref[...])
        k2 = _rotate_pairs(k_ref[:, sl].astype(F32), cos_ref[...], sl_ref[...], sh_ref[...]) * (DK_R ** -0.5)
        if d == 0:
            q_dec = jnp.exp((pos + 1.0) * lg)
            k_dec = jnp.exp((chunk - 1.0 - pos) * lg)
        else:
            q_dec = jnp.exp((chunk - pos) * lg)
            k_dec = jnp.exp(pos * lg)
        ops[d, j] = dict(
            sl=sl, v2=v_ref[:, sl], c_dec=jnp.exp(chunk * lg), k2b=k2.astype(BF16),
            q_dec=(q2 * q_dec).astype(BF16), kd_t=(k2 * k_dec).T.astype(BF16),
            q_lo=jnp.where(lo, q2, 0.0).astype(BF16), q_hi=jnp.where(lo, 0.0, q2).astype(BF16))
    for d, j in units:
        u = ops[d, j]
        u['s2'] = dirs[d][7][j]
        u['o'] = jnp.dot(u['q_dec'], u['s2'].astype(BF16), preferred_element_type=F32)
        u['sc'] = [lax.dot_general(u[name], u['k2b'], _NT, preferred_element_type=F32) for name in ('q_lo', 'q_hi')]
    for d, j in units:
        u = ops[d, j]
        d_s, o_ref = dirs[d][6], dirs[d][8]
        o = u['o']
        for e in range(2):
            ve = jnp.where(lo if e == 0 else jnp.logical_not(lo), u['v2'], jnp.zeros((), BF16))
            o = o + jnp.dot((u['sc'][e] * d_s[2 * j + e]).astype(BF16), ve, preferred_element_type=F32)
        o_ref[:, u['sl']] = o.astype(o_ref.dtype)
    for d, j in units:
        u = ops[d, j]
        upd = jnp.dot(u['kd_t'], u['v2'], preferred_element_type=F32)
        dirs[d][7][j] = u['s2'] * u['c_dec'] + jnp.where(blockdiag, upd, 0.0)

    @pl.when(c == nc - 1)
    def _():
        sff_ref[0] = sf_s[...]
        sfb_ref[0] = sb_s[...]


def _ret_call(z, lg, tables, s0f, s0b, b, t):
    n = z.shape[0]
    chunk = _pick_tile(t, 256)
    nc = t // chunk
    cos_t, sin_lo_t, sin_hi_t = tables
    col0 = MLA_IN_PAD // RET_W

    def zspec(part, rev):
        if rev:
            return pl.BlockSpec((chunk, RET_W), lambda bi, ci: (bi * nc + nc - 1 - ci, col0 + part))
        return pl.BlockSpec((chunk, RET_W), lambda bi, ci: (bi * nc + ci, col0 + part))

    def tspec(rev):
        if rev:
            return pl.BlockSpec((chunk, LANES), lambda bi, ci: (nc - 1 - ci, 0))
        return pl.BlockSpec((chunk, LANES), lambda bi, ci: (ci, 0))

    st_spec = pl.BlockSpec((1, N_PAIR, LANES, LANES), lambda bi, ci: (bi, 0, 0, 0))
    in_specs = [pl.BlockSpec(memory_space=pltpu.SMEM)]
    in_specs += [zspec(0, False), zspec(1, False), zspec(2, False), tspec(False), tspec(False), tspec(False)]
    in_specs += [zspec(0, True), zspec(1, True), zspec(2, True), tspec(True), tspec(True), tspec(True)]
    in_specs += [st_spec, st_spec]
    out_specs = [
        pl.BlockSpec((chunk, RET_W), lambda bi, ci: (bi * nc + ci, 0)),
        pl.BlockSpec((chunk, RET_W), lambda bi, ci: (bi * nc + nc - 1 - ci, 0)),
        st_spec, st_spec,
    ]
    st_shape = jax.ShapeDtypeStruct((b, N_PAIR, LANES, LANES), F32)
    return pl.pallas_call(
        functools.partial(_ret_kernel, chunk=chunk),
        grid=(b, nc),
        in_specs=in_specs,
        out_specs=out_specs,
        out_shape=[jax.ShapeDtypeStruct((n, RET_W), BF16), jax.ShapeDtypeStruct((n, RET_W), BF16),
                   st_shape, st_shape],
        scratch_shapes=[
            pltpu.VMEM((H_R, chunk, chunk), F32), pltpu.VMEM((H_R, chunk, chunk), F32),
            pltpu.VMEM((N_PAIR, LANES, LANES), F32), pltpu.VMEM((N_PAIR, LANES, LANES), F32),
        ],
        compiler_params=_params(2),
        name="retention",
    )(lg, z, z, z, cos_t, sin_lo_t, sin_hi_t, z, z, z, cos_t, sin_lo_t, sin_hi_t, s0f, s0b)


def _ret_tables(n_tok, rotary):
    cos = jnp.ones((n_tok, LANES), F32)
    sin_lo = jnp.zeros((n_tok, LANES), F32)
    sin_hi = jnp.zeros((n_tok, LANES), F32)
    if rotary:
        nf = DK_R // 2
        theta = ROPE_BASE ** (-jnp.arange(nf, dtype=F32) / nf)
        ang = jnp.arange(n_tok, dtype=F32)[:, None] * theta
        c, s = jnp.cos(ang), jnp.sin(ang)
        z = jnp.zeros_like(s)
        cos = jnp.concatenate([c, c, c, c], axis=-1)
        sin_lo = jnp.concatenate([-s, z, -s, z], axis=-1)
        sin_hi = jnp.concatenate([z, s, z, s], axis=-1)
    return cos, sin_lo, sin_hi


def _mix_out_kernel(x_ref, a_ref, of_ref, ob_ref, g_ref, w_ref, mod_ref, o_ref):
    lane = lax.broadcasted_iota(jnp.int32, (1, LANES), 1)
    lo = lane < DV_R
    y = jnp.dot(a_ref[...], w_ref[0:H_A * V_DIM, :], preferred_element_type=F32)
    for j in range(N_PAIR):
        sl = slice(j * LANES, (j + 1) * LANES)
        o = of_ref[:, sl].astype(F32) + ob_ref[:, sl].astype(F32)
        s_lo = jnp.sum(jnp.where(lo, o, 0.0), axis=-1, keepdims=True)
        s_all = jnp.sum(o, axis=-1, keepdims=True)
        mu = jnp.where(lo, s_lo, s_all - s_lo) * (1.0 / DV_R)
        oc = o - mu
        q = oc * oc
        q_lo = jnp.sum(jnp.where(lo, q, 0.0), axis=-1, keepdims=True)
        q_all = jnp.sum(q, axis=-1, keepdims=True)
        var = jnp.where(lo, q_lo, q_all - q_lo) * (1.0 / DV_R)
        g = g_ref[:, sl].astype(F32)
        r = (oc * lax.rsqrt(var + EPS)) * (g * _sigmoid(g))
        row0 = H_A * V_DIM + j * LANES
        y = y + jnp.dot(r.astype(BF16), w_ref[row0:row0 + LANES, :], preferred_element_type=F32)
    o_ref[...] = x_ref[...] + mod_ref[0, 2:3, :] * y


def _mix_out_call(x2, a, o_f, o_b, z, w_out, mod, t):
    n = x2.shape[0]
    tm = _pick_tile(t, 512)
    tpb = t // tm
    gate_col = MLA_IN_PAD // RET_W + 3
    return pl.pallas_call(
        _mix_out_kernel,
        grid=(n // tm,),
        in_specs=[
            pl.BlockSpec((tm, D_MODEL), lambda i: (i, 0)),
            pl.BlockSpec((tm, H_A * V_DIM), lambda i: (i, 0)),
            pl.BlockSpec((tm, RET_W), lambda i: (i, 0)),
            pl.BlockSpec((tm, RET_W), lambda i: (i, 0)),
            pl.BlockSpec((tm, RET_W), lambda i: (i, gate_col)),
            pl.BlockSpec(w_out.shape, lambda i: (0, 0)),
            pl.BlockSpec((1, ADA_CHUNKS, D_MODEL), lambda i: (i // tpb, 0, 0)),
        ],
        out_specs=pl.BlockSpec((tm, D_MODEL), lambda i: (i, 0)),
        out_shape=jax.ShapeDtypeStruct((n, D_MODEL), F32),
        compiler_params=_params(1),
        name="mix_out",
    )(x2, a, o_f, o_b, z, w_out, mod)


def _mlp_kernel(x_ref, g_ref, mod_ref, w1_ref, w2_ref, o_ref):
    h = _norm_mod(x_ref[...], g_ref, mod_ref, 3, 4).astype(BF16)
    parts = []
    for k in range(0, D_FF, MLP_FF_CHUNK):
        a = jnp.maximum(jnp.dot(h, w1_ref[:, k:k + MLP_FF_CHUNK], preferred_element_type=F32), 0.0)
        parts.append((a * a).astype(BF16))
    y = jnp.dot(jnp.concatenate(parts, axis=1), w2_ref[...], preferred_element_type=F32)
    o_ref[...] = x_ref[...] + mod_ref[0, 5:6, :] * y


def _mlp_call(x2, g, mod, w1, w2, t):
    n = x2.shape[0]
    tm = _pick_tile(t, 512)
    tpb = t // tm
    once = pl.Buffered(1)
    return pl.pallas_call(
        _mlp_kernel,
        grid=(n // tm,),
        in_specs=[
            pl.BlockSpec((tm, D_MODEL), lambda i: (i, 0)),
            pl.BlockSpec((1, D_MODEL), lambda i: (0, 0)),
            pl.BlockSpec((1, ADA_CHUNKS, D_MODEL), lambda i: (i // tpb, 0, 0)),
            pl.BlockSpec((D_MODEL, D_FF), lambda i: (0, 0), pipeline_mode=once),
            pl.BlockSpec((D_FF, D_MODEL), lambda i: (0, 0), pipeline_mode=once),
        ],
        out_specs=pl.BlockSpec((tm, D_MODEL), lambda i: (i, 0)),
        out_shape=jax.ShapeDtypeStruct((n, D_MODEL), F32),
        compiler_params=_params(1),
        name="mlp",
    )(x2, g.reshape(1, D_MODEL), mod, w1, w2)


def _s5_compact_weights(p_f, p_b):
    st = lambda i: jnp.stack([p_f[i].astype(F32), p_b[i].astype(F32)])
    ar, ai, br, bi, cr, ci, log_dt = (st(i) for i in range(7))
    dt = jnp.exp(log_dt)[:, :, None]
    lr, li = dt * ar, dt * ai
    steps = np.arange(S5_L)

    def cexp(tau, x_r, x_i):
        mag = jnp.exp(tau * x_r)
        return mag * jnp.cos(tau * x_i), mag * jnp.sin(tau * x_i)

    def taus(table):
        return jnp.asarray(np.asarray(table, np.float32))[:, :, None, None, None]

    a1r, a1i = cexp(1.0, lr, li)
    nr, ni = a1r - 1.0, a1i
    den = ar * ar + ai * ai
    qr, qi = (nr * ar + ni * ai) / den, (ni * ar - nr * ai) / den
    bbr = qr[..., None] * br - qi[..., None] * bi
    bbi = qr[..., None] * bi + qi[..., None] * br

    slab = lambda x: x.reshape((2, N_SLAB, S5_GPS) + x.shape[2:])
    over_h = lambda x: jnp.repeat(x, S5_GROUP, axis=-1)
    lr_fp, li_fp = lr.reshape(2, N_SLAB, 1, S5_HALF), li.reshape(2, N_SLAB, 1, S5_HALF)
    lr_fh, li_fh = (over_h(slab(x).transpose(0, 1, 3, 2)) for x in (lr, li))
    cr_fh, ci_fh = (slab(x).transpose(0, 1, 4, 2, 3).reshape(2, N_SLAB, S5_STATE, LANES) for x in (cr, ci))
    bbr_fp, bbi_fp = (slab(x).transpose(0, 1, 4, 2, 3).reshape(2, N_SLAB, S5_GROUP, S5_HALF) for x in (bbr, bbi))
    bbr_fh, bbi_fh = (over_h(slab(x).transpose(0, 1, 4, 3, 2)) for x in (bbr, bbi))

    def c_pow(table):
        pr, pi = cexp(taus(table), lr_fh[:, None], li_fh[:, None])
        return cr_fh[:, None] * pr - ci_fh[:, None] * pi, cr_fh[:, None] * pi + ci_fh[:, None] * pr

    dr, di = c_pow([steps, steps])
    kern = jnp.sum(dr[:, :, :, None] * bbr_fh[:, None] - di[:, :, :, None] * bbi_fh[:, None], axis=4)
    lag_f = steps[None, :] - steps[:, None]
    toep = jnp.stack([kern[0][np.clip(lag_f, 0, S5_L - 1)], kern[1][np.clip(-lag_f, 0, S5_L - 1)]])
    valid = np.stack([lag_f >= 0, lag_f <= 0])[:, :, :, None, None, None]
    tt = jnp.where(valid, toep, 0.0).transpose(0, 3, 1, 4, 2, 5).reshape(2, N_SLAB, S5_L * S5_GROUP, S5_K)
    pr, pi = cexp(taus([S5_L - 1 - steps, steps]), lr_fp[:, None], li_fp[:, None])
    wr = pr * bbr_fp[:, None] - pi * bbi_fp[:, None]
    wi = pr * bbi_fp[:, None] + pi * bbr_fp[:, None]
    wt = jnp.stack([wr, wi], axis=4).transpose(0, 2, 1, 3, 4, 5).reshape(2, N_SLAB, S5_L * S5_GROUP, 2 * S5_HALF)
    vr, vi = c_pow([steps + 1, S5_L - steps])
    vt = jnp.stack([vr, -vi], axis=1).transpose(0, 3, 1, 4, 2, 5).reshape(2, N_SLAB, 2 * S5_STATE, S5_K)
    alr, ali = cexp(float(S5_L), lr, li)
    al = jnp.stack([alr[0], ali[0], alr[1], ali[1]]).reshape(4, N_SLAB, S5_HALF).transpose(1, 0, 2)
    coef = jnp.concatenate([al, jnp.zeros_like(al)], axis=1)
    return tt, wt, vt, coef


def _nm_s5_kernel(x_ref, g_ref, mod_ref, h_ref, u_ref, hs_s, *, nb, tk):
    x = x_ref[...]
    ms = jnp.mean(x * x, axis=-1, keepdims=True)
    y = x * lax.rsqrt(ms + EPS) * g_ref[...]
    h = y * (1.0 + mod_ref[:, 1:2, :]) + mod_ref[:, 0:1, :]
    h_ref[...] = h.astype(h_ref.dtype)
    pitch = tk + S5_PITCH_PAD
    for bi in range(nb):
        for j in range(N_SLAB):
            hs_s[j, bi * pitch:bi * pitch + tk, :] = h[bi, :, j * LANES:(j + 1) * LANES]
    for kk in range(tk // S5_L):
        for s in range(S5_L):
            for j in range(N_SLAB):
                blk = hs_s[j, pl.ds(kk * S5_L + s, nb, stride=pitch), :]
                u_ref[j, kk * nb:(kk + 1) * nb, s * LANES:(s + 1) * LANES] = blk.astype(u_ref.dtype)


def _norm_mod_s5_call(x3, g, mod):
    b, t, _ = x3.shape
    tk = S5_TOK
    rows = (tk // S5_L) * b
    return pl.pallas_call(
        functools.partial(_nm_s5_kernel, nb=b, tk=tk),
        grid=(t // tk,),
        in_specs=[
            pl.BlockSpec((b, tk, D_MODEL), lambda i: (0, i, 0)),
            pl.BlockSpec((1, D_MODEL), lambda i: (0, 0)),
            pl.BlockSpec((b, ADA_CHUNKS, D_MODEL), lambda i: (0, 0, 0)),
        ],
        out_specs=[
            pl.BlockSpec((b, tk, D_MODEL), lambda i: (0, i, 0)),
            pl.BlockSpec((N_SLAB, rows, S5_K), lambda i: (0, i, 0)),
        ],
        out_shape=[
            jax.ShapeDtypeStruct((b, t, D_MODEL), BF16),
            jax.ShapeDtypeStruct((N_SLAB, (t // S5_L) * b, S5_K), BF16),
        ],
        scratch_shapes=[pltpu.VMEM((N_SLAB, b * (tk + S5_PITCH_PAD), LANES), F32)],
        compiler_params=_params(1),
        name="norm_mod_s5",
    )(x3, g.reshape(1, D_MODEL), mod)


def _s5_kernel(uc_ref, ulf_ref, ulb_ref, tt_ref, wt_ref, vt_ref, coef_ref,
               ycf_ref, ycb_ref, ylf_ref, ylb_ref, t_s, w_s, v_s, u_s, s_s, x_s, carry_s, *, nb, nblk):
    i = pl.program_id(1)

    @pl.when(i == 0)
    def _():
        lane = lax.broadcasted_iota(jnp.int32, (1, S5_K), 1)
        grp_out = (lane // S5_GROUP) % S5_GPS
        grp_state = (lane // S5_STATE) % S5_GPS
        rows = S5_GROUP
        for d in range(2):
            for s in range(S5_L):
                tc = tt_ref[d, 0, s * rows:(s + 1) * rows, :]
                wc = wt_ref[d, 0, s * rows:(s + 1) * rows, :]
                for gi in range(S5_GPS):
                    r0 = s * LANES + gi * rows
                    t_s[d, r0:r0 + rows, :] = jnp.where(grp_out == gi, tc, 0.0).astype(BF16)
                    w_s[d, r0:r0 + rows, :] = jnp.where(grp_state == gi, wc, 0.0).astype(BF16)
            for c in range(2):
                vc = vt_ref[d, 0, c * S5_STATE:(c + 1) * S5_STATE, :]
                for gi in range(S5_GPS):
                    r0 = c * S5_HALF + gi * S5_STATE
                    v_s[d, r0:r0 + S5_STATE, :] = jnp.where(grp_out == gi, vc, 0.0).astype(BF16)
        carry_s[...] = jnp.zeros_like(carry_s)
        u_s[0] = uc_ref[0]
        u_s[1] = uc_ref[0]

    @pl.when(i > 0)
    def _():
        u_s[0] = ulf_ref[0]
        u_s[1] = ulb_ref[0]

    for d in range(2):
        s_s[d] = jnp.dot(u_s[d], w_s[d], preferred_element_type=F32)
    ys = [jnp.dot(u_s[d], t_s[d], preferred_element_type=F32) for d in range(2)]
    for d in range(2):
        a_r = coef_ref[0, 2 * d:2 * d + 1, :]
        a_i = coef_ref[0, 2 * d + 1:2 * d + 2, :]
        x_r, x_i = carry_s[d, :, 0:S5_HALF], carry_s[d, :, S5_HALF:2 * S5_HALF]
        for k in range(nblk):
            rows = slice((k if d == 0 else nblk - 1 - k) * nb, (k if d == 0 else nblk - 1 - k) * nb + nb)
            x_s[d, rows, 0:S5_HALF] = x_r
            x_s[d, rows, S5_HALF:2 * S5_HALF] = x_i
            s_r = s_s[d, rows, 0:S5_HALF]
            s_i = s_s[d, rows, S5_HALF:2 * S5_HALF]
            x_r, x_i = a_r * x_r - a_i * x_i + s_r, a_r * x_i + a_i * x_r + s_i
        carry_s[d, :, 0:S5_HALF] = x_r
        carry_s[d, :, S5_HALF:2 * S5_HALF] = x_i
    for d in range(2):
        y = ys[d] + jnp.dot(x_s[d].astype(BF16), v_s[d], preferred_element_type=F32)
        ys[d] = y.astype(BF16)

    @pl.when(i == 0)
    def _():
        ycf_ref[0] = ys[0]
        ycb_ref[0] = ys[1]

    @pl.when(i > 0)
    def _():
        ylf_ref[0] = ys[0]
        ylb_ref[0] = ys[1]


def _s5_call(u_c, u_l, tt, wt, vt, coef, nb):
    rows = u_c.shape[1]
    n_lat = u_l.shape[1] // rows
    nblk = rows // nb
    tile = lambda fn: pl.BlockSpec((1, rows, S5_K), fn)
    wspec = lambda a: pl.BlockSpec((2, 1) + a.shape[2:], lambda j, i: (0, j, 0, 0))
    ctx_map = lambda j, i: (j, 0, 0)
    fwd_map = lambda j, i: (j, jnp.maximum(i - 1, 0), 0)
    bwd_map = lambda j, i: (j, jnp.minimum(n_lat - i, n_lat - 1), 0)
    yc = jax.ShapeDtypeStruct(u_c.shape, BF16)
    yl = jax.ShapeDtypeStruct(u_l.shape, BF16)
    return pl.pallas_call(
        functools.partial(_s5_kernel, nb=nb, nblk=nblk),
        grid=(N_SLAB, n_lat + 1),
        in_specs=[tile(ctx_map), tile(fwd_map), tile(bwd_map), wspec(tt), wspec(wt), wspec(vt),
                  pl.BlockSpec((1,) + coef.shape[1:], lambda j, i: (j, 0, 0))],
        out_specs=[tile(ctx_map), tile(ctx_map), tile(fwd_map), tile(bwd_map)],
        out_shape=[yc, yc, yl, yl],
        scratch_shapes=[
            pltpu.VMEM((2, S5_K, S5_K), BF16),
            pltpu.VMEM((2, S5_K, 2 * S5_HALF), BF16),
            pltpu.VMEM((2, 2 * S5_HALF, S5_K), BF16),
            pltpu.VMEM((2, rows, S5_K), BF16),
            pltpu.VMEM((2, rows, 2 * S5_HALF), F32),
            pltpu.VMEM((2, rows, 2 * S5_HALF), F32),
            pltpu.VMEM((2, nb, 2 * S5_HALF), F32),
        ],
        compiler_params=_params(2),
        name="s5_scan",
    )(u_c, u_l, u_l, tt, wt, vt, coef)


def _s5_out_kernel(x_ref, yf_ref, yb_ref, h_ref, d_ref, w_ref, mod_ref, o_ref, ys_s, *, nb, tk):
    pitch = tk + S5_PITCH_PAD
    for kk in range(tk // S5_L):
        rows = slice(kk * nb, (kk + 1) * nb)
        for s in range(S5_L):
            lanes = slice(s * LANES, (s + 1) * LANES)
            for j in range(N_SLAB):
                blk = yf_ref[j, rows, lanes].astype(F32) + yb_ref[j, rows, lanes].astype(F32)
                ys_s[j, pl.ds(kk * S5_L + s, nb, stride=pitch), :] = blk
    y = jnp.concatenate(
        [jnp.concatenate([ys_s[j, bi * pitch:bi * pitch + tk, :] for bi in range(nb)], axis=0) for j in range(N_SLAB)],
        axis=1)
    y = y + d_ref[...] * h_ref[...].reshape(nb * tk, D_MODEL).astype(F32)
    g = 0.5 * y * (1.0 + jnp.tanh(math.sqrt(2.0 / math.pi) * (y + 0.044715 * (y * y * y))))
    z = jnp.dot(g.astype(BF16), w_ref[...], preferred_element_type=F32)
    out = z[:, :D_MODEL] * _sigmoid(z[:, D_MODEL:])
    o_ref[...] = x_ref[...] + mod_ref[:, 2:3, :] * out.reshape(nb, tk, D_MODEL)


def _s5_out_call(x3, y_f, y_b, h3, d_skip, w_glu, mod):
    b, t, _ = x3.shape
    tk = S5_TOK
    rows = (tk // S5_L) * b
    tile = pl.BlockSpec((b, tk, D_MODEL), lambda i: (0, i, 0))
    ytile = pl.BlockSpec((N_SLAB, rows, S5_K), lambda i: (0, i, 0))
    return pl.pallas_call(
        functools.partial(_s5_out_kernel, nb=b, tk=tk),
        grid=(t // tk,),
        in_specs=[
            tile, ytile, ytile, tile,
            pl.BlockSpec((1, D_MODEL), lambda i: (0, 0)),
            pl.BlockSpec(w_glu.shape, lambda i: (0, 0)),
            pl.BlockSpec((b, ADA_CHUNKS, D_MODEL), lambda i: (0, 0, 0)),
        ],
        out_specs=tile,
        out_shape=jax.ShapeDtypeStruct((b, t, D_MODEL), F32),
        scratch_shapes=[pltpu.VMEM((N_SLAB, b * (tk + S5_PITCH_PAD), LANES), F32)],
        compiler_params=_params(1),
        name="s5_out",
    )(x3, y_f, y_b, h3, d_skip.reshape(1, D_MODEL).astype(F32), w_glu, mod)


def _even_layer(x2, xc2, mod_l, mod_c, b, t, tc, need_ctx, p):
    n_c = b * tc
    z_c = _norm_mod_matmul_call(xc2, p['norm1_g'], mod_c, n_c, p['w_in'])
    z_l = _norm_mod_matmul_call(x2, p['norm1_g'], mod_l, t, p['w_in'])
    q_c, k_c, vt_c = _mla_prep_call(z_c, n_c, p['mla'], *_mla_tables(n_c, False))
    q_l, k_l, vt_l = _mla_prep_call(z_l, t, p['mla'], *_mla_tables(t, True))
    a_l = _attn_call(q_l, [k_c, k_l], [vt_c, vt_l], t, [tc, t])
    zero = jnp.zeros((b, N_PAIR, LANES, LANES), F32)
    of_c, ob_c, s_cf, s_cb = _ret_call(z_c, p['lg'], _ret_tables(tc, False), zero, zero, b, tc)
    of_l, ob_l, _, _ = _ret_call(z_l, p['lg'], _ret_tables(t, True), s_cf, s_cb, b, t)
    x2 = _mix_out_call(x2, a_l, of_l, ob_l, z_l, p['w_out'], mod_l, t)
    if need_ctx:
        a_c = _attn_call(q_c, [k_c], [vt_c], tc, [tc])
        xc2 = _mix_out_call(xc2, a_c, of_c, ob_c, z_c, p['w_out'], mod_c, n_c)
    return x2, xc2


def _odd_layer(x2, xc2, mod_l, mod_c, b, t, tc, need_ctx, p):
    x3, xc3 = x2.reshape(b, t, D_MODEL), xc2.reshape(b, tc, D_MODEL)
    h_c, u_c = _norm_mod_s5_call(xc3, p['norm1_g'], mod_c)
    h_l, u_l = _norm_mod_s5_call(x3, p['norm1_g'], mod_l)
    tt, wt, vt, coef = p['s5']
    ycf, ycb, ylf, ylb = _s5_call(u_c, u_l, tt, wt, vt, coef, b)
    x2 = _s5_out_call(x3, ylf, ylb, h_l, p['d_skip'], p['w_glu'], mod_l).reshape(b * t, D_MODEL)
    if need_ctx:
        xc2 = _s5_out_call(xc3, ycf, ycb, h_c, p['d_skip'], p['w_glu'], mod_c).reshape(b * tc, D_MODEL)
    return x2, xc2


def kernel(x, c, ctx, c_ctx, ada_w, ada_b, norm1_g, norm2_g, mlp_w1, mlp_w2, w_in, mla_q_norm_g, mla_w_uq, mla_kv_norm_g, mla_w_ukv, mla_qn_g, mla_kn_g, ret_lg_f, ret_lg_b, w_out, s5_a_re_f, s5_a_im_f, s5_b_re_f, s5_b_im_f, s5_c_re_f, s5_c_im_f, s5_log_dt_f, s5_a_re_b, s5_a_im_b, s5_b_re_b, s5_b_im_b, s5_c_re_b, s5_c_im_b, s5_log_dt_b, s5_d, s5_w_glu):
    b, t, _ = x.shape
    tc = ctx.shape[1]
    depth = ada_w.shape[0]
    assert b % 8 == 0 and tc % KEY_CHUNK == 0 and t % tc == 0 and tc % S5_TOK == 0
    rows = -(-(b + 1) // 8) * 8
    cc = jnp.zeros((rows, D_MODEL), F32).at[:b].set(c.astype(F32)).at[b].set(c_ctx.astype(F32))
    mod = _ada_all(cc, ada_w.astype(F32), ada_b.astype(F32))
    x2 = x.reshape(b * t, D_MODEL).astype(F32)
    xc2 = ctx.reshape(b * tc, D_MODEL).astype(F32)
    w1_all, w2_all = mlp_w1.astype(BF16), mlp_w2.astype(BF16)
    w_in_all = jnp.concatenate([w_in[:, :, :MLA_IN], jnp.zeros(w_in.shape[:2] + (MLA_IN_PAD - MLA_IN,), w_in.dtype),
                                w_in[:, :, MLA_IN:]], axis=2).astype(BF16)
    w_out_all, w_glu_all = w_out.astype(BF16), s5_w_glu.astype(BF16)
    mla_all = jax.vmap(_mla_weights)(mla_q_norm_g, mla_w_uq, mla_kv_norm_g, mla_w_ukv, mla_qn_g, mla_kn_g)
    lg_all = jnp.stack([jnp.log1p(-jnp.exp2(ret_lg_f.astype(F32))), jnp.log1p(-jnp.exp2(ret_lg_b.astype(F32)))], axis=1)
    s5_all = jax.vmap(_s5_compact_weights)(
        (s5_a_re_f, s5_a_im_f, s5_b_re_f, s5_b_im_f, s5_c_re_f, s5_c_im_f, s5_log_dt_f),
        (s5_a_re_b, s5_a_im_b, s5_b_re_b, s5_b_im_b, s5_c_re_b, s5_c_im_b, s5_log_dt_b))
    for l in range(depth):
        need_ctx = l < depth - 1
        mod_l = mod[l, :b].reshape(b, ADA_CHUNKS, D_MODEL)
        mod_c = jnp.broadcast_to(mod[l, b].reshape(1, ADA_CHUNKS, D_MODEL), (b, ADA_CHUNKS, D_MODEL))
        if l % 2 == 0:
            e = l // 2
            p = dict(norm1_g=norm1_g[l], w_in=w_in_all[e], mla=[a[e] for a in mla_all], lg=lg_all[e], w_out=w_out_all[e])
            x2, xc2 = _even_layer(x2, xc2, mod_l, mod_c, b, t, tc, need_ctx, p)
        else:
            o = l // 2
            p = dict(norm1_g=norm1_g[l], d_skip=s5_d[o], w_glu=w_glu_all[o], s5=[a[o] for a in s5_all])
            x2, xc2 = _odd_layer(x2, xc2, mod_l, mod_c, b, t, tc, need_ctx, p)
        x2 = _mlp_call(x2, norm2_g[l], mod_l, w1_all[l], w2_all[l], t)
        if need_ctx:
            xc2 = _mlp_call(xc2, norm2_g[l], mod_c, w1_all[l], w2_all[l], b * tc)
    return x2.reshape(b, t, D_MODEL).astype(x.dtype)
```

```python
import functools
import math

import jax
import jax.numpy as jnp
import numpy as np
from jax import lax
from jax.experimental import pallas as pl
from jax.experimental.pallas import tpu as pltpu

F32 = jnp.float32
BF16 = jnp.bfloat16

D_MODEL = 1024
EPS = 1e-6
ADA_CHUNKS = 6
GRID_W = 64
ROPE_BASE = 10000.0
LANES = 128
N_SLAB = D_MODEL // LANES

H_A = 8
Q_LORA = 256
KV_LORA = 128
NOPE_DIM = 64
ROPE_DIM = 32
QK_DIM = NOPE_DIM + ROPE_DIM
V_DIM = 64
HEAD_SLOT = LANES
MLA_IN = Q_LORA + KV_LORA + ROPE_DIM
MLA_IN_PAD = 512
KEY_CHUNK = 256
SUB_ROWS = 64

H_R = 8
DK_R = 64
DV_R = 64
RET_W = H_R * DK_R
N_PAIR = H_R // 2

S5_GROUP = 16
S5_GROUPS = D_MODEL // S5_GROUP
S5_STATE = 64
S5_L = 8
S5_K = S5_L * LANES
S5_GPS = LANES // S5_GROUP
S5_HALF = S5_GPS * S5_STATE
S5_TOK = 32
S5_PITCH_PAD = 8

D_FF = 4 * D_MODEL
MLP_FF_CHUNK = 512
Z_WIDTH = MLA_IN_PAD + 4 * RET_W

VMEM_LIMIT = 56 * 1024 * 1024

_NT = (((1,), (1,)), ((), ()))
_HP = lax.Precision.HIGHEST


def _params(n_grid):
    return pltpu.CompilerParams(
        dimension_semantics=("arbitrary",) * n_grid, vmem_limit_bytes=VMEM_LIMIT)


def _sigmoid(x):
    return 1.0 / (1.0 + jnp.exp(-x))


def _pick_tile(t, pref):
    tile = min(t, pref)
    while t % tile:
        tile //= 2
    return tile


def _ada_kernel(c_ref, w_ref, b_ref, o_ref):
    cc = c_ref[...]
    s = cc * _sigmoid(cc)
    o_ref[0] = jnp.dot(s, w_ref[0], preferred_element_type=F32, precision=_HP) + b_ref[0]


def _ada_all(cc, ada_w, ada_b):
    depth, _, width = ada_w.shape
    r = cc.shape[0]
    tn = 1536
    return pl.pallas_call(
        _ada_kernel,
        grid=(depth, width // tn),
        in_specs=[
            pl.BlockSpec((r, D_MODEL), lambda l, j: (0, 0)),
            pl.BlockSpec((1, D_MODEL, tn), lambda l, j: (l, 0, j)),
            pl.BlockSpec((1, 1, tn), lambda l, j: (l, 0, j)),
        ],
        out_specs=pl.BlockSpec((1, r, tn), lambda l, j: (l, 0, j)),
        out_shape=jax.ShapeDtypeStruct((depth, r, width), F32),
        compiler_params=_params(2),
        name="ada_mod",
    )(cc, ada_w, ada_b.reshape(depth, 1, width))


def _norm_mod(x, g_ref, mod_ref, shift_row, scale_row):
    ms = jnp.mean(x * x, axis=-1, keepdims=True)
    y = x * lax.rsqrt(ms + EPS) * g_ref[...]
    return y * (1.0 + mod_ref[0, scale_row:scale_row + 1, :]) + mod_ref[0, shift_row:shift_row + 1, :]


def _rotate_pairs(x, cos, sin_lo, sin_hi):
    return x * cos + pltpu.roll(x, LANES - DK_R // 2, 1) * sin_lo + pltpu.roll(x, DK_R // 2, 1) * sin_hi


def _nmm_kernel(x_ref, g_ref, mod_ref, w_ref, *rest, rotary):
    o_ref = rest[-1]
    h = _norm_mod(x_ref[...], g_ref, mod_ref, 0, 1).astype(BF16)
    z = jnp.dot(h, w_ref[...], preferred_element_type=F32)
    o_ref[:, 0:MLA_IN_PAD] = z[:, 0:MLA_IN_PAD].astype(o_ref.dtype)
    for blk in range(2 * N_PAIR):
        cols = slice(MLA_IN_PAD + blk * LANES, MLA_IN_PAD + (blk + 1) * LANES)
        v = z[:, cols]
        if rotary:
            v = _rotate_pairs(v, rest[0][...], rest[1][...], rest[2][...])
        if blk >= N_PAIR:
            v = v * (DK_R ** -0.5)
        o_ref[:, cols] = v.astype(o_ref.dtype)
    tail = MLA_IN_PAD + 2 * RET_W
    o_ref[:, tail:] = z[:, tail:].astype(o_ref.dtype)


def _norm_mod_matmul_call(x2, g, mod, t, w, tables=None):
    n = x2.shape[0]
    tm = _pick_tile(t, 512)
    tpb = t // tm
    n_out = w.shape[1]
    in_specs = [
        pl.BlockSpec((tm, D_MODEL), lambda i: (i, 0)),
        pl.BlockSpec((1, D_MODEL), lambda i: (0, 0)),
        pl.BlockSpec((1, ADA_CHUNKS, D_MODEL), lambda i: (i // tpb, 0, 0)),
        pl.BlockSpec((D_MODEL, n_out), lambda i: (0, 0)),
    ]
    args = [x2, g.reshape(1, D_MODEL), mod, w]
    if tables is not None:
        in_specs += [pl.BlockSpec((tm, LANES), lambda i: (i % tpb, 0))] * 3
        args += list(tables)
    return pl.pallas_call(
        functools.partial(_nmm_kernel, rotary=tables is not None),
        grid=(n // tm,),
        in_specs=in_specs,
        out_specs=pl.BlockSpec((tm, n_out), lambda i: (i, 0)),
        out_shape=jax.ShapeDtypeStruct((n, n_out), BF16),
        compiler_params=_params(1),
        name="norm_mod_w_in",
    )(*args)


def _mla_prep_kernel(z_ref, gq_ref, gkv_ref, wq_ref, wqs_ref, wkc_ref, wkcs_ref, wkr_ref, wkrs_ref,
                     wvt_ref, gqn_ref, gqns_ref, gkn_ref, gkns_ref, cos_ref, sin_ref,
                     q_ref, k_ref, vt_ref):
    cq = z_ref[:, 0:Q_LORA].astype(F32)
    ckv = z_ref[:, Q_LORA:Q_LORA + KV_LORA].astype(F32)
    kr = z_ref[:, Q_LORA + KV_LORA:MLA_IN_PAD]
    cqn = (cq * lax.rsqrt(jnp.mean(cq * cq, axis=-1, keepdims=True) + EPS) * gq_ref[...]).astype(BF16)
    ckn = (ckv * lax.rsqrt(jnp.mean(ckv * ckv, axis=-1, keepdims=True) + EPS) * gkv_ref[...]).astype(BF16)
    q = jnp.dot(cqn, wq_ref[...], preferred_element_type=F32)
    qs = jnp.dot(cqn, wqs_ref[...], preferred_element_type=F32)
    k = jnp.dot(ckn, wkc_ref[...], preferred_element_type=F32) + jnp.dot(kr, wkr_ref[...], preferred_element_type=F32)
    ks = jnp.dot(ckn, wkcs_ref[...], preferred_element_type=F32) + jnp.dot(kr, wkrs_ref[...], preferred_element_type=F32)
    vt_ref[...] = lax.dot_general(wvt_ref[...], ckn, _NT, preferred_element_type=F32).astype(vt_ref.dtype)
    cos = cos_ref[...]
    sin = sin_ref[...]
    q_scale = QK_DIM ** -0.5 * math.log2(math.e)
    for h in range(H_A):
        sl = slice(h * HEAD_SLOT, (h + 1) * HEAD_SLOT)
        qh = q[:, sl]
        rq = lax.rsqrt(jnp.sum(qh * qh, axis=-1, keepdims=True) * (1.0 / QK_DIM) + EPS)
        q_rot = (qh * gqn_ref[...] * cos + qs[:, sl] * gqns_ref[...] * sin) * (rq * q_scale)
        q_ref[:, sl] = q_rot.astype(q_ref.dtype)
        kh = k[:, sl]
        rk = lax.rsqrt(jnp.sum(kh * kh, axis=-1, keepdims=True) * (1.0 / QK_DIM) + EPS)
        k_rot = (kh * gkn_ref[...] * cos + ks[:, sl] * gkns_ref[...] * sin) * rk
        k_ref[:, sl] = k_rot.astype(k_ref.dtype)


def _mla_prep_call(z, t, wts, cos_t, sin_t):
    n = z.shape[0]
    tm = _pick_tile(t, 512)
    tpb = t // tm
    full = lambda a: pl.BlockSpec(a.shape, lambda i: (0,) * a.ndim)
    in_specs = [pl.BlockSpec((tm, MLA_IN_PAD), lambda i: (i, 0))] + [full(a) for a in wts] + [
        pl.BlockSpec((tm, HEAD_SLOT), lambda i: (i % tpb, 0)),
        pl.BlockSpec((tm, HEAD_SLOT), lambda i: (i % tpb, 0)),
    ]
    hw = H_A * HEAD_SLOT
    vw = H_A * V_DIM
    return pl.pallas_call(
        _mla_prep_kernel,
        grid=(n // tm,),
        in_specs=in_specs,
        out_specs=[
            pl.BlockSpec((tm, hw), lambda i: (i, 0)),
            pl.BlockSpec((tm, hw), lambda i: (i, 0)),
            pl.BlockSpec((vw, tm), lambda i: (0, i)),
        ],
        out_shape=[
            jax.ShapeDtypeStruct((n, hw), BF16),
            jax.ShapeDtypeStruct((n, hw), BF16),
            jax.ShapeDtypeStruct((vw, n), BF16),
        ],
        compiler_params=_params(1),
        name="mla_prep",
    )(z, *wts, cos_t, sin_t)


def _mla_weights(q_norm_g, w_uq, kv_norm_g, w_ukv, qn_g, kn_g):
    pad = HEAD_SLOT - QK_DIM
    half = ROPE_DIM // 2
    perm = jnp.arange(HEAD_SLOT)
    perm = perm.at[NOPE_DIM:NOPE_DIM + half].set(jnp.arange(NOPE_DIM + half, NOPE_DIM + ROPE_DIM))
    perm = perm.at[NOPE_DIM + half:NOPE_DIM + ROPE_DIM].set(jnp.arange(NOPE_DIM, NOPE_DIM + half))

    def slots(w):
        wp = jnp.pad(w, ((0, 0), (0, 0), (0, pad)))
        return wp, wp[:, :, perm]

    def flat(w):
        return w.reshape(w.shape[0], H_A * HEAD_SLOT).astype(BF16)

    wq, wqs = slots(w_uq.reshape(Q_LORA, H_A, QK_DIM))
    w_kv = w_ukv.reshape(KV_LORA, H_A, NOPE_DIM + V_DIM)
    wkc, wkcs = slots(jnp.pad(w_kv[:, :, :NOPE_DIM], ((0, 0), (0, 0), (0, ROPE_DIM))))
    eye = jnp.zeros((HEAD_SLOT, QK_DIM), F32).at[jnp.arange(ROPE_DIM), NOPE_DIM + jnp.arange(ROPE_DIM)].set(1.0)
    wkr, wkrs = slots(jnp.broadcast_to(eye[:, None, :], (HEAD_SLOT, H_A, QK_DIM)))
    wvt = w_kv[:, :, NOPE_DIM:].reshape(KV_LORA, H_A * V_DIM).T.astype(BF16)

    def gains(g):
        gp = jnp.pad(g.astype(F32), (0, pad))
        return gp.reshape(1, HEAD_SLOT), gp[perm].reshape(1, HEAD_SLOT)

    gqn, gqns = gains(qn_g)
    gkn, gkns = gains(kn_g)
    return [q_norm_g.reshape(1, Q_LORA).astype(F32), kv_norm_g.reshape(1, KV_LORA).astype(F32),
            flat(wq), flat(wqs), flat(wkc), flat(wkcs), flat(wkr), flat(wkrs), wvt, gqn, gqns, gkn, gkns]


def _mla_tables(n_tok, rotary):
    cos = jnp.ones((n_tok, HEAD_SLOT), F32)
    sin = jnp.zeros((n_tok, HEAD_SLOT), F32)
    if rotary:
        rows = n_tok // GRID_W
        r = jnp.repeat(jnp.arange(rows, dtype=F32), GRID_W)
        col = jnp.tile(jnp.arange(GRID_W, dtype=F32), rows)
        nf = ROPE_DIM // 4
        f = ROPE_BASE ** (-jnp.arange(nf, dtype=F32) / nf)
        ang = jnp.concatenate([r[:, None] * f, col[:, None] * f], axis=-1)
        c, s = jnp.cos(ang), jnp.sin(ang)
        half = ROPE_DIM // 2
        cos = cos.at[:, NOPE_DIM:NOPE_DIM + half].set(c).at[:, NOPE_DIM + half:NOPE_DIM + ROPE_DIM].set(c)
        sin = sin.at[:, NOPE_DIM:NOPE_DIM + half].set(-s).at[:, NOPE_DIM + half:NOPE_DIM + ROPE_DIM].set(s)
    return cos, sin


def _attn_kernel(*refs, seg_lens):
    nseg = len(seg_lens)
    q_ref = refs[0]
    k_refs = refs[1:1 + nseg]
    vt_refs = refs[1 + nseg:1 + 2 * nseg]
    o_ref = refs[1 + 2 * nseg]
    s_s = refs[2 + 2 * nseg]
    chunks = [(g, r) for g, n in enumerate(seg_lens) for r in range(0, n, KEY_CHUNK)]
    tq = q_ref.shape[0]

    def fold(x, op):
        n = x.shape[0] // 8
        x = x.reshape(n, 8, tq)
        out = x[0]
        for i in range(1, n):
            out = op(out, x[i])
        return out

    def scores(h):
        sl = slice(h * HEAD_SLOT, (h + 1) * HEAD_SLOT)
        qh = q_ref[:, sl]
        mx = None
        for c, (g, r) in enumerate(chunks):
            s = lax.dot_general(k_refs[g][r:r + KEY_CHUNK, sl], qh, _NT, preferred_element_type=F32)
            s_s[h % 2, c] = s
            cm = fold(s, jnp.maximum)
            mx = cm if mx is None else jnp.maximum(mx, cm)
        return jnp.max(mx, axis=0, keepdims=True)

    def weighted_values(h, m):
        rows = slice(h * V_DIM, (h + 1) * V_DIM)
        lsum = None
        o_t = None
        for c, (g, r) in enumerate(chunks):
            pieces = []
            for r0 in range(0, KEY_CHUNK, SUB_ROWS):
                p = jnp.exp2(s_s[h % 2, c, r0:r0 + SUB_ROWS, :] - m)
                ps = fold(p, jnp.add)
                lsum = ps if lsum is None else lsum + ps
                pieces.append(p.astype(BF16))
            pb = jnp.concatenate(pieces, axis=0)
            t = jnp.dot(vt_refs[g][rows, r:r + KEY_CHUNK], pb, preferred_element_type=F32)
            o_t = t if o_t is None else o_t + t
        return o_t * (1.0 / jnp.sum(lsum, axis=0, keepdims=True))

    outs = []
    m_next = scores(0)
    for h in range(H_A):
        m_cur = m_next
        if h + 1 < H_A:
            m_next = scores(h + 1)
        outs.append(weighted_values(h, m_cur))
        if h % 2 == 1:
            j = h // 2
            o_ref[:, j * LANES:(j + 1) * LANES] = jnp.concatenate(outs[-2:], axis=0).T.astype(o_ref.dtype)


def _attn_call(q, ks, vts, t, tks):
    n = q.shape[0]
    tq = _pick_tile(t, 256)
    tpb = t // tq
    hw = H_A * HEAD_SLOT
    vw = H_A * V_DIM
    n_chunks = sum(tk // KEY_CHUNK for tk in tks)
    in_specs = [pl.BlockSpec((tq, hw), lambda i: (i, 0))]
    in_specs += [pl.BlockSpec((tk, hw), lambda i: (i // tpb, 0)) for tk in tks]
    in_specs += [pl.BlockSpec((vw, tk), lambda i: (0, i // tpb)) for tk in tks]
    return pl.pallas_call(
        functools.partial(_attn_kernel, seg_lens=tuple(tks)),
        grid=(n // tq,),
        in_specs=in_specs,
        out_specs=pl.BlockSpec((tq, vw), lambda i: (i, 0)),
        out_shape=jax.ShapeDtypeStruct((n, vw), BF16),
        scratch_shapes=[pltpu.VMEM((2, n_chunks, KEY_CHUNK, tq), F32)],
        compiler_params=_params(1),
        name="attention_%dseg" % len(tks),
    )(q, *ks, *vts)


def _ret_kernel(lg_ref, qf_ref, kf_ref, vf_ref, qb_ref, kb_ref, vb_ref, s0f_ref, s0b_ref,
                of_ref, ob_ref, sff_ref, sfb_ref, df_s, db_s, sf_s, sb_s, *, chunk):
    b = pl.program_id(0)
    c = pl.program_id(1)
    nc = pl.num_programs(1)

    @pl.when(jnp.logical_and(b == 0, c == 0))
    def _():
        ii = lax.broadcasted_iota(jnp.int32, (chunk, chunk), 0)
        jj = lax.broadcasted_iota(jnp.int32, (chunk, chunk), 1)
        diff = (ii - jj).astype(F32)
        for h in range(H_R):
            df_s[h] = jnp.where(ii >= jj, jnp.exp(jnp.where(ii >= jj, diff, 0.0) * lg_ref[0, h]), 0.0)
            db_s[h] = jnp.where(jj > ii, jnp.exp(jnp.where(jj > ii, -diff, 0.0) * lg_ref[1, h]), 0.0)

    @pl.when(c == 0)
    def _():
        sf_s[...] = s0f_ref[0]
        sb_s[...] = s0b_ref[0]

    lane = lax.broadcasted_iota(jnp.int32, (1, LANES), 1)
    lo = lane < DK_R
    row = lax.broadcasted_iota(jnp.int32, (LANES, LANES), 0)
    colm = lax.broadcasted_iota(jnp.int32, (LANES, LANES), 1)
    blockdiag = (row < DK_R) == (colm < DK_R)
    pos = lax.broadcasted_iota(jnp.int32, (chunk, 1), 0).astype(F32)
    dirs = ((qf_ref, kf_ref, vf_ref, df_s, sf_s, of_ref), (qb_ref, kb_ref, vb_ref, db_s, sb_s, ob_ref))
    units = [(d, j) for d in range(2) for j in range(N_PAIR)]

    ops = {}
    for d, j in units:
        q_ref, k_ref, v_ref = dirs[d][:3]
        sl = slice(j * LANES, (j + 1) * LANES)
        lg = jnp.where(lo, lg_ref[d, 2 * j], lg_ref[d, 2 * j + 1])
        q2 = q_ref[:, sl].astype(F32)
        k2 = k_ref[:, sl].astype(F32)
        if d == 0:
            q_dec = jnp.exp((pos + 1.0) * lg)
            k_dec = jnp.exp((chunk - 1.0 - pos) * lg)
        else:
            q_dec = jnp.exp((chunk - pos) * lg)
            k_dec = jnp.exp(pos * lg)
        ops[d, j] = dict(
            sl=sl, v2=v_ref[:, sl], c_dec=jnp.exp(chunk * lg), k2b=k2.astype(BF16),
            q_dec=(q2 * q_dec).astype(BF16), kd_t=(k2 * k_dec).T.astype(BF16),
            q_lo=jnp.where(lo, q2, 0.0).astype(BF16), q_hi=jnp.where(lo, 0.0, q2).astype(BF16))
    for d, j in units:
        u = ops[d, j]
        u['s2'] = dirs[d][4][j]
        u['o'] = jnp.dot(u['q_dec'], u['s2'].astype(BF16), preferred_element_type=F32)
        u['sc'] = [lax.dot_general(u[name], u['k2b'], _NT, preferred_element_type=F32) for name in ('q_lo', 'q_hi')]
    for d, j in units:
        u = ops[d, j]
        d_s, o_ref = dirs[d][3], dirs[d][5]
        o = u['o']
        for e in range(2):
            ve = jnp.where(lo if e == 0 else jnp.logical_not(lo), u['v2'], jnp.zeros((), BF16))
            o = o + jnp.dot((u['sc'][e] * d_s[2 * j + e]).astype(BF16), ve, preferred_element_type=F32)
        o_ref[:, u['sl']] = o.astype(o_ref.dtype)
    for d, j in units:
        u = ops[d, j]
        upd = jnp.dot(u['kd_t'], u['v2'], preferred_element_type=F32)
        dirs[d][4][j] = u['s2'] * u['c_dec'] + jnp.where(blockdiag, upd, 0.0)

    @pl.when(c == nc - 1)
    def _():
        sff_ref[0] = sf_s[...]
        sfb_ref[0] = sb_s[...]


def _ret_call(z, lg, s0f, s0b, b, t):
    n = z.shape[0]
    chunk = _pick_tile(t, 256)
    nc = t // chunk
    col0 = MLA_IN_PAD // RET_W

    def zspec(part, rev):
        if rev:
            return pl.BlockSpec((chunk, RET_W), lambda bi, ci: (bi * nc + nc - 1 - ci, col0 + part))
        return pl.BlockSpec((chunk, RET_W), lambda bi, ci: (bi * nc + ci, col0 + part))

    st_spec = pl.BlockSpec((1, N_PAIR, LANES, LANES), lambda bi, ci: (bi, 0, 0, 0))
    in_specs = [pl.BlockSpec(memory_space=pltpu.SMEM)]
    in_specs += [zspec(0, False), zspec(1, False), zspec(2, False)]
    in_specs += [zspec(0, True), zspec(1, True), zspec(2, True)]
    in_specs += [st_spec, st_spec]
    out_specs = [
        pl.BlockSpec((chunk, RET_W), lambda bi, ci: (bi * nc + ci, 0)),
        pl.BlockSpec((chunk, RET_W), lambda bi, ci: (bi * nc + nc - 1 - ci, 0)),
        st_spec, st_spec,
    ]
    st_shape = jax.ShapeDtypeStruct((b, N_PAIR, LANES, LANES), F32)
    return pl.pallas_call(
        functools.partial(_ret_kernel, chunk=chunk),
        grid=(b, nc),
        in_specs=in_specs,
        out_specs=out_specs,
        out_shape=[jax.ShapeDtypeStruct((n, RET_W), BF16), jax.ShapeDtypeStruct((n, RET_W), BF16),
                   st_shape, st_shape],
        scratch_shapes=[
            pltpu.VMEM((H_R, chunk, chunk), F32), pltpu.VMEM((H_R, chunk, chunk), F32),
            pltpu.VMEM((N_PAIR, LANES, LANES), F32), pltpu.VMEM((N_PAIR, LANES, LANES), F32),
        ],
        compiler_params=_params(2),
        name="retention",
    )(lg, z, z, z, z, z, z, s0f, s0b)


def _ret_tables(n_tok):
    nf = DK_R // 2
    theta = ROPE_BASE ** (-jnp.arange(nf, dtype=F32) / nf)
    ang = jnp.arange(n_tok, dtype=F32)[:, None] * theta
    c, s = jnp.cos(ang), jnp.sin(ang)
    z = jnp.zeros_like(s)
    return (jnp.concatenate([c, c, c, c], axis=-1), jnp.concatenate([-s, z, -s, z], axis=-1),
            jnp.concatenate([z, s, z, s], axis=-1))


def _mix_out_kernel(x_ref, a_ref, of_ref, ob_ref, g_ref, w_ref, mod_ref, o_ref):
    lane = lax.broadcasted_iota(jnp.int32, (1, LANES), 1)
    lo = lane < DV_R
    y = jnp.dot(a_ref[...], w_ref[0:H_A * V_DIM, :], preferred_element_type=F32)
    for j in range(N_PAIR):
        sl = slice(j * LANES, (j + 1) * LANES)
        o = of_ref[:, sl].astype(F32) + ob_ref[:, sl].astype(F32)
        s_lo = jnp.sum(jnp.where(lo, o, 0.0), axis=-1, keepdims=True)
        s_all = jnp.sum(o, axis=-1, keepdims=True)
        mu = jnp.where(lo, s_lo, s_all - s_lo) * (1.0 / DV_R)
        oc = o - mu
        q = oc * oc
        q_lo = jnp.sum(jnp.where(lo, q, 0.0), axis=-1, keepdims=True)
        q_all = jnp.sum(q, axis=-1, keepdims=True)
        var = jnp.where(lo, q_lo, q_all - q_lo) * (1.0 / DV_R)
        g = g_ref[:, sl].astype(F32)
        r = (oc * lax.rsqrt(var + EPS)) * (g * _sigmoid(g))
        row0 = H_A * V_DIM + j * LANES
        y = y + jnp.dot(r.astype(BF16), w_ref[row0:row0 + LANES, :], preferred_element_type=F32)
    o_ref[...] = x_ref[...] + mod_ref[0, 2:3, :] * y


def _mix_out_call(x2, a, o_f, o_b, z, w_out, mod, t):
    n = x2.shape[0]
    tm = _pick_tile(t, 512)
    tpb = t // tm
    gate_col = MLA_IN_PAD // RET_W + 3
    return pl.pallas_call(
        _mix_out_kernel,
        grid=(n // tm,),
        in_specs=[
            pl.BlockSpec((tm, D_MODEL), lambda i: (i, 0)),
            pl.BlockSpec((tm, H_A * V_DIM), lambda i: (i, 0)),
            pl.BlockSpec((tm, RET_W), lambda i: (i, 0)),
            pl.BlockSpec((tm, RET_W), lambda i: (i, 0)),
            pl.BlockSpec((tm, RET_W), lambda i: (i, gate_col)),
            pl.BlockSpec(w_out.shape, lambda i: (0, 0)),
            pl.BlockSpec((1, ADA_CHUNKS, D_MODEL), lambda i: (i // tpb, 0, 0)),
        ],
        out_specs=pl.BlockSpec((tm, D_MODEL), lambda i: (i, 0)),
        out_shape=jax.ShapeDtypeStruct((n, D_MODEL), F32),
        compiler_params=_params(1),
        name="mix_out",
    )(x2, a, o_f, o_b, z, w_out, mod)


def _mlp_kernel(x_ref, g_ref, mod_ref, w1_ref, w2_ref, o_ref):
    h = _norm_mod(x_ref[...], g_ref, mod_ref, 3, 4).astype(BF16)
    parts = []
    for k in range(0, D_FF, MLP_FF_CHUNK):
        a = jnp.maximum(jnp.dot(h, w1_ref[:, k:k + MLP_FF_CHUNK], preferred_element_type=F32), 0.0)
        parts.append((a * a).astype(BF16))
    y = jnp.dot(jnp.concatenate(parts, axis=1), w2_ref[...], preferred_element_type=F32)
    o_ref[...] = x_ref[...] + mod_ref[0, 5:6, :] * y


def _mlp_call(x2, g, mod, w1, w2, t):
    n = x2.shape[0]
    tm = _pick_tile(t, 512)
    tpb = t // tm
    once = pl.Buffered(1)
    return pl.pallas_call(
        _mlp_kernel,
        grid=(n // tm,),
        in_specs=[
            pl.BlockSpec((tm, D_MODEL), lambda i: (i, 0)),
            pl.BlockSpec((1, D_MODEL), lambda i: (0, 0)),
            pl.BlockSpec((1, ADA_CHUNKS, D_MODEL), lambda i: (i // tpb, 0, 0)),
            pl.BlockSpec((D_MODEL, D_FF), lambda i: (0, 0), pipeline_mode=once),
            pl.BlockSpec((D_FF, D_MODEL), lambda i: (0, 0), pipeline_mode=once),
        ],
        out_specs=pl.BlockSpec((tm, D_MODEL), lambda i: (i, 0)),
        out_shape=jax.ShapeDtypeStruct((n, D_MODEL), F32),
        compiler_params=_params(1),
        name="mlp",
    )(x2, g.reshape(1, D_MODEL), mod, w1, w2)


def _s5_compact_weights(p_f, p_b):
    st = lambda i: jnp.stack([p_f[i].astype(F32), p_b[i].astype(F32)])
    ar, ai, br, bi, cr, ci, log_dt = (st(i) for i in range(7))
    dt = jnp.exp(log_dt)[:, :, None]
    lr, li = dt * ar, dt * ai
    steps = np.arange(S5_L)

    def cexp(tau, x_r, x_i):
        mag = jnp.exp(tau * x_r)
        return mag * jnp.cos(tau * x_i), mag * jnp.sin(tau * x_i)

    def taus(table):
        return jnp.asarray(np.asarray(table, np.float32))[:, :, None, None, None]

    a1r, a1i = cexp(1.0, lr, li)
    nr, ni = a1r - 1.0, a1i
    den = ar * ar + ai * ai
    qr, qi = (nr * ar + ni * ai) / den, (ni * ar - nr * ai) / den
    bbr = qr[..., None] * br - qi[..., None] * bi
    bbi = qr[..., None] * bi + qi[..., None] * br

    slab = lambda x: x.reshape((2, N_SLAB, S5_GPS) + x.shape[2:])
    over_h = lambda x: jnp.repeat(x, S5_GROUP, axis=-1)
    lr_fp, li_fp = lr.reshape(2, N_SLAB, 1, S5_HALF), li.reshape(2, N_SLAB, 1, S5_HALF)
    lr_fh, li_fh = (over_h(slab(x).transpose(0, 1, 3, 2)) for x in (lr, li))
    cr_fh, ci_fh = (slab(x).transpose(0, 1, 4, 2, 3).reshape(2, N_SLAB, S5_STATE, LANES) for x in (cr, ci))
    bbr_fp, bbi_fp = (slab(x).transpose(0, 1, 4, 2, 3).reshape(2, N_SLAB, S5_GROUP, S5_HALF) for x in (bbr, bbi))
    bbr_fh, bbi_fh = (over_h(slab(x).transpose(0, 1, 4, 3, 2)) for x in (bbr, bbi))

    def c_pow(table):
        pr, pi = cexp(taus(table), lr_fh[:, None], li_fh[:, None])
        return cr_fh[:, None] * pr - ci_fh[:, None] * pi, cr_fh[:, None] * pi + ci_fh[:, None] * pr

    dr, di = c_pow([steps, steps])
    kern = jnp.sum(dr[:, :, :, None] * bbr_fh[:, None] - di[:, :, :, None] * bbi_fh[:, None], axis=4)
    lag_f = steps[None, :] - steps[:, None]
    toep = jnp.stack([kern[0][np.clip(lag_f, 0, S5_L - 1)], kern[1][np.clip(-lag_f, 0, S5_L - 1)]])
    valid = np.stack([lag_f >= 0, lag_f <= 0])[:, :, :, None, None, None]
    tt = jnp.where(valid, toep, 0.0).transpose(0, 3, 1, 4, 2, 5).reshape(2, N_SLAB, S5_L * S5_GROUP, S5_K)
    pr, pi = cexp(taus([S5_L - 1 - steps, steps]), lr_fp[:, None], li_fp[:, None])
    wr = pr * bbr_fp[:, None] - pi * bbi_fp[:, None]
    wi = pr * bbi_fp[:, None] + pi * bbr_fp[:, None]
    wt = jnp.stack([wr, wi], axis=4).transpose(0, 2, 1, 3, 4, 5).reshape(2, N_SLAB, S5_L * S5_GROUP, 2 * S5_HALF)
    vr, vi = c_pow([steps + 1, S5_L - steps])
    vt = jnp.stack([vr, -vi], axis=1).transpose(0, 3, 1, 4, 2, 5).reshape(2, N_SLAB, 2 * S5_STATE, S5_K)
    alr, ali = cexp(float(S5_L), lr, li)
    al = jnp.stack([alr[0], ali[0], alr[1], ali[1]]).reshape(4, N_SLAB, S5_HALF).transpose(1, 0, 2)
    coef = jnp.concatenate([al, jnp.zeros_like(al)], axis=1)
    return tt, wt, vt, coef


def _nm_s5_kernel(x_ref, g_ref, mod_ref, h_ref, u_ref, hs_s, *, nb, tk):
    x = x_ref[...]
    ms = jnp.mean(x * x, axis=-1, keepdims=True)
    y = x * lax.rsqrt(ms + EPS) * g_ref[...]
    h = y * (1.0 + mod_ref[:, 1:2, :]) + mod_ref[:, 0:1, :]
    h_ref[...] = h.astype(h_ref.dtype)
    pitch = tk + S5_PITCH_PAD
    for bi in range(nb):
        for j in range(N_SLAB):
            hs_s[j, bi * pitch:bi * pitch + tk, :] = h[bi, :, j * LANES:(j + 1) * LANES]
    for kk in range(tk // S5_L):
        for s in range(S5_L):
            for j in range(N_SLAB):
                blk = hs_s[j, pl.ds(kk * S5_L + s, nb, stride=pitch), :]
                u_ref[j, kk * nb:(kk + 1) * nb, s * LANES:(s + 1) * LANES] = blk.astype(u_ref.dtype)


def _norm_mod_s5_call(x3, g, mod):
    b, t, _ = x3.shape
    tk = S5_TOK
    rows = (tk // S5_L) * b
    return pl.pallas_call(
        functools.partial(_nm_s5_kernel, nb=b, tk=tk),
        grid=(t // tk,),
        in_specs=[
            pl.BlockSpec((b, tk, D_MODEL), lambda i: (0, i, 0)),
            pl.BlockSpec((1, D_MODEL), lambda i: (0, 0)),
            pl.BlockSpec((b, ADA_CHUNKS, D_MODEL), lambda i: (0, 0, 0)),
        ],
        out_specs=[
            pl.BlockSpec((b, tk, D_MODEL), lambda i: (0, i, 0)),
            pl.BlockSpec((N_SLAB, rows, S5_K), lambda i: (0, i, 0)),
        ],
        out_shape=[
            jax.ShapeDtypeStruct((b, t, D_MODEL), BF16),
            jax.ShapeDtypeStruct((N_SLAB, (t // S5_L) * b, S5_K), BF16),
        ],
        scratch_shapes=[pltpu.VMEM((N_SLAB, b * (tk + S5_PITCH_PAD), LANES), F32)],
        compiler_params=_params(1),
        name="norm_mod_s5",
    )(x3, g.reshape(1, D_MODEL), mod)


def _s5_kernel(uc_ref, ulf_ref, ulb_ref, tt_ref, wt_ref, vt_ref, coef_ref,
               ycf_ref, ycb_ref, ylf_ref, ylb_ref, t_s, w_s, v_s, u_s, s_s, x_s, carry_s, *, nb, nblk):
    i = pl.program_id(1)

    @pl.when(i == 0)
    def _():
        lane = lax.broadcasted_iota(jnp.int32, (1, S5_K), 1)
        grp_out = (lane // S5_GROUP) % S5_GPS
        grp_state = (lane // S5_STATE) % S5_GPS
        rows = S5_GROUP
        for d in range(2):
            for s in range(S5_L):
                tc = tt_ref[d, 0, s * rows:(s + 1) * rows, :]
                wc = wt_ref[d, 0, s * rows:(s + 1) * rows, :]
                for gi in range(S5_GPS):
                    r0 = s * LANES + gi * rows
                    t_s[d, r0:r0 + rows, :] = jnp.where(grp_out == gi, tc, 0.0).astype(BF16)
                    w_s[d, r0:r0 + rows, :] = jnp.where(grp_state == gi, wc, 0.0).astype(BF16)
            for c in range(2):
                vc = vt_ref[d, 0, c * S5_STATE:(c + 1) * S5_STATE, :]
                for gi in range(S5_GPS):
                    r0 = c * S5_HALF + gi * S5_STATE
                    v_s[d, r0:r0 + S5_STATE, :] = jnp.where(grp_out == gi, vc, 0.0).astype(BF16)
        carry_s[...] = jnp.zeros_like(carry_s)
        u_s[0] = uc_ref[0]
        u_s[1] = uc_ref[0]

    @pl.when(i > 0)
    def _():
        u_s[0] = ulf_ref[0]
        u_s[1] = ulb_ref[0]

    for d in range(2):
        s_s[d] = jnp.dot(u_s[d], w_s[d], preferred_element_type=F32)
    ys = [jnp.dot(u_s[d], t_s[d], preferred_element_type=F32) for d in range(2)]
    for d in range(2):
        a_r = coef_ref[0, 2 * d:2 * d + 1, :]
        a_i = coef_ref[0, 2 * d + 1:2 * d + 2, :]
        x_r, x_i = carry_s[d, :, 0:S5_HALF], carry_s[d, :, S5_HALF:2 * S5_HALF]
        for k in range(nblk):
            rows = slice((k if d == 0 else nblk - 1 - k) * nb, (k if d == 0 else nblk - 1 - k) * nb + nb)
            x_s[d, rows, 0:S5_HALF] = x_r
            x_s[d, rows, S5_HALF:2 * S5_HALF] = x_i
            s_r = s_s[d, rows, 0:S5_HALF]
            s_i = s_s[d, rows, S5_HALF:2 * S5_HALF]
            x_r, x_i = a_r * x_r - a_i * x_i + s_r, a_r * x_i + a_i * x_r + s_i
        carry_s[d, :, 0:S5_HALF] = x_r
        carry_s[d, :, S5_HALF:2 * S5_HALF] = x_i
    for d in range(2):
        y = ys[d] + jnp.dot(x_s[d].astype(BF16), v_s[d], preferred_element_type=F32)
        ys[d] = y.astype(BF16)

    @pl.when(i == 0)
    def _():
        ycf_ref[0] = ys[0]
        ycb_ref[0] = ys[1]

    @pl.when(i > 0)
    def _():
        ylf_ref[0] = ys[0]
        ylb_ref[0] = ys[1]


def _s5_call(u_c, u_l, tt, wt, vt, coef, nb):
    rows = u_c.shape[1]
    n_lat = u_l.shape[1] // rows
    nblk = rows // nb
    tile = lambda fn: pl.BlockSpec((1, rows, S5_K), fn)
    wspec = lambda a: pl.BlockSpec((2, 1) + a.shape[2:], lambda j, i: (0, j, 0, 0))
    ctx_map = lambda j, i: (j, 0, 0)
    fwd_map = lambda j, i: (j, jnp.maximum(i - 1, 0), 0)
    bwd_map = lambda j, i: (j, jnp.minimum(n_lat - i, n_lat - 1), 0)
    yc = jax.ShapeDtypeStruct(u_c.shape, BF16)
    yl = jax.ShapeDtypeStruct(u_l.shape, BF16)
    return pl.pallas_call(
        functools.partial(_s5_kernel, nb=nb, nblk=nblk),
        grid=(N_SLAB, n_lat + 1),
        in_specs=[tile(ctx_map), tile(fwd_map), tile(bwd_map), wspec(tt), wspec(wt), wspec(vt),
                  pl.BlockSpec((1,) + coef.shape[1:], lambda j, i: (j, 0, 0))],
        out_specs=[tile(ctx_map), tile(ctx_map), tile(fwd_map), tile(bwd_map)],
        out_shape=[yc, yc, yl, yl],
        scratch_shapes=[
            pltpu.VMEM((2, S5_K, S5_K), BF16),
            pltpu.VMEM((2, S5_K, 2 * S5_HALF), BF16),
            pltpu.VMEM((2, 2 * S5_HALF, S5_K), BF16),
            pltpu.VMEM((2, rows, S5_K), BF16),
            pltpu.VMEM((2, rows, 2 * S5_HALF), F32),
            pltpu.VMEM((2, rows, 2 * S5_HALF), F32),
            pltpu.VMEM((2, nb, 2 * S5_HALF), F32),
        ],
        compiler_params=_params(2),
        name="s5_scan",
    )(u_c, u_l, u_l, tt, wt, vt, coef)


def _s5_out_kernel(x_ref, yf_ref, yb_ref, h_ref, d_ref, w_ref, mod_ref, o_ref, ys_s, *, nb, tk):
    pitch = tk + S5_PITCH_PAD
    for kk in range(tk // S5_L):
        rows = slice(kk * nb, (kk + 1) * nb)
        for s in range(S5_L):
            lanes = slice(s * LANES, (s + 1) * LANES)
            for j in range(N_SLAB):
                blk = yf_ref[j, rows, lanes].astype(F32) + yb_ref[j, rows, lanes].astype(F32)
                ys_s[j, pl.ds(kk * S5_L + s, nb, stride=pitch), :] = blk
    y = jnp.concatenate(
        [jnp.concatenate([ys_s[j, bi * pitch:bi * pitch + tk, :] for bi in range(nb)], axis=0) for j in range(N_SLAB)],
        axis=1)
    y = y + d_ref[...] * h_ref[...].reshape(nb * tk, D_MODEL).astype(F32)
    g = 0.5 * y * (1.0 + jnp.tanh(math.sqrt(2.0 / math.pi) * (y + 0.044715 * (y * y * y))))
    z = jnp.dot(g.astype(BF16), w_ref[...], preferred_element_type=F32)
    out = z[:, :D_MODEL] * _sigmoid(z[:, D_MODEL:])
    o_ref[...] = x_ref[...] + mod_ref[:, 2:3, :] * out.reshape(nb, tk, D_MODEL)


def _s5_out_call(x3, y_f, y_b, h3, d_skip, w_glu, mod):
    b, t, _ = x3.shape
    tk = S5_TOK
    rows = (tk // S5_L) * b
    tile = pl.BlockSpec((b, tk, D_MODEL), lambda i: (0, i, 0))
    ytile = pl.BlockSpec((N_SLAB, rows, S5_K), lambda i: (0, i, 0))
    return pl.pallas_call(
        functools.partial(_s5_out_kernel, nb=b, tk=tk),
        grid=(t // tk,),
        in_specs=[
            tile, ytile, ytile, tile,
            pl.BlockSpec((1, D_MODEL), lambda i: (0, 0)),
            pl.BlockSpec(w_glu.shape, lambda i: (0, 0)),
            pl.BlockSpec((b, ADA_CHUNKS, D_MODEL), lambda i: (0, 0, 0)),
        ],
        out_specs=tile,
        out_shape=jax.ShapeDtypeStruct((b, t, D_MODEL), F32),
        scratch_shapes=[pltpu.VMEM((N_SLAB, b * (tk + S5_PITCH_PAD), LANES), F32)],
        compiler_params=_params(1),
        name="s5_out",
    )(x3, y_f, y_b, h3, d_skip.reshape(1, D_MODEL).astype(F32), w_glu, mod)


def _even_layer(x2, xc2, mod_l, mod_c, b, t, tc, need_ctx, p):
    n_c = b * tc
    z_c = _norm_mod_matmul_call(xc2, p['norm1_g'], mod_c, n_c, p['w_in'])
    z_l = _norm_mod_matmul_call(x2, p['norm1_g'], mod_l, t, p['w_in'], tables=_ret_tables(t))
    q_c, k_c, vt_c = _mla_prep_call(z_c, n_c, p['mla'], *_mla_tables(n_c, False))
    q_l, k_l, vt_l = _mla_prep_call(z_l, t, p['mla'], *_mla_tables(t, True))
    a_l = _attn_call(q_l, [k_c, k_l], [vt_c, vt_l], t, [tc, t])
    zero = jnp.zeros((b, N_PAIR, LANES, LANES), F32)
    of_c, ob_c, s_cf, s_cb = _ret_call(z_c, p['lg'], zero, zero, b, tc)
    of_l, ob_l, _, _ = _ret_call(z_l, p['lg'], s_cf, s_cb, b, t)
    x2 = _mix_out_call(x2, a_l, of_l, ob_l, z_l, p['w_out'], mod_l, t)
    if need_ctx:
        a_c = _attn_call(q_c, [k_c], [vt_c], tc, [tc])
        xc2 = _mix_out_call(xc2, a_c, of_c, ob_c, z_c, p['w_out'], mod_c, n_c)
    return x2, xc2


def _odd_layer(x2, xc2, mod_l, mod_c, b, t, tc, need_ctx, p):
    x3, xc3 = x2.reshape(b, t, D_MODEL), xc2.reshape(b, tc, D_MODEL)
    h_c, u_c = _norm_mod_s5_call(xc3, p['norm1_g'], mod_c)
    h_l, u_l = _norm_mod_s5_call(x3, p['norm1_g'], mod_l)
    tt, wt, vt, coef = p['s5']
    ycf, ycb, ylf, ylb = _s5_call(u_c, u_l, tt, wt, vt, coef, b)
    x2 = _s5_out_call(x3, ylf, ylb, h_l, p['d_skip'], p['w_glu'], mod_l).reshape(b * t, D_MODEL)
    if need_ctx:
        xc2 = _s5_out_call(xc3, ycf, ycb, h_c, p['d_skip'], p['w_glu'], mod_c).reshape(b * tc, D_MODEL)
    return x2, xc2


def kernel(x, c, ctx, c_ctx, ada_w, ada_b, norm1_g, norm2_g, mlp_w1, mlp_w2, w_in, mla_q_norm_g, mla_w_uq, mla_kv_norm_g, mla_w_ukv, mla_qn_g, mla_kn_g, ret_lg_f, ret_lg_b, w_out, s5_a_re_f, s5_a_im_f, s5_b_re_f, s5_b_im_f, s5_c_re_f, s5_c_im_f, s5_log_dt_f, s5_a_re_b, s5_a_im_b, s5_b_re_b, s5_b_im_b, s5_c_re_b, s5_c_im_b, s5_log_dt_b, s5_d, s5_w_glu):
    b, t, _ = x.shape
    tc = ctx.shape[1]
    depth = ada_w.shape[0]
    assert b % 8 == 0 and tc % KEY_CHUNK == 0 and t % tc == 0 and tc % S5_TOK == 0
    rows = -(-(b + 1) // 8) * 8
    cc = jnp.zeros((rows, D_MODEL), F32).at[:b].set(c.astype(F32)).at[b].set(c_ctx.astype(F32))
    mod = _ada_all(cc, ada_w.astype(F32), ada_b.astype(F32))
    x2 = x.reshape(b * t, D_MODEL).astype(F32)
    xc2 = ctx.reshape(b * tc, D_MODEL).astype(F32)
    w1_all, w2_all = mlp_w1.astype(BF16), mlp_w2.astype(BF16)
    w_in_all = jnp.concatenate([w_in[:, :, :MLA_IN], jnp.zeros(w_in.shape[:2] + (MLA_IN_PAD - MLA_IN,), w_in.dtype),
                                w_in[:, :, MLA_IN:]], axis=2).astype(BF16)
    w_out_all, w_glu_all = w_out.astype(BF16), s5_w_glu.astype(BF16)
    mla_all = jax.vmap(_mla_weights)(mla_q_norm_g, mla_w_uq, mla_kv_norm_g, mla_w_ukv, mla_qn_g, mla_kn_g)
    lg_all = jnp.stack([jnp.log1p(-jnp.exp2(ret_lg_f.astype(F32))), jnp.log1p(-jnp.exp2(ret_lg_b.astype(F32)))], axis=1)
    s5_all = jax.vmap(_s5_compact_weights)(
        (s5_a_re_f, s5_a_im_f, s5_b_re_f, s5_b_im_f, s5_c_re_f, s5_c_im_f, s5_log_dt_f),
        (s5_a_re_b, s5_a_im_b, s5_b_re_b, s5_b_im_b, s5_c_re_b, s5_c_im_b, s5_log_dt_b))
    for l in range(depth):
        need_ctx = l < depth - 1
        mod_l = mod[l, :b].reshape(b, ADA_CHUNKS, D_MODEL)
        mod_c = jnp.broadcast_to(mod[l, b].reshape(1, ADA_CHUNKS, D_MODEL), (b, ADA_CHUNKS, D_MODEL))
        if l % 2 == 0:
            e = l // 2
            p = dict(norm1_g=norm1_g[l], w_in=w_in_all[e], mla=[a[e] for a in mla_all], lg=lg_all[e], w_out=w_out_all[e])
            x2, xc2 = _even_layer(x2, xc2, mod_l, mod_c, b, t, tc, need_ctx, p)
        else:
            o = l // 2
            p = dict(norm1_g=norm1_g[l], d_skip=s5_d[o], w_glu=w_glu_all[o], s5=[a[o] for a in s5_all])
            x2, xc2 = _odd_layer(x2, xc2, mod_l, mod_c, b, t, tc, need_ctx, p)
        x2 = _mlp_call(x2, norm2_g[l], mod_l, w1_all[l], w2_all[l], t)
        if need_ctx:
            xc2 = _mlp_call(xc2, norm2_g[l], mod_c, w1_all[l], w2_all[l], b * tc)
    return x2.reshape(b, t, D_MODEL).astype(x.dtype)
```

```python
import functools
import math

import jax
import jax.numpy as jnp
import numpy as np
from jax import lax
from jax.experimental import pallas as pl
from jax.experimental.pallas import tpu as pltpu

F32 = jnp.float32
BF16 = jnp.bfloat16

D_MODEL = 1024
EPS = 1e-6
ADA_CHUNKS = 6
GRID_W = 64
ROPE_BASE = 10000.0
LANES = 128
N_SLAB = D_MODEL // LANES

H_A = 8
Q_LORA = 256
KV_LORA = 128
NOPE_DIM = 64
ROPE_DIM = 32
QK_DIM = NOPE_DIM + ROPE_DIM
V_DIM = 64
HEAD_SLOT = LANES
MLA_IN = Q_LORA + KV_LORA + ROPE_DIM
MLA_IN_PAD = 512
KEY_CHUNK = 256
SUB_ROWS = 64

H_R = 8
DK_R = 64
DV_R = 64
RET_W = H_R * DK_R
N_PAIR = H_R // 2

S5_GROUP = 16
S5_GROUPS = D_MODEL // S5_GROUP
S5_STATE = 64
S5_L = 4
S5_K = S5_L * LANES
S5_GPS = LANES // S5_GROUP
S5_HALF = S5_GPS * S5_STATE
S5_TOK = 32
S5_PITCH_PAD = 8

D_FF = 4 * D_MODEL
MLP_FF_CHUNK = 512
Z_WIDTH = MLA_IN_PAD + 4 * RET_W

VMEM_LIMIT = 56 * 1024 * 1024

_NT = (((1,), (1,)), ((), ()))
_HP = lax.Precision.HIGHEST


def _params(n_grid):
    return pltpu.CompilerParams(
        dimension_semantics=("arbitrary",) * n_grid, vmem_limit_bytes=VMEM_LIMIT)


def _sigmoid(x):
    return 1.0 / (1.0 + jnp.exp(-x))


def _pick_tile(t, pref):
    tile = min(t, pref)
    while t % tile:
        tile //= 2
    return tile


def _ada_kernel(c_ref, w_ref, b_ref, o_ref):
    cc = c_ref[...]
    s = cc * _sigmoid(cc)
    o_ref[0] = jnp.dot(s, w_ref[0], preferred_element_type=F32, precision=_HP) + b_ref[0]


def _ada_all(cc, ada_w, ada_b):
    depth, _, width = ada_w.shape
    r = cc.shape[0]
    tn = 1536
    return pl.pallas_call(
        _ada_kernel,
        grid=(depth, width // tn),
        in_specs=[
            pl.BlockSpec((r, D_MODEL), lambda l, j: (0, 0)),
            pl.BlockSpec((1, D_MODEL, tn), lambda l, j: (l, 0, j)),
            pl.BlockSpec((1, 1, tn), lambda l, j: (l, 0, j)),
        ],
        out_specs=pl.BlockSpec((1, r, tn), lambda l, j: (l, 0, j)),
        out_shape=jax.ShapeDtypeStruct((depth, r, width), F32),
        compiler_params=_params(2),
        name="ada_mod",
    )(cc, ada_w, ada_b.reshape(depth, 1, width))


def _norm_mod(x, g_ref, mod_ref, shift_row, scale_row):
    ms = jnp.mean(x * x, axis=-1, keepdims=True)
    y = x * lax.rsqrt(ms + EPS) * g_ref[...]
    return y * (1.0 + mod_ref[0, scale_row:scale_row + 1, :]) + mod_ref[0, shift_row:shift_row + 1, :]


def _rotate_pairs(x, cos, sin_lo, sin_hi):
    return x * cos + pltpu.roll(x, LANES - DK_R // 2, 1) * sin_lo + pltpu.roll(x, DK_R // 2, 1) * sin_hi


def _nmm_kernel(x_ref, g_ref, mod_ref, w_ref, *rest, rotary):
    o_ref = rest[-1]
    h = _norm_mod(x_ref[...], g_ref, mod_ref, 0, 1).astype(BF16)
    z = jnp.dot(h, w_ref[...], preferred_element_type=F32)
    o_ref[:, 0:MLA_IN_PAD] = z[:, 0:MLA_IN_PAD].astype(o_ref.dtype)
    for blk in range(2 * N_PAIR):
        cols = slice(MLA_IN_PAD + blk * LANES, MLA_IN_PAD + (blk + 1) * LANES)
        v = z[:, cols]
        if rotary:
            v = _rotate_pairs(v, rest[0][...], rest[1][...], rest[2][...])
        if blk >= N_PAIR:
            v = v * (DK_R ** -0.5)
        o_ref[:, cols] = v.astype(o_ref.dtype)
    tail = MLA_IN_PAD + 2 * RET_W
    o_ref[:, tail:] = z[:, tail:].astype(o_ref.dtype)


def _norm_mod_matmul_call(x2, g, mod, t, w, tables=None):
    n = x2.shape[0]
    tm = _pick_tile(t, 512)
    tpb = t // tm
    n_out = w.shape[1]
    in_specs = [
        pl.BlockSpec((tm, D_MODEL), lambda i: (i, 0)),
        pl.BlockSpec((1, D_MODEL), lambda i: (0, 0)),
        pl.BlockSpec((1, ADA_CHUNKS, D_MODEL), lambda i: (i // tpb, 0, 0)),
        pl.BlockSpec((D_MODEL, n_out), lambda i: (0, 0)),
    ]
    args = [x2, g.reshape(1, D_MODEL), mod, w]
    if tables is not None:
        in_specs += [pl.BlockSpec((tm, LANES), lambda i: (i % tpb, 0))] * 3
        args += list(tables)
    return pl.pallas_call(
        functools.partial(_nmm_kernel, rotary=tables is not None),
        grid=(n // tm,),
        in_specs=in_specs,
        out_specs=pl.BlockSpec((tm, n_out), lambda i: (i, 0)),
        out_shape=jax.ShapeDtypeStruct((n, n_out), BF16),
        compiler_params=_params(1),
        name="norm_mod_w_in",
    )(*args)


def _mla_prep_kernel(z_ref, gq_ref, gkv_ref, wq_ref, wqs_ref, wkc_ref, wkcs_ref, wkr_ref, wkrs_ref,
                     wvt_ref, gqn_ref, gqns_ref, gkn_ref, gkns_ref, cos_ref, sin_ref,
                     q_ref, k_ref, vt_ref):
    cq = z_ref[:, 0:Q_LORA].astype(F32)
    ckv = z_ref[:, Q_LORA:Q_LORA + KV_LORA].astype(F32)
    kr = z_ref[:, Q_LORA + KV_LORA:MLA_IN_PAD]
    cqn = (cq * lax.rsqrt(jnp.mean(cq * cq, axis=-1, keepdims=True) + EPS) * gq_ref[...]).astype(BF16)
    ckn = (ckv * lax.rsqrt(jnp.mean(ckv * ckv, axis=-1, keepdims=True) + EPS) * gkv_ref[...]).astype(BF16)
    q = jnp.dot(cqn, wq_ref[...], preferred_element_type=F32)
    qs = jnp.dot(cqn, wqs_ref[...], preferred_element_type=F32)
    k = jnp.dot(ckn, wkc_ref[...], preferred_element_type=F32) + jnp.dot(kr, wkr_ref[...], preferred_element_type=F32)
    ks = jnp.dot(ckn, wkcs_ref[...], preferred_element_type=F32) + jnp.dot(kr, wkrs_ref[...], preferred_element_type=F32)
    vt_ref[...] = lax.dot_general(wvt_ref[...], ckn, _NT, preferred_element_type=F32).astype(vt_ref.dtype)
    cos = cos_ref[...]
    sin = sin_ref[...]
    q_scale = QK_DIM ** -0.5 * math.log2(math.e)
    for h in range(H_A):
        sl = slice(h * HEAD_SLOT, (h + 1) * HEAD_SLOT)
        qh = q[:, sl]
        rq = lax.rsqrt(jnp.sum(qh * qh, axis=-1, keepdims=True) * (1.0 / QK_DIM) + EPS)
        q_rot = (qh * gqn_ref[...] * cos + qs[:, sl] * gqns_ref[...] * sin) * (rq * q_scale)
        q_ref[:, sl] = q_rot.astype(q_ref.dtype)
        kh = k[:, sl]
        rk = lax.rsqrt(jnp.sum(kh * kh, axis=-1, keepdims=True) * (1.0 / QK_DIM) + EPS)
        k_rot = (kh * gkn_ref[...] * cos + ks[:, sl] * gkns_ref[...] * sin) * rk
        k_ref[:, sl] = k_rot.astype(k_ref.dtype)


def _mla_prep_call(z, t, wts, cos_t, sin_t):
    n = z.shape[0]
    tm = _pick_tile(t, 512)
    tpb = t // tm
    full = lambda a: pl.BlockSpec(a.shape, lambda i: (0,) * a.ndim)
    in_specs = [pl.BlockSpec((tm, MLA_IN_PAD), lambda i: (i, 0))] + [full(a) for a in wts] + [
        pl.BlockSpec((tm, HEAD_SLOT), lambda i: (i % tpb, 0)),
        pl.BlockSpec((tm, HEAD_SLOT), lambda i: (i % tpb, 0)),
    ]
    hw = H_A * HEAD_SLOT
    vw = H_A * V_DIM
    return pl.pallas_call(
        _mla_prep_kernel,
        grid=(n // tm,),
        in_specs=in_specs,
        out_specs=[
            pl.BlockSpec((tm, hw), lambda i: (i, 0)),
            pl.BlockSpec((tm, hw), lambda i: (i, 0)),
            pl.BlockSpec((vw, tm), lambda i: (0, i)),
        ],
        out_shape=[
            jax.ShapeDtypeStruct((n, hw), BF16),
            jax.ShapeDtypeStruct((n, hw), BF16),
            jax.ShapeDtypeStruct((vw, n), BF16),
        ],
        compiler_params=_params(1),
        name="mla_prep",
    )(z, *wts, cos_t, sin_t)


def _mla_weights(q_norm_g, w_uq, kv_norm_g, w_ukv, qn_g, kn_g):
    pad = HEAD_SLOT - QK_DIM
    half = ROPE_DIM // 2
    perm = jnp.arange(HEAD_SLOT)
    perm = perm.at[NOPE_DIM:NOPE_DIM + half].set(jnp.arange(NOPE_DIM + half, NOPE_DIM + ROPE_DIM))
    perm = perm.at[NOPE_DIM + half:NOPE_DIM + ROPE_DIM].set(jnp.arange(NOPE_DIM, NOPE_DIM + half))

    def slots(w):
        wp = jnp.pad(w, ((0, 0), (0, 0), (0, pad)))
        return wp, wp[:, :, perm]

    def flat(w):
        return w.reshape(w.shape[0], H_A * HEAD_SLOT).astype(BF16)

    wq, wqs = slots(w_uq.reshape(Q_LORA, H_A, QK_DIM))
    w_kv = w_ukv.reshape(KV_LORA, H_A, NOPE_DIM + V_DIM)
    wkc, wkcs = slots(jnp.pad(w_kv[:, :, :NOPE_DIM], ((0, 0), (0, 0), (0, ROPE_DIM))))
    eye = jnp.zeros((HEAD_SLOT, QK_DIM), F32).at[jnp.arange(ROPE_DIM), NOPE_DIM + jnp.arange(ROPE_DIM)].set(1.0)
    wkr, wkrs = slots(jnp.broadcast_to(eye[:, None, :], (HEAD_SLOT, H_A, QK_DIM)))
    wvt = w_kv[:, :, NOPE_DIM:].reshape(KV_LORA, H_A * V_DIM).T.astype(BF16)

    def gains(g):
        gp = jnp.pad(g.astype(F32), (0, pad))
        return gp.reshape(1, HEAD_SLOT), gp[perm].reshape(1, HEAD_SLOT)

    gqn, gqns = gains(qn_g)
    gkn, gkns = gains(kn_g)
    return [q_norm_g.reshape(1, Q_LORA).astype(F32), kv_norm_g.reshape(1, KV_LORA).astype(F32),
            flat(wq), flat(wqs), flat(wkc), flat(wkcs), flat(wkr), flat(wkrs), wvt, gqn, gqns, gkn, gkns]


def _mla_tables(n_tok, rotary):
    cos = jnp.ones((n_tok, HEAD_SLOT), F32)
    sin = jnp.zeros((n_tok, HEAD_SLOT), F32)
    if rotary:
        rows = n_tok // GRID_W
        r = jnp.repeat(jnp.arange(rows, dtype=F32), GRID_W)
        col = jnp.tile(jnp.arange(GRID_W, dtype=F32), rows)
        nf = ROPE_DIM // 4
        f = ROPE_BASE ** (-jnp.arange(nf, dtype=F32) / nf)
        ang = jnp.concatenate([r[:, None] * f, col[:, None] * f], axis=-1)
        c, s = jnp.cos(ang), jnp.sin(ang)
        half = ROPE_DIM // 2
        cos = cos.at[:, NOPE_DIM:NOPE_DIM + half].set(c).at[:, NOPE_DIM + half:NOPE_DIM + ROPE_DIM].set(c)
        sin = sin.at[:, NOPE_DIM:NOPE_DIM + half].set(-s).at[:, NOPE_DIM + half:NOPE_DIM + ROPE_DIM].set(s)
    return cos, sin


def _attn_kernel(*refs, seg_lens):
    nseg = len(seg_lens)
    q_ref = refs[0]
    k_refs = refs[1:1 + nseg]
    vt_refs = refs[1 + nseg:1 + 2 * nseg]
    o_ref = refs[1 + 2 * nseg]
    s_s = refs[2 + 2 * nseg]
    chunks = [(g, r) for g, n in enumerate(seg_lens) for r in range(0, n, KEY_CHUNK)]
    tq = q_ref.shape[0]

    def fold(x, op):
        n = x.shape[0] // 8
        x = x.reshape(n, 8, tq)
        out = x[0]
        for i in range(1, n):
            out = op(out, x[i])
        return out

    def scores(h):
        sl = slice(h * HEAD_SLOT, (h + 1) * HEAD_SLOT)
        qh = q_ref[:, sl]
        mx = None
        for c, (g, r) in enumerate(chunks):
            s = lax.dot_general(k_refs[g][r:r + KEY_CHUNK, sl], qh, _NT, preferred_element_type=F32)
            s_s[h % 2, c] = s
            cm = fold(s, jnp.maximum)
            mx = cm if mx is None else jnp.maximum(mx, cm)
        return jnp.max(mx, axis=0, keepdims=True)

    def weighted_values(h, m):
        rows = slice(h * V_DIM, (h + 1) * V_DIM)
        lsum = None
        o_t = None
        for c, (g, r) in enumerate(chunks):
            pieces = []
            for r0 in range(0, KEY_CHUNK, SUB_ROWS):
                p = jnp.exp2(s_s[h % 2, c, r0:r0 + SUB_ROWS, :] - m)
                ps = fold(p, jnp.add)
                lsum = ps if lsum is None else lsum + ps
                pieces.append(p.astype(BF16))
            pb = jnp.concatenate(pieces, axis=0)
            t = jnp.dot(vt_refs[g][rows, r:r + KEY_CHUNK], pb, preferred_element_type=F32)
            o_t = t if o_t is None else o_t + t
        return o_t * (1.0 / jnp.sum(lsum, axis=0, keepdims=True))

    outs = []
    m_next = scores(0)
    for h in range(H_A):
        m_cur = m_next
        if h + 1 < H_A:
            m_next = scores(h + 1)
        outs.append(weighted_values(h, m_cur))
        if h % 2 == 1:
            j = h // 2
            o_ref[:, j * LANES:(j + 1) * LANES] = jnp.concatenate(outs[-2:], axis=0).T.astype(o_ref.dtype)


def _attn_call(q, ks, vts, t, tks):
    n = q.shape[0]
    tq = _pick_tile(t, 256)
    tpb = t // tq
    hw = H_A * HEAD_SLOT
    vw = H_A * V_DIM
    n_chunks = sum(tk // KEY_CHUNK for tk in tks)
    in_specs = [pl.BlockSpec((tq, hw), lambda i: (i, 0))]
    in_specs += [pl.BlockSpec((tk, hw), lambda i: (i // tpb, 0)) for tk in tks]
    in_specs += [pl.BlockSpec((vw, tk), lambda i: (0, i // tpb)) for tk in tks]
    return pl.pallas_call(
        functools.partial(_attn_kernel, seg_lens=tuple(tks)),
        grid=(n // tq,),
        in_specs=in_specs,
        out_specs=pl.BlockSpec((tq, vw), lambda i: (i, 0)),
        out_shape=jax.ShapeDtypeStruct((n, vw), BF16),
        scratch_shapes=[pltpu.VMEM((2, n_chunks, KEY_CHUNK, tq), F32)],
        compiler_params=_params(1),
        name="attention_%dseg" % len(tks),
    )(q, *ks, *vts)


def _ret_kernel(lg_ref, qf_ref, kf_ref, vf_ref, qb_ref, kb_ref, vb_ref, s0f_ref, s0b_ref,
                of_ref, ob_ref, sff_ref, sfb_ref, df_s, db_s, sf_s, sb_s, *, chunk):
    b = pl.program_id(0)
    c = pl.program_id(1)
    nc = pl.num_programs(1)

    @pl.when(jnp.logical_and(b == 0, c == 0))
    def _():
        ii = lax.broadcasted_iota(jnp.int32, (chunk, chunk), 0)
        jj = lax.broadcasted_iota(jnp.int32, (chunk, chunk), 1)
        diff = (ii - jj).astype(F32)
        for h in range(H_R):
            df_s[h] = jnp.where(ii >= jj, jnp.exp(jnp.where(ii >= jj, diff, 0.0) * lg_ref[0, h]), 0.0)
            db_s[h] = jnp.where(jj > ii, jnp.exp(jnp.where(jj > ii, -diff, 0.0) * lg_ref[1, h]), 0.0)

    @pl.when(c == 0)
    def _():
        sf_s[...] = s0f_ref[0]
        sb_s[...] = s0b_ref[0]

    lane = lax.broadcasted_iota(jnp.int32, (1, LANES), 1)
    lo = lane < DK_R
    row = lax.broadcasted_iota(jnp.int32, (LANES, LANES), 0)
    colm = lax.broadcasted_iota(jnp.int32, (LANES, LANES), 1)
    blockdiag = (row < DK_R) == (colm < DK_R)
    pos = lax.broadcasted_iota(jnp.int32, (chunk, 1), 0).astype(F32)
    dirs = ((qf_ref, kf_ref, vf_ref, df_s, sf_s, of_ref), (qb_ref, kb_ref, vb_ref, db_s, sb_s, ob_ref))
    units = [(d, j) for d in range(2) for j in range(N_PAIR)]

    ops = {}
    for d, j in units:
        q_ref, k_ref, v_ref = dirs[d][:3]
        sl = slice(j * LANES, (j + 1) * LANES)
        lg = jnp.where(lo, lg_ref[d, 2 * j], lg_ref[d, 2 * j + 1])
        q2 = q_ref[:, sl].astype(F32)
        k2 = k_ref[:, sl].astype(F32)
        if d == 0:
            q_dec = jnp.exp((pos + 1.0) * lg)
            k_dec = jnp.exp((chunk - 1.0 - pos) * lg)
        else:
            q_dec = jnp.exp((chunk - pos) * lg)
            k_dec = jnp.exp(pos * lg)
        ops[d, j] = dict(
            sl=sl, v2=v_ref[:, sl], c_dec=jnp.exp(chunk * lg), k2b=k2.astype(BF16),
            q_dec=(q2 * q_dec).astype(BF16), kd_t=(k2 * k_dec).T.astype(BF16),
            q_lo=jnp.where(lo, q2, 0.0).astype(BF16), q_hi=jnp.where(lo, 0.0, q2).astype(BF16))
    for d, j in units:
        u = ops[d, j]
        u['s2'] = dirs[d][4][j]
        u['o'] = jnp.dot(u['q_dec'], u['s2'].astype(BF16), preferred_element_type=F32)
        u['sc'] = [lax.dot_general(u[name], u['k2b'], _NT, preferred_element_type=F32) for name in ('q_lo', 'q_hi')]
    for d, j in units:
        u = ops[d, j]
        d_s, o_ref = dirs[d][3], dirs[d][5]
        o = u['o']
        for e in range(2):
            ve = jnp.where(lo if e == 0 else jnp.logical_not(lo), u['v2'], jnp.zeros((), BF16))
            o = o + jnp.dot((u['sc'][e] * d_s[2 * j + e]).astype(BF16), ve, preferred_element_type=F32)
        o_ref[:, u['sl']] = o.astype(o_ref.dtype)
    for d, j in units:
        u = ops[d, j]
        upd = jnp.dot(u['kd_t'], u['v2'], preferred_element_type=F32)
        dirs[d][4][j] = u['s2'] * u['c_dec'] + jnp.where(blockdiag, upd, 0.0)

    @pl.when(c == nc - 1)
    def _():
        sff_ref[0] = sf_s[...]
        sfb_ref[0] = sb_s[...]


def _ret_call(z, lg, s0f, s0b, b, t):
    n = z.shape[0]
    chunk = _pick_tile(t, 256)
    nc = t // chunk
    col0 = MLA_IN_PAD // RET_W

    def zspec(part, rev):
        if rev:
            return pl.BlockSpec((chunk, RET_W), lambda bi, ci: (bi * nc + nc - 1 - ci, col0 + part))
        return pl.BlockSpec((chunk, RET_W), lambda bi, ci: (bi * nc + ci, col0 + part))

    st_spec = pl.BlockSpec((1, N_PAIR, LANES, LANES), lambda bi, ci: (bi, 0, 0, 0))
    in_specs = [pl.BlockSpec(memory_space=pltpu.SMEM)]
    in_specs += [zspec(0, False), zspec(1, False), zspec(2, False)]
    in_specs += [zspec(0, True), zspec(1, True), zspec(2, True)]
    in_specs += [st_spec, st_spec]
    out_specs = [
        pl.BlockSpec((chunk, RET_W), lambda bi, ci: (bi * nc + ci, 0)),
        pl.BlockSpec((chunk, RET_W), lambda bi, ci: (bi * nc + nc - 1 - ci, 0)),
        st_spec, st_spec,
    ]
    st_shape = jax.ShapeDtypeStruct((b, N_PAIR, LANES, LANES), F32)
    return pl.pallas_call(
        functools.partial(_ret_kernel, chunk=chunk),
        grid=(b, nc),
        in_specs=in_specs,
        out_specs=out_specs,
        out_shape=[jax.ShapeDtypeStruct((n, RET_W), BF16), jax.ShapeDtypeStruct((n, RET_W), BF16),
                   st_shape, st_shape],
        scratch_shapes=[
            pltpu.VMEM((H_R, chunk, chunk), F32), pltpu.VMEM((H_R, chunk, chunk), F32),
            pltpu.VMEM((N_PAIR, LANES, LANES), F32), pltpu.VMEM((N_PAIR, LANES, LANES), F32),
        ],
        compiler_params=_params(2),
        name="retention",
    )(lg, z, z, z, z, z, z, s0f, s0b)


def _ret_tables(n_tok):
    nf = DK_R // 2
    theta = ROPE_BASE ** (-jnp.arange(nf, dtype=F32) / nf)
    ang = jnp.arange(n_tok, dtype=F32)[:, None] * theta
    c, s = jnp.cos(ang), jnp.sin(ang)
    z = jnp.zeros_like(s)
    return (jnp.concatenate([c, c, c, c], axis=-1), jnp.concatenate([-s, z, -s, z], axis=-1),
            jnp.concatenate([z, s, z, s], axis=-1))


def _mix_out_kernel(x_ref, a_ref, of_ref, ob_ref, g_ref, w_ref, mod_ref, o_ref):
    lane = lax.broadcasted_iota(jnp.int32, (1, LANES), 1)
    lo = lane < DV_R
    y = jnp.dot(a_ref[...], w_ref[0:H_A * V_DIM, :], preferred_element_type=F32)
    for j in range(N_PAIR):
        sl = slice(j * LANES, (j + 1) * LANES)
        o = of_ref[:, sl].astype(F32) + ob_ref[:, sl].astype(F32)
        s_lo = jnp.sum(jnp.where(lo, o, 0.0), axis=-1, keepdims=True)
        s_all = jnp.sum(o, axis=-1, keepdims=True)
        mu = jnp.where(lo, s_lo, s_all - s_lo) * (1.0 / DV_R)
        oc = o - mu
        q = oc * oc
        q_lo = jnp.sum(jnp.where(lo, q, 0.0), axis=-1, keepdims=True)
        q_all = jnp.sum(q, axis=-1, keepdims=True)
        var = jnp.where(lo, q_lo, q_all - q_lo) * (1.0 / DV_R)
        g = g_ref[:, sl].astype(F32)
        r = (oc * lax.rsqrt(var + EPS)) * (g * _sigmoid(g))
        row0 = H_A * V_DIM + j * LANES
        y = y + jnp.dot(r.astype(BF16), w_ref[row0:row0 + LANES, :], preferred_element_type=F32)
    o_ref[...] = x_ref[...] + mod_ref[0, 2:3, :] * y


def _mix_out_call(x2, a, o_f, o_b, z, w_out, mod, t):
    n = x2.shape[0]
    tm = _pick_tile(t, 512)
    tpb = t // tm
    gate_col = MLA_IN_PAD // RET_W + 3
    return pl.pallas_call(
        _mix_out_kernel,
        grid=(n // tm,),
        in_specs=[
            pl.BlockSpec((tm, D_MODEL), lambda i: (i, 0)),
            pl.BlockSpec((tm, H_A * V_DIM), lambda i: (i, 0)),
            pl.BlockSpec((tm, RET_W), lambda i: (i, 0)),
            pl.BlockSpec((tm, RET_W), lambda i: (i, 0)),
            pl.BlockSpec((tm, RET_W), lambda i: (i, gate_col)),
            pl.BlockSpec(w_out.shape, lambda i: (0, 0)),
            pl.BlockSpec((1, ADA_CHUNKS, D_MODEL), lambda i: (i // tpb, 0, 0)),
        ],
        out_specs=pl.BlockSpec((tm, D_MODEL), lambda i: (i, 0)),
        out_shape=jax.ShapeDtypeStruct((n, D_MODEL), F32),
        compiler_params=_params(1),
        name="mix_out",
    )(x2, a, o_f, o_b, z, w_out, mod)


def _mlp_kernel(x_ref, g_ref, mod_ref, w1_ref, w2_ref, o_ref):
    h = _norm_mod(x_ref[...], g_ref, mod_ref, 3, 4).astype(BF16)
    parts = []
    for k in range(0, D_FF, MLP_FF_CHUNK):
        a = jnp.maximum(jnp.dot(h, w1_ref[:, k:k + MLP_FF_CHUNK], preferred_element_type=F32), 0.0)
        parts.append((a * a).astype(BF16))
    y = jnp.dot(jnp.concatenate(parts, axis=1), w2_ref[...], preferred_element_type=F32)
    o_ref[...] = x_ref[...] + mod_ref[0, 5:6, :] * y


def _mlp_call(x2, g, mod, w1, w2, t):
    n = x2.shape[0]
    tm = _pick_tile(t, 512)
    tpb = t // tm
    once = pl.Buffered(1)
    return pl.pallas_call(
        _mlp_kernel,
        grid=(n // tm,),
        in_specs=[
            pl.BlockSpec((tm, D_MODEL), lambda i: (i, 0)),
            pl.BlockSpec((1, D_MODEL), lambda i: (0, 0)),
            pl.BlockSpec((1, ADA_CHUNKS, D_MODEL), lambda i: (i // tpb, 0, 0)),
            pl.BlockSpec((D_MODEL, D_FF), lambda i: (0, 0), pipeline_mode=once),
            pl.BlockSpec((D_FF, D_MODEL), lambda i: (0, 0), pipeline_mode=once),
        ],
        out_specs=pl.BlockSpec((tm, D_MODEL), lambda i: (i, 0)),
        out_shape=jax.ShapeDtypeStruct((n, D_MODEL), F32),
        compiler_params=_params(1),
        name="mlp",
    )(x2, g.reshape(1, D_MODEL), mod, w1, w2)


def _s5_compact_weights(p_f, p_b):
    st = lambda i: jnp.stack([p_f[i].astype(F32), p_b[i].astype(F32)])
    ar, ai, br, bi, cr, ci, log_dt = (st(i) for i in range(7))
    dt = jnp.exp(log_dt)[:, :, None]
    lr, li = dt * ar, dt * ai
    steps = np.arange(S5_L)

    def cexp(tau, x_r, x_i):
        mag = jnp.exp(tau * x_r)
        return mag * jnp.cos(tau * x_i), mag * jnp.sin(tau * x_i)

    def taus(table):
        return jnp.asarray(np.asarray(table, np.float32))[:, :, None, None, None]

    a1r, a1i = cexp(1.0, lr, li)
    nr, ni = a1r - 1.0, a1i
    den = ar * ar + ai * ai
    qr, qi = (nr * ar + ni * ai) / den, (ni * ar - nr * ai) / den
    bbr = qr[..., None] * br - qi[..., None] * bi
    bbi = qr[..., None] * bi + qi[..., None] * br

    slab = lambda x: x.reshape((2, N_SLAB, S5_GPS) + x.shape[2:])
    over_h = lambda x: jnp.repeat(x, S5_GROUP, axis=-1)
    lr_fp, li_fp = lr.reshape(2, N_SLAB, 1, S5_HALF), li.reshape(2, N_SLAB, 1, S5_HALF)
    lr_fh, li_fh = (over_h(slab(x).transpose(0, 1, 3, 2)) for x in (lr, li))
    cr_fh, ci_fh = (slab(x).transpose(0, 1, 4, 2, 3).reshape(2, N_SLAB, S5_STATE, LANES) for x in (cr, ci))
    bbr_fp, bbi_fp = (slab(x).transpose(0, 1, 4, 2, 3).reshape(2, N_SLAB, S5_GROUP, S5_HALF) for x in (bbr, bbi))
    bbr_fh, bbi_fh = (over_h(slab(x).transpose(0, 1, 4, 3, 2)) for x in (bbr, bbi))

    def c_pow(table):
        pr, pi = cexp(taus(table), lr_fh[:, None], li_fh[:, None])
        return cr_fh[:, None] * pr - ci_fh[:, None] * pi, cr_fh[:, None] * pi + ci_fh[:, None] * pr

    dr, di = c_pow([steps, steps])
    kern = jnp.sum(dr[:, :, :, None] * bbr_fh[:, None] - di[:, :, :, None] * bbi_fh[:, None], axis=4)
    lag_f = steps[None, :] - steps[:, None]
    toep = jnp.stack([kern[0][np.clip(lag_f, 0, S5_L - 1)], kern[1][np.clip(-lag_f, 0, S5_L - 1)]])
    valid = np.stack([lag_f >= 0, lag_f <= 0])[:, :, :, None, None, None]
    tt = jnp.where(valid, toep, 0.0).transpose(0, 3, 1, 4, 2, 5).reshape(2, N_SLAB, S5_L * S5_GROUP, S5_K)
    pr, pi = cexp(taus([S5_L - 1 - steps, steps]), lr_fp[:, None], li_fp[:, None])
    wr = pr * bbr_fp[:, None] - pi * bbi_fp[:, None]
    wi = pr * bbi_fp[:, None] + pi * bbr_fp[:, None]
    wt = jnp.stack([wr, wi], axis=4).transpose(0, 2, 1, 3, 4, 5).reshape(2, N_SLAB, S5_L * S5_GROUP, 2 * S5_HALF)
    vr, vi = c_pow([steps + 1, S5_L - steps])
    vt = jnp.stack([vr, -vi], axis=1).transpose(0, 3, 1, 4, 2, 5).reshape(2, N_SLAB, 2 * S5_STATE, S5_K)
    alr, ali = cexp(float(S5_L), lr, li)
    al = jnp.stack([alr[0], ali[0], alr[1], ali[1]]).reshape(4, N_SLAB, S5_HALF).transpose(1, 0, 2)
    coef = jnp.concatenate([al, jnp.zeros_like(al)], axis=1)
    return tt, wt, vt, coef


def _nm_s5_kernel(x_ref, g_ref, mod_ref, h_ref, u_ref, hs_s, *, nb, tk):
    x = x_ref[...]
    ms = jnp.mean(x * x, axis=-1, keepdims=True)
    y = x * lax.rsqrt(ms + EPS) * g_ref[...]
    h = y * (1.0 + mod_ref[:, 1:2, :]) + mod_ref[:, 0:1, :]
    h_ref[...] = h.astype(h_ref.dtype)
    pitch = tk + S5_PITCH_PAD
    for bi in range(nb):
        for j in range(N_SLAB):
            hs_s[j, bi * pitch:bi * pitch + tk, :] = h[bi, :, j * LANES:(j + 1) * LANES]
    for kk in range(tk // S5_L):
        for s in range(S5_L):
            for j in range(N_SLAB):
                blk = hs_s[j, pl.ds(kk * S5_L + s, nb, stride=pitch), :]
                u_ref[j, kk * nb:(kk + 1) * nb, s * LANES:(s + 1) * LANES] = blk.astype(u_ref.dtype)


def _norm_mod_s5_call(x3, g, mod):
    b, t, _ = x3.shape
    tk = S5_TOK
    rows = (tk // S5_L) * b
    return pl.pallas_call(
        functools.partial(_nm_s5_kernel, nb=b, tk=tk),
        grid=(t // tk,),
        in_specs=[
            pl.BlockSpec((b, tk, D_MODEL), lambda i: (0, i, 0)),
            pl.BlockSpec((1, D_MODEL), lambda i: (0, 0)),
            pl.BlockSpec((b, ADA_CHUNKS, D_MODEL), lambda i: (0, 0, 0)),
        ],
        out_specs=[
            pl.BlockSpec((b, tk, D_MODEL), lambda i: (0, i, 0)),
            pl.BlockSpec((N_SLAB, rows, S5_K), lambda i: (0, i, 0)),
        ],
        out_shape=[
            jax.ShapeDtypeStruct((b, t, D_MODEL), BF16),
            jax.ShapeDtypeStruct((N_SLAB, (t // S5_L) * b, S5_K), BF16),
        ],
        scratch_shapes=[pltpu.VMEM((N_SLAB, b * (tk + S5_PITCH_PAD), LANES), F32)],
        compiler_params=_params(1),
        name="norm_mod_s5",
    )(x3, g.reshape(1, D_MODEL), mod)


def _s5_kernel(uc_ref, ulf_ref, ulb_ref, tt_ref, wt_ref, vt_ref, coef_ref,
               ycf_ref, ycb_ref, ylf_ref, ylb_ref, t_s, w_s, v_s, u_s, s_s, x_s, carry_s, *, nb, nblk):
    i = pl.program_id(1)

    @pl.when(i == 0)
    def _():
        grp_out = (lax.broadcasted_iota(jnp.int32, (1, S5_K), 1) // S5_GROUP) % S5_GPS
        grp_state = (lax.broadcasted_iota(jnp.int32, (1, 2 * S5_HALF), 1) // S5_STATE) % S5_GPS
        rows = S5_GROUP
        for d in range(2):
            for s in range(S5_L):
                tc = tt_ref[d, 0, s * rows:(s + 1) * rows, :]
                wc = wt_ref[d, 0, s * rows:(s + 1) * rows, :]
                for gi in range(S5_GPS):
                    r0 = s * LANES + gi * rows
                    t_s[d, r0:r0 + rows, :] = jnp.where(grp_out == gi, tc, 0.0).astype(BF16)
                    w_s[d, r0:r0 + rows, :] = jnp.where(grp_state == gi, wc, 0.0).astype(BF16)
            for c in range(2):
                vc = vt_ref[d, 0, c * S5_STATE:(c + 1) * S5_STATE, :]
                for gi in range(S5_GPS):
                    r0 = c * S5_HALF + gi * S5_STATE
                    v_s[d, r0:r0 + S5_STATE, :] = jnp.where(grp_out == gi, vc, 0.0).astype(BF16)
        carry_s[...] = jnp.zeros_like(carry_s)
        u_s[0] = uc_ref[0]
        u_s[1] = uc_ref[0]

    @pl.when(i > 0)
    def _():
        u_s[0] = ulf_ref[0]
        u_s[1] = ulb_ref[0]

    for d in range(2):
        s_s[d] = jnp.dot(u_s[d], w_s[d], preferred_element_type=F32)
    ys = [jnp.dot(u_s[d], t_s[d], preferred_element_type=F32) for d in range(2)]
    for d in range(2):
        a_r = coef_ref[0, 2 * d:2 * d + 1, :]
        a_i = coef_ref[0, 2 * d + 1:2 * d + 2, :]
        x_r, x_i = carry_s[d, :, 0:S5_HALF], carry_s[d, :, S5_HALF:2 * S5_HALF]
        for k in range(nblk):
            rows = slice((k if d == 0 else nblk - 1 - k) * nb, (k if d == 0 else nblk - 1 - k) * nb + nb)
            x_s[d, rows, 0:S5_HALF] = x_r
            x_s[d, rows, S5_HALF:2 * S5_HALF] = x_i
            s_r = s_s[d, rows, 0:S5_HALF]
            s_i = s_s[d, rows, S5_HALF:2 * S5_HALF]
            x_r, x_i = a_r * x_r - a_i * x_i + s_r, a_r * x_i + a_i * x_r + s_i
        carry_s[d, :, 0:S5_HALF] = x_r
        carry_s[d, :, S5_HALF:2 * S5_HALF] = x_i
    for d in range(2):
        y = ys[d] + jnp.dot(x_s[d].astype(BF16), v_s[d], preferred_element_type=F32)
        ys[d] = y.astype(BF16)

    @pl.when(i == 0)
    def _():
        ycf_ref[0] = ys[0]
        ycb_ref[0] = ys[1]

    @pl.when(i > 0)
    def _():
        ylf_ref[0] = ys[0]
        ylb_ref[0] = ys[1]


def _s5_call(u_c, u_l, tt, wt, vt, coef, nb):
    rows = u_c.shape[1]
    n_lat = u_l.shape[1] // rows
    nblk = rows // nb
    tile = lambda fn: pl.BlockSpec((1, rows, S5_K), fn)
    wspec = lambda a: pl.BlockSpec((2, 1) + a.shape[2:], lambda j, i: (0, j, 0, 0))
    ctx_map = lambda j, i: (j, 0, 0)
    fwd_map = lambda j, i: (j, jnp.maximum(i - 1, 0), 0)
    bwd_map = lambda j, i: (j, jnp.minimum(n_lat - i, n_lat - 1), 0)
    yc = jax.ShapeDtypeStruct(u_c.shape, BF16)
    yl = jax.ShapeDtypeStruct(u_l.shape, BF16)
    return pl.pallas_call(
        functools.partial(_s5_kernel, nb=nb, nblk=nblk),
        grid=(N_SLAB, n_lat + 1),
        in_specs=[tile(ctx_map), tile(fwd_map), tile(bwd_map), wspec(tt), wspec(wt), wspec(vt),
                  pl.BlockSpec((1,) + coef.shape[1:], lambda j, i: (j, 0, 0))],
        out_specs=[tile(ctx_map), tile(ctx_map), tile(fwd_map), tile(bwd_map)],
        out_shape=[yc, yc, yl, yl],
        scratch_shapes=[
            pltpu.VMEM((2, S5_K, S5_K), BF16),
            pltpu.VMEM((2, S5_K, 2 * S5_HALF), BF16),
            pltpu.VMEM((2, 2 * S5_HALF, S5_K), BF16),
            pltpu.VMEM((2, rows, S5_K), BF16),
            pltpu.VMEM((2, rows, 2 * S5_HALF), F32),
            pltpu.VMEM((2, rows, 2 * S5_HALF), F32),
            pltpu.VMEM((2, nb, 2 * S5_HALF), F32),
        ],
        compiler_params=_params(2),
        name="s5_scan",
    )(u_c, u_l, u_l, tt, wt, vt, coef)


def _s5_out_kernel(x_ref, yf_ref, yb_ref, h_ref, d_ref, w_ref, mod_ref, o_ref, ys_s, *, nb, tk):
    pitch = tk + S5_PITCH_PAD
    for kk in range(tk // S5_L):
        rows = slice(kk * nb, (kk + 1) * nb)
        for s in range(S5_L):
            lanes = slice(s * LANES, (s + 1) * LANES)
            for j in range(N_SLAB):
                blk = yf_ref[j, rows, lanes].astype(F32) + yb_ref[j, rows, lanes].astype(F32)
                ys_s[j, pl.ds(kk * S5_L + s, nb, stride=pitch), :] = blk
    y = jnp.concatenate(
        [jnp.concatenate([ys_s[j, bi * pitch:bi * pitch + tk, :] for bi in range(nb)], axis=0) for j in range(N_SLAB)],
        axis=1)
    y = y + d_ref[...] * h_ref[...].reshape(nb * tk, D_MODEL).astype(F32)
    g = 0.5 * y * (1.0 + jnp.tanh(math.sqrt(2.0 / math.pi) * (y + 0.044715 * (y * y * y))))
    z = jnp.dot(g.astype(BF16), w_ref[...], preferred_element_type=F32)
    out = z[:, :D_MODEL] * _sigmoid(z[:, D_MODEL:])
    o_ref[...] = x_ref[...] + mod_ref[:, 2:3, :] * out.reshape(nb, tk, D_MODEL)


def _s5_out_call(x3, y_f, y_b, h3, d_skip, w_glu, mod):
    b, t, _ = x3.shape
    tk = S5_TOK
    rows = (tk // S5_L) * b
    tile = pl.BlockSpec((b, tk, D_MODEL), lambda i: (0, i, 0))
    ytile = pl.BlockSpec((N_SLAB, rows, S5_K), lambda i: (0, i, 0))
    return pl.pallas_call(
        functools.partial(_s5_out_kernel, nb=b, tk=tk),
        grid=(t // tk,),
        in_specs=[
            tile, ytile, ytile, tile,
            pl.BlockSpec((1, D_MODEL), lambda i: (0, 0)),
            pl.BlockSpec(w_glu.shape, lambda i: (0, 0)),
            pl.BlockSpec((b, ADA_CHUNKS, D_MODEL), lambda i: (0, 0, 0)),
        ],
        out_specs=tile,
        out_shape=jax.ShapeDtypeStruct((b, t, D_MODEL), F32),
        scratch_shapes=[pltpu.VMEM((N_SLAB, b * (tk + S5_PITCH_PAD), LANES), F32)],
        compiler_params=_params(1),
        name="s5_out",
    )(x3, y_f, y_b, h3, d_skip.reshape(1, D_MODEL).astype(F32), w_glu, mod)


def _even_layer(x2, xc2, mod_l, mod_c, b, t, tc, need_ctx, p):
    n_c = b * tc
    z_c = _norm_mod_matmul_call(xc2, p['norm1_g'], mod_c, n_c, p['w_in'])
    z_l = _norm_mod_matmul_call(x2, p['norm1_g'], mod_l, t, p['w_in'], tables=_ret_tables(t))
    q_c, k_c, vt_c = _mla_prep_call(z_c, n_c, p['mla'], *_mla_tables(n_c, False))
    q_l, k_l, vt_l = _mla_prep_call(z_l, t, p['mla'], *_mla_tables(t, True))
    a_l = _attn_call(q_l, [k_c, k_l], [vt_c, vt_l], t, [tc, t])
    zero = jnp.zeros((b, N_PAIR, LANES, LANES), F32)
    of_c, ob_c, s_cf, s_cb = _ret_call(z_c, p['lg'], zero, zero, b, tc)
    of_l, ob_l, _, _ = _ret_call(z_l, p['lg'], s_cf, s_cb, b, t)
    x2 = _mix_out_call(x2, a_l, of_l, ob_l, z_l, p['w_out'], mod_l, t)
    if need_ctx:
        a_c = _attn_call(q_c, [k_c], [vt_c], tc, [tc])
        xc2 = _mix_out_call(xc2, a_c, of_c, ob_c, z_c, p['w_out'], mod_c, n_c)
    return x2, xc2


def _odd_layer(x2, xc2, mod_l, mod_c, b, t, tc, need_ctx, p):
    x3, xc3 = x2.reshape(b, t, D_MODEL), xc2.reshape(b, tc, D_MODEL)
    h_c, u_c = _norm_mod_s5_call(xc3, p['norm1_g'], mod_c)
    h_l, u_l = _norm_mod_s5_call(x3, p['norm1_g'], mod_l)
    tt, wt, vt, coef = p['s5']
    ycf, ycb, ylf, ylb = _s5_call(u_c, u_l, tt, wt, vt, coef, b)
    x2 = _s5_out_call(x3, ylf, ylb, h_l, p['d_skip'], p['w_glu'], mod_l).reshape(b * t, D_MODEL)
    if need_ctx:
        xc2 = _s5_out_call(xc3, ycf, ycb, h_c, p['d_skip'], p['w_glu'], mod_c).reshape(b * tc, D_MODEL)
    return x2, xc2


def kernel(x, c, ctx, c_ctx, ada_w, ada_b, norm1_g, norm2_g, mlp_w1, mlp_w2, w_in, mla_q_norm_g, mla_w_uq, mla_kv_norm_g, mla_w_ukv, mla_qn_g, mla_kn_g, ret_lg_f, ret_lg_b, w_out, s5_a_re_f, s5_a_im_f, s5_b_re_f, s5_b_im_f, s5_c_re_f, s5_c_im_f, s5_log_dt_f, s5_a_re_b, s5_a_im_b, s5_b_re_b, s5_b_im_b, s5_c_re_b, s5_c_im_b, s5_log_dt_b, s5_d, s5_w_glu):
    b, t, _ = x.shape
    tc = ctx.shape[1]
    depth = ada_w.shape[0]
    assert b % 8 == 0 and tc % KEY_CHUNK == 0 and t % tc == 0 and tc % S5_TOK == 0
    rows = -(-(b + 1) // 8) * 8
    cc = jnp.zeros((rows, D_MODEL), F32).at[:b].set(c.astype(F32)).at[b].set(c_ctx.astype(F32))
    mod = _ada_all(cc, ada_w.astype(F32), ada_b.astype(F32))
    x2 = x.reshape(b * t, D_MODEL).astype(F32)
    xc2 = ctx.reshape(b * tc, D_MODEL).astype(F32)
    w1_all, w2_all = mlp_w1.astype(BF16), mlp_w2.astype(BF16)
    w_in_all = jnp.concatenate([w_in[:, :, :MLA_IN], jnp.zeros(w_in.shape[:2] + (MLA_IN_PAD - MLA_IN,), w_in.dtype),
                                w_in[:, :, MLA_IN:]], axis=2).astype(BF16)
    w_out_all, w_glu_all = w_out.astype(BF16), s5_w_glu.astype(BF16)
    mla_all = jax.vmap(_mla_weights)(mla_q_norm_g, mla_w_uq, mla_kv_norm_g, mla_w_ukv, mla_qn_g, mla_kn_g)
    lg_all = jnp.stack([jnp.log1p(-jnp.exp2(ret_lg_f.astype(F32))), jnp.log1p(-jnp.exp2(ret_lg_b.astype(F32)))], axis=1)
    s5_all = jax.vmap(_s5_compact_weights)(
        (s5_a_re_f, s5_a_im_f, s5_b_re_f, s5_b_im_f, s5_c_re_f, s5_c_im_f, s5_log_dt_f),
        (s5_a_re_b, s5_a_im_b, s5_b_re_b, s5_b_im_b, s5_c_re_b, s5_c_im_b, s5_log_dt_b))
    for l in range(depth):
        need_ctx = l < depth - 1
        mod_l = mod[l, :b].reshape(b, ADA_CHUNKS, D_MODEL)
        mod_c = jnp.broadcast_to(mod[l, b].reshape(1, ADA_CHUNKS, D_MODEL), (b, ADA_CHUNKS, D_MODEL))
        if l % 2 == 0:
            e = l // 2
            p = dict(norm1_g=norm1_g[l], w_in=w_in_all[e], mla=[a[e] for a in mla_all], lg=lg_all[e], w_out=w_out_all[e])
            x2, xc2 = _even_layer(x2, xc2, mod_l, mod_c, b, t, tc, need_ctx, p)
        else:
            o = l // 2
            p = dict(norm1_g=norm1_g[l], d_skip=s5_d[o], w_glu=w_glu_all[o], s5=[a[o] for a in s5_all])
            x2, xc2 = _odd_layer(x2, xc2, mod_l, mod_c, b, t, tc, need_ctx, p)
        x2 = _mlp_call(x2, norm2_g[l], mod_l, w1_all[l], w2_all[l], t)
        if need_ctx:
            xc2 = _mlp_call(xc2, norm2_g[l], mod_c, w1_all[l], w2_all[l], b * tc)
    return x2.reshape(b, t, D_MODEL).astype(x.dtype)
```

```python
import functools
import math

import jax
import jax.numpy as jnp
import numpy as np
from jax import lax
from jax.experimental import pallas as pl
from jax.experimental.pallas import tpu as pltpu

F32 = jnp.float32
BF16 = jnp.bfloat16

D_MODEL = 1024
EPS = 1e-6
ADA_CHUNKS = 6
GRID_W = 64
ROPE_BASE = 10000.0
LANES = 128
N_SLAB = D_MODEL // LANES

H_A = 8
Q_LORA = 256
KV_LORA = 128
NOPE_DIM = 64
ROPE_DIM = 32
QK_DIM = NOPE_DIM + ROPE_DIM
V_DIM = 64
HEAD_SLOT = LANES
MLA_IN = Q_LORA + KV_LORA + ROPE_DIM
MLA_IN_PAD = 512
KEY_CHUNK = 256
SUB_ROWS = 64

H_R = 8
DK_R = 64
DV_R = 64
RET_W = H_R * DK_R
N_PAIR = H_R // 2

S5_GROUP = 16
S5_GROUPS = D_MODEL // S5_GROUP
S5_STATE = 64
S5_L = 4
S5_K = S5_L * LANES
S5_GPS = LANES // S5_GROUP
S5_HALF = S5_GPS * S5_STATE
S5_TOK = 32
S5_PITCH_PAD = 8

D_FF = 4 * D_MODEL
MLP_FF_CHUNK = 512
Z_WIDTH = MLA_IN_PAD + 4 * RET_W

VMEM_LIMIT = 56 * 1024 * 1024

_NT = (((1,), (1,)), ((), ()))
_HP = lax.Precision.HIGHEST


def _params(n_grid):
    return pltpu.CompilerParams(
        dimension_semantics=("arbitrary",) * n_grid, vmem_limit_bytes=VMEM_LIMIT)


def _sigmoid(x):
    return 1.0 / (1.0 + jnp.exp(-x))


def _pick_tile(t, pref):
    tile = min(t, pref)
    while t % tile:
        tile //= 2
    return tile


def _ada_kernel(c_ref, w_ref, b_ref, o_ref):
    cc = c_ref[...]
    s = cc * _sigmoid(cc)
    o_ref[0] = jnp.dot(s, w_ref[0], preferred_element_type=F32, precision=_HP) + b_ref[0]


def _ada_all(cc, ada_w, ada_b):
    depth, _, width = ada_w.shape
    r = cc.shape[0]
    tn = 1536
    return pl.pallas_call(
        _ada_kernel,
        grid=(depth, width // tn),
        in_specs=[
            pl.BlockSpec((r, D_MODEL), lambda l, j: (0, 0)),
            pl.BlockSpec((1, D_MODEL, tn), lambda l, j: (l, 0, j)),
            pl.BlockSpec((1, 1, tn), lambda l, j: (l, 0, j)),
        ],
        out_specs=pl.BlockSpec((1, r, tn), lambda l, j: (l, 0, j)),
        out_shape=jax.ShapeDtypeStruct((depth, r, width), F32),
        compiler_params=_params(2),
        name="ada_mod",
    )(cc, ada_w, ada_b.reshape(depth, 1, width))


def _norm_mod(x, g_ref, mod_ref, shift_row, scale_row):
    ms = jnp.mean(x * x, axis=-1, keepdims=True)
    y = x * lax.rsqrt(ms + EPS) * g_ref[...]
    return y * (1.0 + mod_ref[0, scale_row:scale_row + 1, :]) + mod_ref[0, shift_row:shift_row + 1, :]


def _rotate_pairs(x, cos, sin_lo, sin_hi):
    return x * cos + pltpu.roll(x, LANES - DK_R // 2, 1) * sin_lo + pltpu.roll(x, DK_R // 2, 1) * sin_hi


def _nmm_kernel(x_ref, g_ref, mod_ref, w_ref, *rest, rotary):
    o_ref = rest[-1]
    h = _norm_mod(x_ref[...], g_ref, mod_ref, 0, 1).astype(BF16)
    z = jnp.dot(h, w_ref[...], preferred_element_type=F32)
    o_ref[:, 0:MLA_IN_PAD] = z[:, 0:MLA_IN_PAD].astype(o_ref.dtype)
    for blk in range(2 * N_PAIR):
        cols = slice(MLA_IN_PAD + blk * LANES, MLA_IN_PAD + (blk + 1) * LANES)
        v = z[:, cols]
        if rotary:
            v = _rotate_pairs(v, rest[0][...], rest[1][...], rest[2][...])
        if blk >= N_PAIR:
            v = v * (DK_R ** -0.5)
        o_ref[:, cols] = v.astype(o_ref.dtype)
    tail = MLA_IN_PAD + 2 * RET_W
    o_ref[:, tail:] = z[:, tail:].astype(o_ref.dtype)


def _layer_spec(w_all, layer, **kw):
    return pl.BlockSpec((None,) + w_all.shape[1:], lambda i: (layer, 0, 0), **kw)


def _norm_mod_matmul_call(x2, g, mod, t, w, layer, tables=None):
    n = x2.shape[0]
    tm = _pick_tile(t, 512)
    tpb = t // tm
    n_out = w.shape[-1]
    in_specs = [
        pl.BlockSpec((tm, D_MODEL), lambda i: (i, 0)),
        pl.BlockSpec((1, D_MODEL), lambda i: (0, 0)),
        pl.BlockSpec((1, ADA_CHUNKS, D_MODEL), lambda i: (i // tpb, 0, 0)),
        _layer_spec(w, layer),
    ]
    args = [x2, g.reshape(1, D_MODEL), mod, w]
    if tables is not None:
        in_specs += [pl.BlockSpec((tm, LANES), lambda i: (i % tpb, 0))] * 3
        args += list(tables)
    return pl.pallas_call(
        functools.partial(_nmm_kernel, rotary=tables is not None),
        grid=(n // tm,),
        in_specs=in_specs,
        out_specs=pl.BlockSpec((tm, n_out), lambda i: (i, 0)),
        out_shape=jax.ShapeDtypeStruct((n, n_out), BF16),
        compiler_params=_params(1),
        name="norm_mod_w_in",
    )(*args)


def _mla_prep_kernel(z_ref, gq_ref, gkv_ref, wq_ref, wqs_ref, wkc_ref, wkcs_ref, wkr_ref, wkrs_ref,
                     wvt_ref, gqn_ref, gqns_ref, gkn_ref, gkns_ref, cos_ref, sin_ref,
                     q_ref, k_ref, vt_ref):
    cq = z_ref[:, 0:Q_LORA].astype(F32)
    ckv = z_ref[:, Q_LORA:Q_LORA + KV_LORA].astype(F32)
    kr = z_ref[:, Q_LORA + KV_LORA:MLA_IN_PAD]
    cqn = (cq * lax.rsqrt(jnp.mean(cq * cq, axis=-1, keepdims=True) + EPS) * gq_ref[...]).astype(BF16)
    ckn = (ckv * lax.rsqrt(jnp.mean(ckv * ckv, axis=-1, keepdims=True) + EPS) * gkv_ref[...]).astype(BF16)
    q = jnp.dot(cqn, wq_ref[...], preferred_element_type=F32)
    qs = jnp.dot(cqn, wqs_ref[...], preferred_element_type=F32)
    k = jnp.dot(ckn, wkc_ref[...], preferred_element_type=F32) + jnp.dot(kr, wkr_ref[...], preferred_element_type=F32)
    ks = jnp.dot(ckn, wkcs_ref[...], preferred_element_type=F32) + jnp.dot(kr, wkrs_ref[...], preferred_element_type=F32)
    vt_ref[...] = lax.dot_general(wvt_ref[...], ckn, _NT, preferred_element_type=F32).astype(vt_ref.dtype)
    cos = cos_ref[...]
    sin = sin_ref[...]
    q_scale = QK_DIM ** -0.5 * math.log2(math.e)
    for h in range(H_A):
        sl = slice(h * HEAD_SLOT, (h + 1) * HEAD_SLOT)
        qh = q[:, sl]
        rq = lax.rsqrt(jnp.sum(qh * qh, axis=-1, keepdims=True) * (1.0 / QK_DIM) + EPS)
        q_rot = (qh * gqn_ref[...] * cos + qs[:, sl] * gqns_ref[...] * sin) * (rq * q_scale)
        q_ref[:, sl] = q_rot.astype(q_ref.dtype)
        kh = k[:, sl]
        rk = lax.rsqrt(jnp.sum(kh * kh, axis=-1, keepdims=True) * (1.0 / QK_DIM) + EPS)
        k_rot = (kh * gkn_ref[...] * cos + ks[:, sl] * gkns_ref[...] * sin) * rk
        k_ref[:, sl] = k_rot.astype(k_ref.dtype)


def _mla_prep_call(z, t, wts, cos_t, sin_t):
    n = z.shape[0]
    tm = _pick_tile(t, 512)
    tpb = t // tm
    full = lambda a: pl.BlockSpec(a.shape, lambda i: (0,) * a.ndim)
    in_specs = [pl.BlockSpec((tm, MLA_IN_PAD), lambda i: (i, 0))] + [full(a) for a in wts] + [
        pl.BlockSpec((tm, HEAD_SLOT), lambda i: (i % tpb, 0)),
        pl.BlockSpec((tm, HEAD_SLOT), lambda i: (i % tpb, 0)),
    ]
    hw = H_A * HEAD_SLOT
    vw = H_A * V_DIM
    return pl.pallas_call(
        _mla_prep_kernel,
        grid=(n // tm,),
        in_specs=in_specs,
        out_specs=[
            pl.BlockSpec((tm, hw), lambda i: (i, 0)),
            pl.BlockSpec((tm, hw), lambda i: (i, 0)),
            pl.BlockSpec((vw, tm), lambda i: (0, i)),
        ],
        out_shape=[
            jax.ShapeDtypeStruct((n, hw), BF16),
            jax.ShapeDtypeStruct((n, hw), BF16),
            jax.ShapeDtypeStruct((vw, n), BF16),
        ],
        compiler_params=_params(1),
        name="mla_prep",
    )(z, *wts, cos_t, sin_t)


def _mla_weights(q_norm_g, w_uq, kv_norm_g, w_ukv, qn_g, kn_g):
    pad = HEAD_SLOT - QK_DIM
    half = ROPE_DIM // 2
    perm = jnp.arange(HEAD_SLOT)
    perm = perm.at[NOPE_DIM:NOPE_DIM + half].set(jnp.arange(NOPE_DIM + half, NOPE_DIM + ROPE_DIM))
    perm = perm.at[NOPE_DIM + half:NOPE_DIM + ROPE_DIM].set(jnp.arange(NOPE_DIM, NOPE_DIM + half))

    def slots(w):
        wp = jnp.pad(w, ((0, 0), (0, 0), (0, pad)))
        return wp, wp[:, :, perm]

    def flat(w):
        return w.reshape(w.shape[0], H_A * HEAD_SLOT).astype(BF16)

    wq, wqs = slots(w_uq.reshape(Q_LORA, H_A, QK_DIM))
    w_kv = w_ukv.reshape(KV_LORA, H_A, NOPE_DIM + V_DIM)
    wkc, wkcs = slots(jnp.pad(w_kv[:, :, :NOPE_DIM], ((0, 0), (0, 0), (0, ROPE_DIM))))
    eye = jnp.zeros((HEAD_SLOT, QK_DIM), F32).at[jnp.arange(ROPE_DIM), NOPE_DIM + jnp.arange(ROPE_DIM)].set(1.0)
    wkr, wkrs = slots(jnp.broadcast_to(eye[:, None, :], (HEAD_SLOT, H_A, QK_DIM)))
    wvt = w_kv[:, :, NOPE_DIM:].reshape(KV_LORA, H_A * V_DIM).T.astype(BF16)

    def gains(g):
        gp = jnp.pad(g.astype(F32), (0, pad))
        return gp.reshape(1, HEAD_SLOT), gp[perm].reshape(1, HEAD_SLOT)

    gqn, gqns = gains(qn_g)
    gkn, gkns = gains(kn_g)
    return [q_norm_g.reshape(1, Q_LORA).astype(F32), kv_norm_g.reshape(1, KV_LORA).astype(F32),
            flat(wq), flat(wqs), flat(wkc), flat(wkcs), flat(wkr), flat(wkrs), wvt, gqn, gqns, gkn, gkns]


def _mla_tables(n_tok, rotary):
    cos = jnp.ones((n_tok, HEAD_SLOT), F32)
    sin = jnp.zeros((n_tok, HEAD_SLOT), F32)
    if rotary:
        rows = n_tok // GRID_W
        r = jnp.repeat(jnp.arange(rows, dtype=F32), GRID_W)
        col = jnp.tile(jnp.arange(GRID_W, dtype=F32), rows)
        nf = ROPE_DIM // 4
        f = ROPE_BASE ** (-jnp.arange(nf, dtype=F32) / nf)
        ang = jnp.concatenate([r[:, None] * f, col[:, None] * f], axis=-1)
        c, s = jnp.cos(ang), jnp.sin(ang)
        half = ROPE_DIM // 2
        cos = cos.at[:, NOPE_DIM:NOPE_DIM + half].set(c).at[:, NOPE_DIM + half:NOPE_DIM + ROPE_DIM].set(c)
        sin = sin.at[:, NOPE_DIM:NOPE_DIM + half].set(-s).at[:, NOPE_DIM + half:NOPE_DIM + ROPE_DIM].set(s)
    return cos, sin


def _attn_kernel(*refs, seg_lens):
    nseg = len(seg_lens)
    q_ref = refs[0]
    k_refs = refs[1:1 + nseg]
    vt_refs = refs[1 + nseg:1 + 2 * nseg]
    o_ref = refs[1 + 2 * nseg]
    s_s = refs[2 + 2 * nseg]
    chunks = [(g, r) for g, n in enumerate(seg_lens) for r in range(0, n, KEY_CHUNK)]
    tq = q_ref.shape[0]

    def fold(x, op):
        n = x.shape[0] // 8
        x = x.reshape(n, 8, tq)
        out = x[0]
        for i in range(1, n):
            out = op(out, x[i])
        return out

    def scores(h):
        sl = slice(h * HEAD_SLOT, (h + 1) * HEAD_SLOT)
        qh = q_ref[:, sl]
        mx = None
        for c, (g, r) in enumerate(chunks):
            s = lax.dot_general(k_refs[g][r:r + KEY_CHUNK, sl], qh, _NT, preferred_element_type=F32)
            s_s[h % 2, c] = s
            cm = fold(s, jnp.maximum)
            mx = cm if mx is None else jnp.maximum(mx, cm)
        return jnp.max(mx, axis=0, keepdims=True)

    def weighted_values(h, m):
        rows = slice(h * V_DIM, (h + 1) * V_DIM)
        lsum = None
        o_t = None
        for c, (g, r) in enumerate(chunks):
            pieces = []
            for r0 in range(0, KEY_CHUNK, SUB_ROWS):
                p = jnp.exp2(s_s[h % 2, c, r0:r0 + SUB_ROWS, :] - m)
                ps = fold(p, jnp.add)
                lsum = ps if lsum is None else lsum + ps
                pieces.append(p.astype(BF16))
            pb = jnp.concatenate(pieces, axis=0)
            t = jnp.dot(vt_refs[g][rows, r:r + KEY_CHUNK], pb, preferred_element_type=F32)
            o_t = t if o_t is None else o_t + t
        return o_t * (1.0 / jnp.sum(lsum, axis=0, keepdims=True))

    outs = []
    m_next = scores(0)
    for h in range(H_A):
        m_cur = m_next
        if h + 1 < H_A:
            m_next = scores(h + 1)
        outs.append(weighted_values(h, m_cur))
        if h % 2 == 1:
            j = h // 2
            o_ref[:, j * LANES:(j + 1) * LANES] = jnp.concatenate(outs[-2:], axis=0).T.astype(o_ref.dtype)


def _attn_call(q, ks, vts, t, tks):
    n = q.shape[0]
    tq = _pick_tile(t, 256)
    tpb = t // tq
    hw = H_A * HEAD_SLOT
    vw = H_A * V_DIM
    n_chunks = sum(tk // KEY_CHUNK for tk in tks)
    in_specs = [pl.BlockSpec((tq, hw), lambda i: (i, 0))]
    in_specs += [pl.BlockSpec((tk, hw), lambda i: (i // tpb, 0)) for tk in tks]
    in_specs += [pl.BlockSpec((vw, tk), lambda i: (0, i // tpb)) for tk in tks]
    return pl.pallas_call(
        functools.partial(_attn_kernel, seg_lens=tuple(tks)),
        grid=(n // tq,),
        in_specs=in_specs,
        out_specs=pl.BlockSpec((tq, vw), lambda i: (i, 0)),
        out_shape=jax.ShapeDtypeStruct((n, vw), BF16),
        scratch_shapes=[pltpu.VMEM((2, n_chunks, KEY_CHUNK, tq), F32)],
        compiler_params=_params(1),
        name="attention_%dseg" % len(tks),
    )(q, *ks, *vts)


def _ret_kernel(lg_ref, qf_ref, kf_ref, vf_ref, qb_ref, kb_ref, vb_ref, s0f_ref, s0b_ref,
                of_ref, ob_ref, sff_ref, sfb_ref, df_s, db_s, sf_s, sb_s, *, chunk):
    b = pl.program_id(0)
    c = pl.program_id(1)
    nc = pl.num_programs(1)

    @pl.when(jnp.logical_and(b == 0, c == 0))
    def _():
        ii = lax.broadcasted_iota(jnp.int32, (chunk, chunk), 0)
        jj = lax.broadcasted_iota(jnp.int32, (chunk, chunk), 1)
        diff = (ii - jj).astype(F32)
        for h in range(H_R):
            df_s[h] = jnp.where(ii >= jj, jnp.exp(jnp.where(ii >= jj, diff, 0.0) * lg_ref[0, h]), 0.0)
            db_s[h] = jnp.where(jj > ii, jnp.exp(jnp.where(jj > ii, -diff, 0.0) * lg_ref[1, h]), 0.0)

    @pl.when(c == 0)
    def _():
        sf_s[...] = s0f_ref[0]
        sb_s[...] = s0b_ref[0]

    lane = lax.broadcasted_iota(jnp.int32, (1, LANES), 1)
    lo = lane < DK_R
    row = lax.broadcasted_iota(jnp.int32, (LANES, LANES), 0)
    colm = lax.broadcasted_iota(jnp.int32, (LANES, LANES), 1)
    blockdiag = (row < DK_R) == (colm < DK_R)
    pos = lax.broadcasted_iota(jnp.int32, (chunk, 1), 0).astype(F32)
    dirs = ((qf_ref, kf_ref, vf_ref, df_s, sf_s, of_ref), (qb_ref, kb_ref, vb_ref, db_s, sb_s, ob_ref))
    units = [(d, j) for d in range(2) for j in range(N_PAIR)]

    ops = {}
    for d, j in units:
        q_ref, k_ref, v_ref = dirs[d][:3]
        sl = slice(j * LANES, (j + 1) * LANES)
        lg = jnp.where(lo, lg_ref[d, 2 * j], lg_ref[d, 2 * j + 1])
        q2 = q_ref[:, sl].astype(F32)
        k2 = k_ref[:, sl].astype(F32)
        if d == 0:
            q_dec = jnp.exp((pos + 1.0) * lg)
            k_dec = jnp.exp((chunk - 1.0 - pos) * lg)
        else:
            q_dec = jnp.exp((chunk - pos) * lg)
            k_dec = jnp.exp(pos * lg)
        ops[d, j] = dict(
            sl=sl, v2=v_ref[:, sl], c_dec=jnp.exp(chunk * lg), k2b=k2.astype(BF16),
            q_dec=(q2 * q_dec).astype(BF16), kd_t=(k2 * k_dec).T.astype(BF16),
            q_lo=jnp.where(lo, q2, 0.0).astype(BF16), q_hi=jnp.where(lo, 0.0, q2).astype(BF16))
    for d, j in units:
        u = ops[d, j]
        u['s2'] = dirs[d][4][j]
        u['o'] = jnp.dot(u['q_dec'], u['s2'].astype(BF16), preferred_element_type=F32)
        u['sc'] = [lax.dot_general(u[name], u['k2b'], _NT, preferred_element_type=F32) for name in ('q_lo', 'q_hi')]
    for d, j in units:
        u = ops[d, j]
        d_s, o_ref = dirs[d][3], dirs[d][5]
        o = u['o']
        for e in range(2):
            ve = jnp.where(lo if e == 0 else jnp.logical_not(lo), u['v2'], jnp.zeros((), BF16))
            o = o + jnp.dot((u['sc'][e] * d_s[2 * j + e]).astype(BF16), ve, preferred_element_type=F32)
        o_ref[:, u['sl']] = o.astype(o_ref.dtype)
    for d, j in units:
        u = ops[d, j]
        upd = jnp.dot(u['kd_t'], u['v2'], preferred_element_type=F32)
        dirs[d][4][j] = u['s2'] * u['c_dec'] + jnp.where(blockdiag, upd, 0.0)

    @pl.when(c == nc - 1)
    def _():
        sff_ref[0] = sf_s[...]
        sfb_ref[0] = sb_s[...]


def _ret_call(z, lg, s0f, s0b, b, t):
    n = z.shape[0]
    chunk = _pick_tile(t, 256)
    nc = t // chunk
    col0 = MLA_IN_PAD // RET_W

    def zspec(part, rev):
        if rev:
            return pl.BlockSpec((chunk, RET_W), lambda bi, ci: (bi * nc + nc - 1 - ci, col0 + part))
        return pl.BlockSpec((chunk, RET_W), lambda bi, ci: (bi * nc + ci, col0 + part))

    st_spec = pl.BlockSpec((1, N_PAIR, LANES, LANES), lambda bi, ci: (bi, 0, 0, 0))
    in_specs = [pl.BlockSpec(memory_space=pltpu.SMEM)]
    in_specs += [zspec(0, False), zspec(1, False), zspec(2, False)]
    in_specs += [zspec(0, True), zspec(1, True), zspec(2, True)]
    in_specs += [st_spec, st_spec]
    out_specs = [
        pl.BlockSpec((chunk, RET_W), lambda bi, ci: (bi * nc + ci, 0)),
        pl.BlockSpec((chunk, RET_W), lambda bi, ci: (bi * nc + nc - 1 - ci, 0)),
        st_spec, st_spec,
    ]
    st_shape = jax.ShapeDtypeStruct((b, N_PAIR, LANES, LANES), F32)
    return pl.pallas_call(
        functools.partial(_ret_kernel, chunk=chunk),
        grid=(b, nc),
        in_specs=in_specs,
        out_specs=out_specs,
        out_shape=[jax.ShapeDtypeStruct((n, RET_W), BF16), jax.ShapeDtypeStruct((n, RET_W), BF16),
                   st_shape, st_shape],
        scratch_shapes=[
            pltpu.VMEM((H_R, chunk, chunk), F32), pltpu.VMEM((H_R, chunk, chunk), F32),
            pltpu.VMEM((N_PAIR, LANES, LANES), F32), pltpu.VMEM((N_PAIR, LANES, LANES), F32),
        ],
        compiler_params=_params(2),
        name="retention",
    )(lg, z, z, z, z, z, z, s0f, s0b)


def _ret_tables(n_tok):
    nf = DK_R // 2
    theta = ROPE_BASE ** (-jnp.arange(nf, dtype=F32) / nf)
    ang = jnp.arange(n_tok, dtype=F32)[:, None] * theta
    c, s = jnp.cos(ang), jnp.sin(ang)
    z = jnp.zeros_like(s)
    return (jnp.concatenate([c, c, c, c], axis=-1), jnp.concatenate([-s, z, -s, z], axis=-1),
            jnp.concatenate([z, s, z, s], axis=-1))


def _mix_out_kernel(x_ref, a_ref, of_ref, ob_ref, g_ref, w_ref, mod_ref, o_ref):
    lane = lax.broadcasted_iota(jnp.int32, (1, LANES), 1)
    lo = lane < DV_R
    y = jnp.dot(a_ref[...], w_ref[0:H_A * V_DIM, :], preferred_element_type=F32)
    for j in range(N_PAIR):
        sl = slice(j * LANES, (j + 1) * LANES)
        o = of_ref[:, sl].astype(F32) + ob_ref[:, sl].astype(F32)
        s_lo = jnp.sum(jnp.where(lo, o, 0.0), axis=-1, keepdims=True)
        s_all = jnp.sum(o, axis=-1, keepdims=True)
        mu = jnp.where(lo, s_lo, s_all - s_lo) * (1.0 / DV_R)
        oc = o - mu
        q = oc * oc
        q_lo = jnp.sum(jnp.where(lo, q, 0.0), axis=-1, keepdims=True)
        q_all = jnp.sum(q, axis=-1, keepdims=True)
        var = jnp.where(lo, q_lo, q_all - q_lo) * (1.0 / DV_R)
        g = g_ref[:, sl].astype(F32)
        r = (oc * lax.rsqrt(var + EPS)) * (g * _sigmoid(g))
        row0 = H_A * V_DIM + j * LANES
        y = y + jnp.dot(r.astype(BF16), w_ref[row0:row0 + LANES, :], preferred_element_type=F32)
    o_ref[...] = x_ref[...] + mod_ref[0, 2:3, :] * y


def _mix_out_call(x2, a, o_f, o_b, z, w_out, layer, mod, t):
    n = x2.shape[0]
    tm = _pick_tile(t, 512)
    tpb = t // tm
    gate_col = MLA_IN_PAD // RET_W + 3
    return pl.pallas_call(
        _mix_out_kernel,
        grid=(n // tm,),
        in_specs=[
            pl.BlockSpec((tm, D_MODEL), lambda i: (i, 0)),
            pl.BlockSpec((tm, H_A * V_DIM), lambda i: (i, 0)),
            pl.BlockSpec((tm, RET_W), lambda i: (i, 0)),
            pl.BlockSpec((tm, RET_W), lambda i: (i, 0)),
            pl.BlockSpec((tm, RET_W), lambda i: (i, gate_col)),
            _layer_spec(w_out, layer),
            pl.BlockSpec((1, ADA_CHUNKS, D_MODEL), lambda i: (i // tpb, 0, 0)),
        ],
        out_specs=pl.BlockSpec((tm, D_MODEL), lambda i: (i, 0)),
        out_shape=jax.ShapeDtypeStruct((n, D_MODEL), F32),
        compiler_params=_params(1),
        name="mix_out",
    )(x2, a, o_f, o_b, z, w_out, mod)


def _mlp_kernel(x_ref, g_ref, mod_ref, w1_ref, w2_ref, o_ref):
    h = _norm_mod(x_ref[...], g_ref, mod_ref, 3, 4).astype(BF16)
    parts = []
    for k in range(0, D_FF, MLP_FF_CHUNK):
        a = jnp.maximum(jnp.dot(h, w1_ref[:, k:k + MLP_FF_CHUNK], preferred_element_type=F32), 0.0)
        parts.append((a * a).astype(BF16))
    y = jnp.dot(jnp.concatenate(parts, axis=1), w2_ref[...], preferred_element_type=F32)
    o_ref[...] = x_ref[...] + mod_ref[0, 5:6, :] * y


def _mlp_call(x2, g, mod, w1, w2, layer, t):
    n = x2.shape[0]
    tm = _pick_tile(t, 512)
    tpb = t // tm
    once = pl.Buffered(1)
    return pl.pallas_call(
        _mlp_kernel,
        grid=(n // tm,),
        in_specs=[
            pl.BlockSpec((tm, D_MODEL), lambda i: (i, 0)),
            pl.BlockSpec((1, D_MODEL), lambda i: (0, 0)),
            pl.BlockSpec((1, ADA_CHUNKS, D_MODEL), lambda i: (i // tpb, 0, 0)),
            _layer_spec(w1, layer, pipeline_mode=once),
            _layer_spec(w2, layer, pipeline_mode=once),
        ],
        out_specs=pl.BlockSpec((tm, D_MODEL), lambda i: (i, 0)),
        out_shape=jax.ShapeDtypeStruct((n, D_MODEL), F32),
        compiler_params=_params(1),
        name="mlp",
    )(x2, g.reshape(1, D_MODEL), mod, w1, w2)


def _s5_compact_weights(p_f, p_b):
    st = lambda i: jnp.stack([p_f[i].astype(F32), p_b[i].astype(F32)])
    ar, ai, br, bi, cr, ci, log_dt = (st(i) for i in range(7))
    dt = jnp.exp(log_dt)[:, :, None]
    lr, li = dt * ar, dt * ai
    steps = np.arange(S5_L)

    def cexp(tau, x_r, x_i):
        mag = jnp.exp(tau * x_r)
        return mag * jnp.cos(tau * x_i), mag * jnp.sin(tau * x_i)

    def taus(table):
        return jnp.asarray(np.asarray(table, np.float32))[:, :, None, None, None]

    a1r, a1i = cexp(1.0, lr, li)
    nr, ni = a1r - 1.0, a1i
    den = ar * ar + ai * ai
    qr, qi = (nr * ar + ni * ai) / den, (ni * ar - nr * ai) / den
    bbr = qr[..., None] * br - qi[..., None] * bi
    bbi = qr[..., None] * bi + qi[..., None] * br

    slab = lambda x: x.reshape((2, N_SLAB, S5_GPS) + x.shape[2:])
    over_h = lambda x: jnp.repeat(x, S5_GROUP, axis=-1)
    lr_fp, li_fp = lr.reshape(2, N_SLAB, 1, S5_HALF), li.reshape(2, N_SLAB, 1, S5_HALF)
    lr_fh, li_fh = (over_h(slab(x).transpose(0, 1, 3, 2)) for x in (lr, li))
    cr_fh, ci_fh = (slab(x).transpose(0, 1, 4, 2, 3).reshape(2, N_SLAB, S5_STATE, LANES) for x in (cr, ci))
    bbr_fp, bbi_fp = (slab(x).transpose(0, 1, 4, 2, 3).reshape(2, N_SLAB, S5_GROUP, S5_HALF) for x in (bbr, bbi))
    bbr_fh, bbi_fh = (over_h(slab(x).transpose(0, 1, 4, 3, 2)) for x in (bbr, bbi))

    def c_pow(table):
        pr, pi = cexp(taus(table), lr_fh[:, None], li_fh[:, None])
        return cr_fh[:, None] * pr - ci_fh[:, None] * pi, cr_fh[:, None] * pi + ci_fh[:, None] * pr

    dr, di = c_pow([steps, steps])
    kern = jnp.sum(dr[:, :, :, None] * bbr_fh[:, None] - di[:, :, :, None] * bbi_fh[:, None], axis=4)
    lag_f = steps[None, :] - steps[:, None]
    toep = jnp.stack([kern[0][np.clip(lag_f, 0, S5_L - 1)], kern[1][np.clip(-lag_f, 0, S5_L - 1)]])
    valid = np.stack([lag_f >= 0, lag_f <= 0])[:, :, :, None, None, None]
    tt = jnp.where(valid, toep, 0.0).transpose(0, 3, 1, 4, 2, 5).reshape(2, N_SLAB, S5_L * S5_GROUP, S5_K)
    pr, pi = cexp(taus([S5_L - 1 - steps, steps]), lr_fp[:, None], li_fp[:, None])
    wr = pr * bbr_fp[:, None] - pi * bbi_fp[:, None]
    wi = pr * bbi_fp[:, None] + pi * bbr_fp[:, None]
    wt = jnp.stack([wr, wi], axis=4).transpose(0, 2, 1, 3, 4, 5).reshape(2, N_SLAB, S5_L * S5_GROUP, 2 * S5_HALF)
    vr, vi = c_pow([steps + 1, S5_L - steps])
    vt = jnp.stack([vr, -vi], axis=1).transpose(0, 3, 1, 4, 2, 5).reshape(2, N_SLAB, 2 * S5_STATE, S5_K)
    alr, ali = cexp(float(S5_L), lr, li)
    al = jnp.stack([alr[0], ali[0], alr[1], ali[1]]).reshape(4, N_SLAB, S5_HALF).transpose(1, 0, 2)
    coef = jnp.concatenate([al, jnp.zeros_like(al)], axis=1)
    return tt, wt, vt, coef


def _nm_s5_kernel(x_ref, g_ref, mod_ref, h_ref, u_ref, hs_s, *, nb, tk):
    x = x_ref[...]
    ms = jnp.mean(x * x, axis=-1, keepdims=True)
    y = x * lax.rsqrt(ms + EPS) * g_ref[...]
    h = y * (1.0 + mod_ref[:, 1:2, :]) + mod_ref[:, 0:1, :]
    h_ref[...] = h.astype(h_ref.dtype)
    pitch = tk + S5_PITCH_PAD
    for bi in range(nb):
        for j in range(N_SLAB):
            hs_s[j, bi * pitch:bi * pitch + tk, :] = h[bi, :, j * LANES:(j + 1) * LANES]
    for kk in range(tk // S5_L):
        for s in range(S5_L):
            for j in range(N_SLAB):
                blk = hs_s[j, pl.ds(kk * S5_L + s, nb, stride=pitch), :]
                u_ref[j, kk * nb:(kk + 1) * nb, s * LANES:(s + 1) * LANES] = blk.astype(u_ref.dtype)


def _norm_mod_s5_call(x3, g, mod):
    b, t, _ = x3.shape
    tk = S5_TOK
    rows = (tk // S5_L) * b
    return pl.pallas_call(
        functools.partial(_nm_s5_kernel, nb=b, tk=tk),
        grid=(t // tk,),
        in_specs=[
            pl.BlockSpec((b, tk, D_MODEL), lambda i: (0, i, 0)),
            pl.BlockSpec((1, D_MODEL), lambda i: (0, 0)),
            pl.BlockSpec((b, ADA_CHUNKS, D_MODEL), lambda i: (0, 0, 0)),
        ],
        out_specs=[
            pl.BlockSpec((b, tk, D_MODEL), lambda i: (0, i, 0)),
            pl.BlockSpec((N_SLAB, rows, S5_K), lambda i: (0, i, 0)),
        ],
        out_shape=[
            jax.ShapeDtypeStruct((b, t, D_MODEL), BF16),
            jax.ShapeDtypeStruct((N_SLAB, (t // S5_L) * b, S5_K), BF16),
        ],
        scratch_shapes=[pltpu.VMEM((N_SLAB, b * (tk + S5_PITCH_PAD), LANES), F32)],
        compiler_params=_params(1),
        name="norm_mod_s5",
    )(x3, g.reshape(1, D_MODEL), mod)


def _s5_kernel(uc_ref, ulf_ref, ulb_ref, tt_ref, wt_ref, vt_ref, coef_ref,
               ycf_ref, ycb_ref, ylf_ref, ylb_ref, t_s, w_s, v_s, u_s, s_s, x_s, carry_s, *, nb, nblk):
    i = pl.program_id(1)

    @pl.when(i == 0)
    def _():
        grp_out = (lax.broadcasted_iota(jnp.int32, (1, S5_K), 1) // S5_GROUP) % S5_GPS
        grp_state = (lax.broadcasted_iota(jnp.int32, (1, 2 * S5_HALF), 1) // S5_STATE) % S5_GPS
        rows = S5_GROUP
        for d in range(2):
            for s in range(S5_L):
                tc = tt_ref[d, 0, s * rows:(s + 1) * rows, :]
                wc = wt_ref[d, 0, s * rows:(s + 1) * rows, :]
                for gi in range(S5_GPS):
                    r0 = s * LANES + gi * rows
                    t_s[d, r0:r0 + rows, :] = jnp.where(grp_out == gi, tc, 0.0).astype(BF16)
                    w_s[d, r0:r0 + rows, :] = jnp.where(grp_state == gi, wc, 0.0).astype(BF16)
            for c in range(2):
                vc = vt_ref[d, 0, c * S5_STATE:(c + 1) * S5_STATE, :]
                for gi in range(S5_GPS):
                    r0 = c * S5_HALF + gi * S5_STATE
                    v_s[d, r0:r0 + S5_STATE, :] = jnp.where(grp_out == gi, vc, 0.0).astype(BF16)
        carry_s[...] = jnp.zeros_like(carry_s)
        u_s[0] = uc_ref[0]
        u_s[1] = uc_ref[0]

    @pl.when(i > 0)
    def _():
        u_s[0] = ulf_ref[0]
        u_s[1] = ulb_ref[0]

    for d in range(2):
        s_s[d] = jnp.dot(u_s[d], w_s[d], preferred_element_type=F32)
    ys = [jnp.dot(u_s[d], t_s[d], preferred_element_type=F32) for d in range(2)]
    for d in range(2):
        a_r = coef_ref[0, 2 * d:2 * d + 1, :]
        a_i = coef_ref[0, 2 * d + 1:2 * d + 2, :]
        x_r, x_i = carry_s[d, :, 0:S5_HALF], carry_s[d, :, S5_HALF:2 * S5_HALF]
        for k in range(nblk):
            rows = slice((k if d == 0 else nblk - 1 - k) * nb, (k if d == 0 else nblk - 1 - k) * nb + nb)
            x_s[d, rows, 0:S5_HALF] = x_r
            x_s[d, rows, S5_HALF:2 * S5_HALF] = x_i
            s_r = s_s[d, rows, 0:S5_HALF]
            s_i = s_s[d, rows, S5_HALF:2 * S5_HALF]
            x_r, x_i = a_r * x_r - a_i * x_i + s_r, a_r * x_i + a_i * x_r + s_i
        carry_s[d, :, 0:S5_HALF] = x_r
        carry_s[d, :, S5_HALF:2 * S5_HALF] = x_i
    for d in range(2):
        y = ys[d] + jnp.dot(x_s[d].astype(BF16), v_s[d], preferred_element_type=F32)
        ys[d] = y.astype(BF16)

    @pl.when(i == 0)
    def _():
        ycf_ref[0] = ys[0]
        ycb_ref[0] = ys[1]

    @pl.when(i > 0)
    def _():
        ylf_ref[0] = ys[0]
        ylb_ref[0] = ys[1]


def _s5_call(u_c, u_l, tt, wt, vt, coef, nb):
    rows = u_c.shape[1]
    n_lat = u_l.shape[1] // rows
    nblk = rows // nb
    tile = lambda fn: pl.BlockSpec((1, rows, S5_K), fn)
    wspec = lambda a: pl.BlockSpec((2, 1) + a.shape[2:], lambda j, i: (0, j, 0, 0))
    ctx_map = lambda j, i: (j, 0, 0)
    fwd_map = lambda j, i: (j, jnp.maximum(i - 1, 0), 0)
    bwd_map = lambda j, i: (j, jnp.minimum(n_lat - i, n_lat - 1), 0)
    yc = jax.ShapeDtypeStruct(u_c.shape, BF16)
    yl = jax.ShapeDtypeStruct(u_l.shape, BF16)
    return pl.pallas_call(
        functools.partial(_s5_kernel, nb=nb, nblk=nblk),
        grid=(N_SLAB, n_lat + 1),
        in_specs=[tile(ctx_map), tile(fwd_map), tile(bwd_map), wspec(tt), wspec(wt), wspec(vt),
                  pl.BlockSpec((1,) + coef.shape[1:], lambda j, i: (j, 0, 0))],
        out_specs=[tile(ctx_map), tile(ctx_map), tile(fwd_map), tile(bwd_map)],
        out_shape=[yc, yc, yl, yl],
        scratch_shapes=[
            pltpu.VMEM((2, S5_K, S5_K), BF16),
            pltpu.VMEM((2, S5_K, 2 * S5_HALF), BF16),
            pltpu.VMEM((2, 2 * S5_HALF, S5_K), BF16),
            pltpu.VMEM((2, rows, S5_K), BF16),
            pltpu.VMEM((2, rows, 2 * S5_HALF), F32),
            pltpu.VMEM((2, rows, 2 * S5_HALF), F32),
            pltpu.VMEM((2, nb, 2 * S5_HALF), F32),
        ],
        compiler_params=_params(2),
        name="s5_scan",
    )(u_c, u_l, u_l, tt, wt, vt, coef)


def _s5_out_kernel(x_ref, yf_ref, yb_ref, h_ref, d_ref, w_ref, mod_ref, o_ref, ys_s, *, nb, tk):
    pitch = tk + S5_PITCH_PAD
    for kk in range(tk // S5_L):
        rows = slice(kk * nb, (kk + 1) * nb)
        for s in range(S5_L):
            lanes = slice(s * LANES, (s + 1) * LANES)
            for j in range(N_SLAB):
                blk = yf_ref[j, rows, lanes].astype(F32) + yb_ref[j, rows, lanes].astype(F32)
                ys_s[j, pl.ds(kk * S5_L + s, nb, stride=pitch), :] = blk
    y = jnp.concatenate(
        [jnp.concatenate([ys_s[j, bi * pitch:bi * pitch + tk, :] for bi in range(nb)], axis=0) for j in range(N_SLAB)],
        axis=1)
    y = y + d_ref[...] * h_ref[...].reshape(nb * tk, D_MODEL).astype(F32)
    g = 0.5 * y * (1.0 + jnp.tanh(math.sqrt(2.0 / math.pi) * (y + 0.044715 * (y * y * y))))
    z = jnp.dot(g.astype(BF16), w_ref[...], preferred_element_type=F32)
    out = z[:, :D_MODEL] * _sigmoid(z[:, D_MODEL:])
    o_ref[...] = x_ref[...] + mod_ref[:, 2:3, :] * out.reshape(nb, tk, D_MODEL)


def _s5_out_call(x3, y_f, y_b, h3, d_skip, w_glu, layer, mod):
    b, t, _ = x3.shape
    tk = S5_TOK
    rows = (tk // S5_L) * b
    tile = pl.BlockSpec((b, tk, D_MODEL), lambda i: (0, i, 0))
    ytile = pl.BlockSpec((N_SLAB, rows, S5_K), lambda i: (0, i, 0))
    return pl.pallas_call(
        functools.partial(_s5_out_kernel, nb=b, tk=tk),
        grid=(t // tk,),
        in_specs=[
            tile, ytile, ytile, tile,
            pl.BlockSpec((1, D_MODEL), lambda i: (0, 0)),
            _layer_spec(w_glu, layer),
            pl.BlockSpec((b, ADA_CHUNKS, D_MODEL), lambda i: (0, 0, 0)),
        ],
        out_specs=tile,
        out_shape=jax.ShapeDtypeStruct((b, t, D_MODEL), F32),
        scratch_shapes=[pltpu.VMEM((N_SLAB, b * (tk + S5_PITCH_PAD), LANES), F32)],
        compiler_params=_params(1),
        name="s5_out",
    )(x3, y_f, y_b, h3, d_skip.reshape(1, D_MODEL).astype(F32), w_glu, mod)


def _even_layer(x2, xc2, mod_l, mod_c, b, t, tc, need_ctx, p):
    n_c = b * tc
    z_c = _norm_mod_matmul_call(xc2, p['norm1_g'], mod_c, n_c, p['w_in'], p['layer'])
    z_l = _norm_mod_matmul_call(x2, p['norm1_g'], mod_l, t, p['w_in'], p['layer'], tables=_ret_tables(t))
    q_c, k_c, vt_c = _mla_prep_call(z_c, n_c, p['mla'], *_mla_tables(n_c, False))
    q_l, k_l, vt_l = _mla_prep_call(z_l, t, p['mla'], *_mla_tables(t, True))
    a_l = _attn_call(q_l, [k_c, k_l], [vt_c, vt_l], t, [tc, t])
    zero = jnp.zeros((b, N_PAIR, LANES, LANES), F32)
    of_c, ob_c, s_cf, s_cb = _ret_call(z_c, p['lg'], zero, zero, b, tc)
    of_l, ob_l, _, _ = _ret_call(z_l, p['lg'], s_cf, s_cb, b, t)
    x2 = _mix_out_call(x2, a_l, of_l, ob_l, z_l, p['w_out'], p['layer'], mod_l, t)
    if need_ctx:
        a_c = _attn_call(q_c, [k_c], [vt_c], tc, [tc])
        xc2 = _mix_out_call(xc2, a_c, of_c, ob_c, z_c, p['w_out'], p['layer'], mod_c, n_c)
    return x2, xc2


def _odd_layer(x2, xc2, mod_l, mod_c, b, t, tc, need_ctx, p):
    x3, xc3 = x2.reshape(b, t, D_MODEL), xc2.reshape(b, tc, D_MODEL)
    h_c, u_c = _norm_mod_s5_call(xc3, p['norm1_g'], mod_c)
    h_l, u_l = _norm_mod_s5_call(x3, p['norm1_g'], mod_l)
    tt, wt, vt, coef = p['s5']
    ycf, ycb, ylf, ylb = _s5_call(u_c, u_l, tt, wt, vt, coef, b)
    x2 = _s5_out_call(x3, ylf, ylb, h_l, p['d_skip'], p['w_glu'], p['layer'], mod_l).reshape(b * t, D_MODEL)
    if need_ctx:
        xc2 = _s5_out_call(xc3, ycf, ycb, h_c, p['d_skip'], p['w_glu'], p['layer'], mod_c).reshape(b * tc, D_MODEL)
    return x2, xc2


def kernel(x, c, ctx, c_ctx, ada_w, ada_b, norm1_g, norm2_g, mlp_w1, mlp_w2, w_in, mla_q_norm_g, mla_w_uq, mla_kv_norm_g, mla_w_ukv, mla_qn_g, mla_kn_g, ret_lg_f, ret_lg_b, w_out, s5_a_re_f, s5_a_im_f, s5_b_re_f, s5_b_im_f, s5_c_re_f, s5_c_im_f, s5_log_dt_f, s5_a_re_b, s5_a_im_b, s5_b_re_b, s5_b_im_b, s5_c_re_b, s5_c_im_b, s5_log_dt_b, s5_d, s5_w_glu):
    b, t, _ = x.shape
    tc = ctx.shape[1]
    depth = ada_w.shape[0]
    assert b % 8 == 0 and tc % KEY_CHUNK == 0 and t % tc == 0 and tc % S5_TOK == 0
    rows = -(-(b + 1) // 8) * 8
    cc = jnp.zeros((rows, D_MODEL), F32).at[:b].set(c.astype(F32)).at[b].set(c_ctx.astype(F32))
    mod = _ada_all(cc, ada_w.astype(F32), ada_b.astype(F32))
    x2 = x.reshape(b * t, D_MODEL).astype(F32)
    xc2 = ctx.reshape(b * tc, D_MODEL).astype(F32)
    w1_all, w2_all = mlp_w1.astype(BF16), mlp_w2.astype(BF16)
    w_in_all = jnp.concatenate([w_in[:, :, :MLA_IN], jnp.zeros(w_in.shape[:2] + (MLA_IN_PAD - MLA_IN,), w_in.dtype),
                                w_in[:, :, MLA_IN:]], axis=2).astype(BF16)
    w_out_all, w_glu_all = w_out.astype(BF16), s5_w_glu.astype(BF16)
    mla_all = jax.vmap(_mla_weights)(mla_q_norm_g, mla_w_uq, mla_kv_norm_g, mla_w_ukv, mla_qn_g, mla_kn_g)
    lg_all = jnp.stack([jnp.log1p(-jnp.exp2(ret_lg_f.astype(F32))), jnp.log1p(-jnp.exp2(ret_lg_b.astype(F32)))], axis=1)
    s5_all = jax.vmap(_s5_compact_weights)(
        (s5_a_re_f, s5_a_im_f, s5_b_re_f, s5_b_im_f, s5_c_re_f, s5_c_im_f, s5_log_dt_f),
        (s5_a_re_b, s5_a_im_b, s5_b_re_b, s5_b_im_b, s5_c_re_b, s5_c_im_b, s5_log_dt_b))
    for l in range(depth):
        need_ctx = l < depth - 1
        mod_l = mod[l, :b].reshape(b, ADA_CHUNKS, D_MODEL)
        mod_c = jnp.broadcast_to(mod[l, b].reshape(1, ADA_CHUNKS, D_MODEL), (b, ADA_CHUNKS, D_MODEL))
        if l % 2 == 0:
            e = l // 2
            p = dict(norm1_g=norm1_g[l], layer=e, w_in=w_in_all, mla=[a[e] for a in mla_all], lg=lg_all[e], w_out=w_out_all)
            x2, xc2 = _even_layer(x2, xc2, mod_l, mod_c, b, t, tc, need_ctx, p)
        else:
            o = l // 2
            p = dict(norm1_g=norm1_g[l], layer=o, d_skip=s5_d[o], w_glu=w_glu_all, s5=[a[o] for a in s5_all])
            x2, xc2 = _odd_layer(x2, xc2, mod_l, mod_c, b, t, tc, need_ctx, p)
        x2 = _mlp_call(x2, norm2_g[l], mod_l, w1_all, w2_all, l, t)
        if need_ctx:
            xc2 = _mlp_call(xc2, norm2_g[l], mod_c, w1_all, w2_all, l, b * tc)
    return x2.reshape(b, t, D_MODEL).astype(x.dtype)
```

```python
import functools
import math

import jax
import jax.numpy as jnp
import numpy as np
from jax import lax
from jax.experimental import pallas as pl
from jax.experimental.pallas import tpu as pltpu

F32 = jnp.float32
BF16 = jnp.bfloat16

D_MODEL = 1024
EPS = 1e-6
ADA_CHUNKS = 6
GRID_W = 64
ROPE_BASE = 10000.0
LANES = 128
N_SLAB = D_MODEL // LANES

H_A = 8
Q_LORA = 256
KV_LORA = 128
NOPE_DIM = 64
ROPE_DIM = 32
QK_DIM = NOPE_DIM + ROPE_DIM
V_DIM = 64
HEAD_SLOT = LANES
MLA_IN = Q_LORA + KV_LORA + ROPE_DIM
MLA_IN_PAD = 512
KEY_CHUNK = 256
SUB_ROWS = 64

H_R = 8
DK_R = 64
DV_R = 64
RET_W = H_R * DK_R
N_PAIR = H_R // 2

S5_GROUP = 16
S5_GROUPS = D_MODEL // S5_GROUP
S5_STATE = 64
S5_L = 4
S5_K = S5_L * LANES
S5_GPS = LANES // S5_GROUP
S5_HALF = S5_GPS * S5_STATE
S5_TOK = 32
S5_PITCH_PAD = 8

D_FF = 4 * D_MODEL
MLP_FF_CHUNK = 512
Z_WIDTH = MLA_IN_PAD + 4 * RET_W

VMEM_LIMIT = 56 * 1024 * 1024

_NT = (((1,), (1,)), ((), ()))
_HP = lax.Precision.HIGHEST


def _params(n_grid):
    return pltpu.CompilerParams(
        dimension_semantics=("arbitrary",) * n_grid, vmem_limit_bytes=VMEM_LIMIT)


def _sigmoid(x):
    return 1.0 / (1.0 + jnp.exp(-x))


def _pick_tile(t, pref):
    tile = min(t, pref)
    while t % tile:
        tile //= 2
    return tile


def _ada_kernel(c_ref, w_ref, b_ref, o_ref):
    cc = c_ref[...]
    s = cc * _sigmoid(cc)
    o_ref[0] = jnp.dot(s, w_ref[0], preferred_element_type=F32, precision=_HP) + b_ref[0]


def _ada_all(cc, ada_w, ada_b):
    depth, _, width = ada_w.shape
    r = cc.shape[0]
    tn = 1536
    return pl.pallas_call(
        _ada_kernel,
        grid=(depth, width // tn),
        in_specs=[
            pl.BlockSpec((r, D_MODEL), lambda l, j: (0, 0)),
            pl.BlockSpec((1, D_MODEL, tn), lambda l, j: (l, 0, j)),
            pl.BlockSpec((1, 1, tn), lambda l, j: (l, 0, j)),
        ],
        out_specs=pl.BlockSpec((1, r, tn), lambda l, j: (l, 0, j)),
        out_shape=jax.ShapeDtypeStruct((depth, r, width), F32),
        compiler_params=_params(2),
        name="ada_mod",
    )(cc, ada_w, ada_b.reshape(depth, 1, width))


def _norm_mod(x, g_ref, mod_ref, shift_row, scale_row):
    ms = jnp.mean(x * x, axis=-1, keepdims=True)
    y = x * lax.rsqrt(ms + EPS) * g_ref[...]
    return y * (1.0 + mod_ref[0, scale_row:scale_row + 1, :]) + mod_ref[0, shift_row:shift_row + 1, :]


def _rotate_pairs(x, cos, sin_lo, sin_hi):
    return x * cos + pltpu.roll(x, LANES - DK_R // 2, 1) * sin_lo + pltpu.roll(x, DK_R // 2, 1) * sin_hi


def _nmm_kernel(x_ref, g_ref, mod_ref, w_ref, *rest, rotary):
    o_ref = rest[-1]
    h = _norm_mod(x_ref[...], g_ref, mod_ref, 0, 1).astype(BF16)
    z = jnp.dot(h, w_ref[...], preferred_element_type=F32)
    o_ref[:, 0:MLA_IN_PAD] = z[:, 0:MLA_IN_PAD].astype(o_ref.dtype)
    for blk in range(2 * N_PAIR):
        cols = slice(MLA_IN_PAD + blk * LANES, MLA_IN_PAD + (blk + 1) * LANES)
        v = z[:, cols]
        if rotary:
            v = _rotate_pairs(v, rest[0][...], rest[1][...], rest[2][...])
        if blk >= N_PAIR:
            v = v * (DK_R ** -0.5)
        o_ref[:, cols] = v.astype(o_ref.dtype)
    tail = MLA_IN_PAD + 2 * RET_W
    o_ref[:, tail:] = z[:, tail:].astype(o_ref.dtype)


def _layer_spec(w_all, layer, **kw):
    return pl.BlockSpec((None,) + w_all.shape[1:], lambda i: (layer, 0, 0), **kw)


def _norm_mod_matmul_call(x2, g, mod, t, w, layer, tables=None):
    n = x2.shape[0]
    tm = _pick_tile(t, 1024)
    tpb = t // tm
    n_out = w.shape[-1]
    in_specs = [
        pl.BlockSpec((tm, D_MODEL), lambda i: (i, 0)),
        pl.BlockSpec((1, D_MODEL), lambda i: (0, 0)),
        pl.BlockSpec((1, ADA_CHUNKS, D_MODEL), lambda i: (i // tpb, 0, 0)),
        _layer_spec(w, layer),
    ]
    args = [x2, g.reshape(1, D_MODEL), mod, w]
    if tables is not None:
        in_specs += [pl.BlockSpec((tm, LANES), lambda i: (i % tpb, 0))] * 3
        args += list(tables)
    return pl.pallas_call(
        functools.partial(_nmm_kernel, rotary=tables is not None),
        grid=(n // tm,),
        in_specs=in_specs,
        out_specs=pl.BlockSpec((tm, n_out), lambda i: (i, 0)),
        out_shape=jax.ShapeDtypeStruct((n, n_out), BF16),
        compiler_params=_params(1),
        name="norm_mod_w_in",
    )(*args)


def _mla_prep_kernel(z_ref, gq_ref, gkv_ref, wq_ref, wqs_ref, wkc_ref, wkcs_ref, wkr_ref, wkrs_ref,
                     wvt_ref, gqn_ref, gqns_ref, gkn_ref, gkns_ref, cos_ref, sin_ref,
                     q_ref, k_ref, vt_ref):
    cq = z_ref[:, 0:Q_LORA].astype(F32)
    ckv = z_ref[:, Q_LORA:Q_LORA + KV_LORA].astype(F32)
    kr = z_ref[:, Q_LORA + KV_LORA:MLA_IN_PAD]
    cqn = (cq * lax.rsqrt(jnp.mean(cq * cq, axis=-1, keepdims=True) + EPS) * gq_ref[...]).astype(BF16)
    ckn = (ckv * lax.rsqrt(jnp.mean(ckv * ckv, axis=-1, keepdims=True) + EPS) * gkv_ref[...]).astype(BF16)
    q = jnp.dot(cqn, wq_ref[...], preferred_element_type=F32)
    qs = jnp.dot(cqn, wqs_ref[...], preferred_element_type=F32)
    k = jnp.dot(ckn, wkc_ref[...], preferred_element_type=F32) + jnp.dot(kr, wkr_ref[...], preferred_element_type=F32)
    ks = jnp.dot(ckn, wkcs_ref[...], preferred_element_type=F32) + jnp.dot(kr, wkrs_ref[...], preferred_element_type=F32)
    vt_ref[...] = lax.dot_general(wvt_ref[...], ckn, _NT, preferred_element_type=F32).astype(vt_ref.dtype)
    cos = cos_ref[...]
    sin = sin_ref[...]
    q_scale = QK_DIM ** -0.5 * math.log2(math.e)
    for h in range(H_A):
        sl = slice(h * HEAD_SLOT, (h + 1) * HEAD_SLOT)
        qh = q[:, sl]
        rq = lax.rsqrt(jnp.sum(qh * qh, axis=-1, keepdims=True) * (1.0 / QK_DIM) + EPS)
        q_rot = (qh * gqn_ref[...] * cos + qs[:, sl] * gqns_ref[...] * sin) * (rq * q_scale)
        q_ref[:, sl] = q_rot.astype(q_ref.dtype)
        kh = k[:, sl]
        rk = lax.rsqrt(jnp.sum(kh * kh, axis=-1, keepdims=True) * (1.0 / QK_DIM) + EPS)
        k_rot = (kh * gkn_ref[...] * cos + ks[:, sl] * gkns_ref[...] * sin) * rk
        k_ref[:, sl] = k_rot.astype(k_ref.dtype)


def _mla_prep_call(z, t, wts, cos_t, sin_t):
    n = z.shape[0]
    tm = _pick_tile(t, 512)
    tpb = t // tm
    full = lambda a: pl.BlockSpec(a.shape, lambda i: (0,) * a.ndim)
    in_specs = [pl.BlockSpec((tm, MLA_IN_PAD), lambda i: (i, 0))] + [full(a) for a in wts] + [
        pl.BlockSpec((tm, HEAD_SLOT), lambda i: (i % tpb, 0)),
        pl.BlockSpec((tm, HEAD_SLOT), lambda i: (i % tpb, 0)),
    ]
    hw = H_A * HEAD_SLOT
    vw = H_A * V_DIM
    return pl.pallas_call(
        _mla_prep_kernel,
        grid=(n // tm,),
        in_specs=in_specs,
        out_specs=[
            pl.BlockSpec((tm, hw), lambda i: (i, 0)),
            pl.BlockSpec((tm, hw), lambda i: (i, 0)),
            pl.BlockSpec((vw, tm), lambda i: (0, i)),
        ],
        out_shape=[
            jax.ShapeDtypeStruct((n, hw), BF16),
            jax.ShapeDtypeStruct((n, hw), BF16),
            jax.ShapeDtypeStruct((vw, n), BF16),
        ],
        compiler_params=_params(1),
        name="mla_prep",
    )(z, *wts, cos_t, sin_t)


def _mla_weights(q_norm_g, w_uq, kv_norm_g, w_ukv, qn_g, kn_g):
    pad = HEAD_SLOT - QK_DIM
    half = ROPE_DIM // 2
    perm = jnp.arange(HEAD_SLOT)
    perm = perm.at[NOPE_DIM:NOPE_DIM + half].set(jnp.arange(NOPE_DIM + half, NOPE_DIM + ROPE_DIM))
    perm = perm.at[NOPE_DIM + half:NOPE_DIM + ROPE_DIM].set(jnp.arange(NOPE_DIM, NOPE_DIM + half))

    def slots(w):
        wp = jnp.pad(w, ((0, 0), (0, 0), (0, pad)))
        return wp, wp[:, :, perm]

    def flat(w):
        return w.reshape(w.shape[0], H_A * HEAD_SLOT).astype(BF16)

    wq, wqs = slots(w_uq.reshape(Q_LORA, H_A, QK_DIM))
    w_kv = w_ukv.reshape(KV_LORA, H_A, NOPE_DIM + V_DIM)
    wkc, wkcs = slots(jnp.pad(w_kv[:, :, :NOPE_DIM], ((0, 0), (0, 0), (0, ROPE_DIM))))
    eye = jnp.zeros((HEAD_SLOT, QK_DIM), F32).at[jnp.arange(ROPE_DIM), NOPE_DIM + jnp.arange(ROPE_DIM)].set(1.0)
    wkr, wkrs = slots(jnp.broadcast_to(eye[:, None, :], (HEAD_SLOT, H_A, QK_DIM)))
    wvt = w_kv[:, :, NOPE_DIM:].reshape(KV_LORA, H_A * V_DIM).T.astype(BF16)

    def gains(g):
        gp = jnp.pad(g.astype(F32), (0, pad))
        return gp.reshape(1, HEAD_SLOT), gp[perm].reshape(1, HEAD_SLOT)

    gqn, gqns = gains(qn_g)
    gkn, gkns = gains(kn_g)
    return [q_norm_g.reshape(1, Q_LORA).astype(F32), kv_norm_g.reshape(1, KV_LORA).astype(F32),
            flat(wq), flat(wqs), flat(wkc), flat(wkcs), flat(wkr), flat(wkrs), wvt, gqn, gqns, gkn, gkns]


def _mla_tables(n_tok, rotary):
    cos = jnp.ones((n_tok, HEAD_SLOT), F32)
    sin = jnp.zeros((n_tok, HEAD_SLOT), F32)
    if rotary:
        rows = n_tok // GRID_W
        r = jnp.repeat(jnp.arange(rows, dtype=F32), GRID_W)
        col = jnp.tile(jnp.arange(GRID_W, dtype=F32), rows)
        nf = ROPE_DIM // 4
        f = ROPE_BASE ** (-jnp.arange(nf, dtype=F32) / nf)
        ang = jnp.concatenate([r[:, None] * f, col[:, None] * f], axis=-1)
        c, s = jnp.cos(ang), jnp.sin(ang)
        half = ROPE_DIM // 2
        cos = cos.at[:, NOPE_DIM:NOPE_DIM + half].set(c).at[:, NOPE_DIM + half:NOPE_DIM + ROPE_DIM].set(c)
        sin = sin.at[:, NOPE_DIM:NOPE_DIM + half].set(-s).at[:, NOPE_DIM + half:NOPE_DIM + ROPE_DIM].set(s)
    return cos, sin


def _attn_kernel(*refs, seg_lens):
    nseg = len(seg_lens)
    q_ref = refs[0]
    k_refs = refs[1:1 + nseg]
    vt_refs = refs[1 + nseg:1 + 2 * nseg]
    o_ref = refs[1 + 2 * nseg]
    s_s = refs[2 + 2 * nseg]
    chunks = [(g, r) for g, n in enumerate(seg_lens) for r in range(0, n, KEY_CHUNK)]
    tq = q_ref.shape[0]

    def fold(x, op):
        n = x.shape[0] // 8
        x = x.reshape(n, 8, tq)
        out = x[0]
        for i in range(1, n):
            out = op(out, x[i])
        return out

    def scores(h):
        sl = slice(h * HEAD_SLOT, (h + 1) * HEAD_SLOT)
        qh = q_ref[:, sl]
        mx = None
        for c, (g, r) in enumerate(chunks):
            s = lax.dot_general(k_refs[g][r:r + KEY_CHUNK, sl], qh, _NT, preferred_element_type=F32)
            s_s[h % 2, c] = s
            cm = fold(s, jnp.maximum)
            mx = cm if mx is None else jnp.maximum(mx, cm)
        return jnp.max(mx, axis=0, keepdims=True)

    def weighted_values(h, m):
        rows = slice(h * V_DIM, (h + 1) * V_DIM)
        lsum = None
        o_t = None
        for c, (g, r) in enumerate(chunks):
            pieces = []
            for r0 in range(0, KEY_CHUNK, SUB_ROWS):
                p = jnp.exp2(s_s[h % 2, c, r0:r0 + SUB_ROWS, :] - m)
                ps = fold(p, jnp.add)
                lsum = ps if lsum is None else lsum + ps
                pieces.append(p.astype(BF16))
            pb = jnp.concatenate(pieces, axis=0)
            t = jnp.dot(vt_refs[g][rows, r:r + KEY_CHUNK], pb, preferred_element_type=F32)
            o_t = t if o_t is None else o_t + t
        return o_t * (1.0 / jnp.sum(lsum, axis=0, keepdims=True))

    outs = []
    m_next = scores(0)
    for h in range(H_A):
        m_cur = m_next
        if h + 1 < H_A:
            m_next = scores(h + 1)
        outs.append(weighted_values(h, m_cur))
        if h % 2 == 1:
            j = h // 2
            o_ref[:, j * LANES:(j + 1) * LANES] = jnp.concatenate(outs[-2:], axis=0).T.astype(o_ref.dtype)


def _attn_call(q, ks, vts, t, tks):
    n = q.shape[0]
    tq = _pick_tile(t, 256)
    tpb = t // tq
    hw = H_A * HEAD_SLOT
    vw = H_A * V_DIM
    n_chunks = sum(tk // KEY_CHUNK for tk in tks)
    in_specs = [pl.BlockSpec((tq, hw), lambda i: (i, 0))]
    in_specs += [pl.BlockSpec((tk, hw), lambda i: (i // tpb, 0)) for tk in tks]
    in_specs += [pl.BlockSpec((vw, tk), lambda i: (0, i // tpb)) for tk in tks]
    return pl.pallas_call(
        functools.partial(_attn_kernel, seg_lens=tuple(tks)),
        grid=(n // tq,),
        in_specs=in_specs,
        out_specs=pl.BlockSpec((tq, vw), lambda i: (i, 0)),
        out_shape=jax.ShapeDtypeStruct((n, vw), BF16),
        scratch_shapes=[pltpu.VMEM((2, n_chunks, KEY_CHUNK, tq), F32)],
        compiler_params=_params(1),
        name="attention_%dseg" % len(tks),
    )(q, *ks, *vts)


def _ret_kernel(lg_ref, qf_ref, kf_ref, vf_ref, qb_ref, kb_ref, vb_ref, s0f_ref, s0b_ref,
                of_ref, ob_ref, sff_ref, sfb_ref, df_s, db_s, sf_s, sb_s, *, chunk):
    b = pl.program_id(0)
    c = pl.program_id(1)
    nc = pl.num_programs(1)

    @pl.when(jnp.logical_and(b == 0, c == 0))
    def _():
        ii = lax.broadcasted_iota(jnp.int32, (chunk, chunk), 0)
        jj = lax.broadcasted_iota(jnp.int32, (chunk, chunk), 1)
        diff = (ii - jj).astype(F32)
        for h in range(H_R):
            df_s[h] = jnp.where(ii >= jj, jnp.exp(jnp.where(ii >= jj, diff, 0.0) * lg_ref[0, h]), 0.0)
            db_s[h] = jnp.where(jj > ii, jnp.exp(jnp.where(jj > ii, -diff, 0.0) * lg_ref[1, h]), 0.0)

    @pl.when(c == 0)
    def _():
        sf_s[...] = s0f_ref[0]
        sb_s[...] = s0b_ref[0]

    lane = lax.broadcasted_iota(jnp.int32, (1, LANES), 1)
    lo = lane < DK_R
    row = lax.broadcasted_iota(jnp.int32, (LANES, LANES), 0)
    colm = lax.broadcasted_iota(jnp.int32, (LANES, LANES), 1)
    blockdiag = (row < DK_R) == (colm < DK_R)
    pos = lax.broadcasted_iota(jnp.int32, (chunk, 1), 0).astype(F32)
    dirs = ((qf_ref, kf_ref, vf_ref, df_s, sf_s, of_ref), (qb_ref, kb_ref, vb_ref, db_s, sb_s, ob_ref))
    units = [(d, j) for d in range(2) for j in range(N_PAIR)]

    ops = {}
    for d, j in units:
        q_ref, k_ref, v_ref = dirs[d][:3]
        sl = slice(j * LANES, (j + 1) * LANES)
        lg = jnp.where(lo, lg_ref[d, 2 * j], lg_ref[d, 2 * j + 1])
        q2 = q_ref[:, sl].astype(F32)
        k2 = k_ref[:, sl].astype(F32)
        if d == 0:
            q_dec = jnp.exp((pos + 1.0) * lg)
            k_dec = jnp.exp((chunk - 1.0 - pos) * lg)
        else:
            q_dec = jnp.exp((chunk - pos) * lg)
            k_dec = jnp.exp(pos * lg)
        ops[d, j] = dict(
            sl=sl, v2=v_ref[:, sl], c_dec=jnp.exp(chunk * lg), k2b=k2.astype(BF16),
            q_dec=(q2 * q_dec).astype(BF16), kd_t=(k2 * k_dec).T.astype(BF16),
            q_lo=jnp.where(lo, q2, 0.0).astype(BF16), q_hi=jnp.where(lo, 0.0, q2).astype(BF16))
    for d, j in units:
        u = ops[d, j]
        u['s2'] = dirs[d][4][j]
        u['o'] = jnp.dot(u['q_dec'], u['s2'].astype(BF16), preferred_element_type=F32)
        u['sc'] = [lax.dot_general(u[name], u['k2b'], _NT, preferred_element_type=F32) for name in ('q_lo', 'q_hi')]
    for d, j in units:
        u = ops[d, j]
        d_s, o_ref = dirs[d][3], dirs[d][5]
        o = u['o']
        for e in range(2):
            ve = jnp.where(lo if e == 0 else jnp.logical_not(lo), u['v2'], jnp.zeros((), BF16))
            o = o + jnp.dot((u['sc'][e] * d_s[2 * j + e]).astype(BF16), ve, preferred_element_type=F32)
        o_ref[:, u['sl']] = o.astype(o_ref.dtype)
    for d, j in units:
        u = ops[d, j]
        upd = jnp.dot(u['kd_t'], u['v2'], preferred_element_type=F32)
        dirs[d][4][j] = u['s2'] * u['c_dec'] + jnp.where(blockdiag, upd, 0.0)

    @pl.when(c == nc - 1)
    def _():
        sff_ref[0] = sf_s[...]
        sfb_ref[0] = sb_s[...]


def _ret_call(z, lg, s0f, s0b, b, t):
    n = z.shape[0]
    chunk = _pick_tile(t, 256)
    nc = t // chunk
    col0 = MLA_IN_PAD // RET_W

    def zspec(part, rev):
        if rev:
            return pl.BlockSpec((chunk, RET_W), lambda bi, ci: (bi * nc + nc - 1 - ci, col0 + part))
        return pl.BlockSpec((chunk, RET_W), lambda bi, ci: (bi * nc + ci, col0 + part))

    st_spec = pl.BlockSpec((1, N_PAIR, LANES, LANES), lambda bi, ci: (bi, 0, 0, 0))
    in_specs = [pl.BlockSpec(memory_space=pltpu.SMEM)]
    in_specs += [zspec(0, False), zspec(1, False), zspec(2, False)]
    in_specs += [zspec(0, True), zspec(1, True), zspec(2, True)]
    in_specs += [st_spec, st_spec]
    out_specs = [
        pl.BlockSpec((chunk, RET_W), lambda bi, ci: (bi * nc + ci, 0)),
        pl.BlockSpec((chunk, RET_W), lambda bi, ci: (bi * nc + nc - 1 - ci, 0)),
        st_spec, st_spec,
    ]
    st_shape = jax.ShapeDtypeStruct((b, N_PAIR, LANES, LANES), F32)
    return pl.pallas_call(
        functools.partial(_ret_kernel, chunk=chunk),
        grid=(b, nc),
        in_specs=in_specs,
        out_specs=out_specs,
        out_shape=[jax.ShapeDtypeStruct((n, RET_W), BF16), jax.ShapeDtypeStruct((n, RET_W), BF16),
                   st_shape, st_shape],
        scratch_shapes=[
            pltpu.VMEM((H_R, chunk, chunk), F32), pltpu.VMEM((H_R, chunk, chunk), F32),
            pltpu.VMEM((N_PAIR, LANES, LANES), F32), pltpu.VMEM((N_PAIR, LANES, LANES), F32),
        ],
        compiler_params=_params(2),
        name="retention",
    )(lg, z, z, z, z, z, z, s0f, s0b)


def _ret_tables(n_tok):
    nf = DK_R // 2
    theta = ROPE_BASE ** (-jnp.arange(nf, dtype=F32) / nf)
    ang = jnp.arange(n_tok, dtype=F32)[:, None] * theta
    c, s = jnp.cos(ang), jnp.sin(ang)
    z = jnp.zeros_like(s)
    return (jnp.concatenate([c, c, c, c], axis=-1), jnp.concatenate([-s, z, -s, z], axis=-1),
            jnp.concatenate([z, s, z, s], axis=-1))


def _mix_out_kernel(x_ref, a_ref, of_ref, ob_ref, g_ref, w_ref, mod_ref, o_ref):
    lane = lax.broadcasted_iota(jnp.int32, (1, LANES), 1)
    lo = lane < DV_R
    y = jnp.dot(a_ref[...], w_ref[0:H_A * V_DIM, :], preferred_element_type=F32)
    for j in range(N_PAIR):
        sl = slice(j * LANES, (j + 1) * LANES)
        o = of_ref[:, sl].astype(F32) + ob_ref[:, sl].astype(F32)
        s_lo = jnp.sum(jnp.where(lo, o, 0.0), axis=-1, keepdims=True)
        s_all = jnp.sum(o, axis=-1, keepdims=True)
        mu = jnp.where(lo, s_lo, s_all - s_lo) * (1.0 / DV_R)
        oc = o - mu
        q = oc * oc
        q_lo = jnp.sum(jnp.where(lo, q, 0.0), axis=-1, keepdims=True)
        q_all = jnp.sum(q, axis=-1, keepdims=True)
        var = jnp.where(lo, q_lo, q_all - q_lo) * (1.0 / DV_R)
        g = g_ref[:, sl].astype(F32)
        r = (oc * lax.rsqrt(var + EPS)) * (g * _sigmoid(g))
        row0 = H_A * V_DIM + j * LANES
        y = y + jnp.dot(r.astype(BF16), w_ref[row0:row0 + LANES, :], preferred_element_type=F32)
    o_ref[...] = x_ref[...] + mod_ref[0, 2:3, :] * y


def _mix_out_call(x2, a, o_f, o_b, z, w_out, layer, mod, t):
    n = x2.shape[0]
    tm = _pick_tile(t, 1024)
    tpb = t // tm
    gate_col = MLA_IN_PAD // RET_W + 3
    return pl.pallas_call(
        _mix_out_kernel,
        grid=(n // tm,),
        in_specs=[
            pl.BlockSpec((tm, D_MODEL), lambda i: (i, 0)),
            pl.BlockSpec((tm, H_A * V_DIM), lambda i: (i, 0)),
            pl.BlockSpec((tm, RET_W), lambda i: (i, 0)),
            pl.BlockSpec((tm, RET_W), lambda i: (i, 0)),
            pl.BlockSpec((tm, RET_W), lambda i: (i, gate_col)),
            _layer_spec(w_out, layer),
            pl.BlockSpec((1, ADA_CHUNKS, D_MODEL), lambda i: (i // tpb, 0, 0)),
        ],
        out_specs=pl.BlockSpec((tm, D_MODEL), lambda i: (i, 0)),
        out_shape=jax.ShapeDtypeStruct((n, D_MODEL), F32),
        compiler_params=_params(1),
        name="mix_out",
    )(x2, a, o_f, o_b, z, w_out, mod)


def _mlp_kernel(x_ref, g_ref, mod_ref, w1_ref, w2_ref, o_ref):
    h = _norm_mod(x_ref[...], g_ref, mod_ref, 3, 4).astype(BF16)
    parts = []
    for k in range(0, D_FF, MLP_FF_CHUNK):
        a = jnp.maximum(jnp.dot(h, w1_ref[:, k:k + MLP_FF_CHUNK], preferred_element_type=F32), 0.0)
        parts.append((a * a).astype(BF16))
    y = jnp.dot(jnp.concatenate(parts, axis=1), w2_ref[...], preferred_element_type=F32)
    o_ref[...] = x_ref[...] + mod_ref[0, 5:6, :] * y


def _mlp_call(x2, g, mod, w1, w2, layer, t):
    n = x2.shape[0]
    tm = _pick_tile(t, 512)
    tpb = t // tm
    once = pl.Buffered(1)
    return pl.pallas_call(
        _mlp_kernel,
        grid=(n // tm,),
        in_specs=[
            pl.BlockSpec((tm, D_MODEL), lambda i: (i, 0)),
            pl.BlockSpec((1, D_MODEL), lambda i: (0, 0)),
            pl.BlockSpec((1, ADA_CHUNKS, D_MODEL), lambda i: (i // tpb, 0, 0)),
            _layer_spec(w1, layer, pipeline_mode=once),
            _layer_spec(w2, layer, pipeline_mode=once),
        ],
        out_specs=pl.BlockSpec((tm, D_MODEL), lambda i: (i, 0)),
        out_shape=jax.ShapeDtypeStruct((n, D_MODEL), F32),
        compiler_params=_params(1),
        name="mlp",
    )(x2, g.reshape(1, D_MODEL), mod, w1, w2)


def _s5_compact_weights(p_f, p_b):
    st = lambda i: jnp.stack([p_f[i].astype(F32), p_b[i].astype(F32)])
    ar, ai, br, bi, cr, ci, log_dt = (st(i) for i in range(7))
    dt = jnp.exp(log_dt)[:, :, None]
    lr, li = dt * ar, dt * ai
    steps = np.arange(S5_L)

    def cexp(tau, x_r, x_i):
        mag = jnp.exp(tau * x_r)
        return mag * jnp.cos(tau * x_i), mag * jnp.sin(tau * x_i)

    def taus(table):
        return jnp.asarray(np.asarray(table, np.float32))[:, :, None, None, None]

    a1r, a1i = cexp(1.0, lr, li)
    nr, ni = a1r - 1.0, a1i
    den = ar * ar + ai * ai
    qr, qi = (nr * ar + ni * ai) / den, (ni * ar - nr * ai) / den
    bbr = qr[..., None] * br - qi[..., None] * bi
    bbi = qr[..., None] * bi + qi[..., None] * br

    slab = lambda x: x.reshape((2, N_SLAB, S5_GPS) + x.shape[2:])
    over_h = lambda x: jnp.repeat(x, S5_GROUP, axis=-1)
    lr_fp, li_fp = lr.reshape(2, N_SLAB, 1, S5_HALF), li.reshape(2, N_SLAB, 1, S5_HALF)
    lr_fh, li_fh = (over_h(slab(x).transpose(0, 1, 3, 2)) for x in (lr, li))
    cr_fh, ci_fh = (slab(x).transpose(0, 1, 4, 2, 3).reshape(2, N_SLAB, S5_STATE, LANES) for x in (cr, ci))
    bbr_fp, bbi_fp = (slab(x).transpose(0, 1, 4, 2, 3).reshape(2, N_SLAB, S5_GROUP, S5_HALF) for x in (bbr, bbi))
    bbr_fh, bbi_fh = (over_h(slab(x).transpose(0, 1, 4, 3, 2)) for x in (bbr, bbi))

    def c_pow(table):
        pr, pi = cexp(taus(table), lr_fh[:, None], li_fh[:, None])
        return cr_fh[:, None] * pr - ci_fh[:, None] * pi, cr_fh[:, None] * pi + ci_fh[:, None] * pr

    dr, di = c_pow([steps, steps])
    kern = jnp.sum(dr[:, :, :, None] * bbr_fh[:, None] - di[:, :, :, None] * bbi_fh[:, None], axis=4)
    lag_f = steps[None, :] - steps[:, None]
    toep = jnp.stack([kern[0][np.clip(lag_f, 0, S5_L - 1)], kern[1][np.clip(-lag_f, 0, S5_L - 1)]])
    valid = np.stack([lag_f >= 0, lag_f <= 0])[:, :, :, None, None, None]
    tt = jnp.where(valid, toep, 0.0).transpose(0, 3, 1, 4, 2, 5).reshape(2, N_SLAB, S5_L * S5_GROUP, S5_K)
    pr, pi = cexp(taus([S5_L - 1 - steps, steps]), lr_fp[:, None], li_fp[:, None])
    wr = pr * bbr_fp[:, None] - pi * bbi_fp[:, None]
    wi = pr * bbi_fp[:, None] + pi * bbr_fp[:, None]
    wt = jnp.stack([wr, wi], axis=4).transpose(0, 2, 1, 3, 4, 5).reshape(2, N_SLAB, S5_L * S5_GROUP, 2 * S5_HALF)
    vr, vi = c_pow([steps + 1, S5_L - steps])
    vt = jnp.stack([vr, -vi], axis=1).transpose(0, 3, 1, 4, 2, 5).reshape(2, N_SLAB, 2 * S5_STATE, S5_K)
    alr, ali = cexp(float(S5_L), lr, li)
    al = jnp.stack([alr[0], ali[0], alr[1], ali[1]]).reshape(4, N_SLAB, S5_HALF).transpose(1, 0, 2)
    coef = jnp.concatenate([al, jnp.zeros_like(al)], axis=1)
    return tt, wt, vt, coef


def _nm_s5_kernel(x_ref, g_ref, mod_ref, h_ref, u_ref, hs_s, *, nb, tk):
    x = x_ref[...]
    ms = jnp.mean(x * x, axis=-1, keepdims=True)
    y = x * lax.rsqrt(ms + EPS) * g_ref[...]
    h = y * (1.0 + mod_ref[:, 1:2, :]) + mod_ref[:, 0:1, :]
    h_ref[...] = h.astype(h_ref.dtype)
    pitch = tk + S5_PITCH_PAD
    for bi in range(nb):
        for j in range(N_SLAB):
            hs_s[j, bi * pitch:bi * pitch + tk, :] = h[bi, :, j * LANES:(j + 1) * LANES]
    for kk in range(tk // S5_L):
        for s in range(S5_L):
            for j in range(N_SLAB):
                blk = hs_s[j, pl.ds(kk * S5_L + s, nb, stride=pitch), :]
                u_ref[j, kk * nb:(kk + 1) * nb, s * LANES:(s + 1) * LANES] = blk.astype(u_ref.dtype)


def _norm_mod_s5_call(x3, g, mod):
    b, t, _ = x3.shape
    tk = S5_TOK
    rows = (tk // S5_L) * b
    return pl.pallas_call(
        functools.partial(_nm_s5_kernel, nb=b, tk=tk),
        grid=(t // tk,),
        in_specs=[
            pl.BlockSpec((b, tk, D_MODEL), lambda i: (0, i, 0)),
            pl.BlockSpec((1, D_MODEL), lambda i: (0, 0)),
            pl.BlockSpec((b, ADA_CHUNKS, D_MODEL), lambda i: (0, 0, 0)),
        ],
        out_specs=[
            pl.BlockSpec((b, tk, D_MODEL), lambda i: (0, i, 0)),
            pl.BlockSpec((N_SLAB, rows, S5_K), lambda i: (0, i, 0)),
        ],
        out_shape=[
            jax.ShapeDtypeStruct((b, t, D_MODEL), BF16),
            jax.ShapeDtypeStruct((N_SLAB, (t // S5_L) * b, S5_K), BF16),
        ],
        scratch_shapes=[pltpu.VMEM((N_SLAB, b * (tk + S5_PITCH_PAD), LANES), F32)],
        compiler_params=_params(1),
        name="norm_mod_s5",
    )(x3, g.reshape(1, D_MODEL), mod)


def _s5_kernel(uc_ref, ulf_ref, ulb_ref, tt_ref, wt_ref, vt_ref, coef_ref,
               ycf_ref, ycb_ref, ylf_ref, ylb_ref, t_s, w_s, v_s, u_s, s_s, x_s, carry_s, *, nb, nblk):
    i = pl.program_id(1)

    @pl.when(i == 0)
    def _():
        grp_out = (lax.broadcasted_iota(jnp.int32, (1, S5_K), 1) // S5_GROUP) % S5_GPS
        grp_state = (lax.broadcasted_iota(jnp.int32, (1, 2 * S5_HALF), 1) // S5_STATE) % S5_GPS
        rows = S5_GROUP
        for d in range(2):
            for s in range(S5_L):
                tc = tt_ref[d, 0, s * rows:(s + 1) * rows, :]
                wc = wt_ref[d, 0, s * rows:(s + 1) * rows, :]
                for gi in range(S5_GPS):
                    r0 = s * LANES + gi * rows
                    t_s[d, r0:r0 + rows, :] = jnp.where(grp_out == gi, tc, 0.0).astype(BF16)
                    w_s[d, r0:r0 + rows, :] = jnp.where(grp_state == gi, wc, 0.0).astype(BF16)
            for c in range(2):
                vc = vt_ref[d, 0, c * S5_STATE:(c + 1) * S5_STATE, :]
                for gi in range(S5_GPS):
                    r0 = c * S5_HALF + gi * S5_STATE
                    v_s[d, r0:r0 + S5_STATE, :] = jnp.where(grp_out == gi, vc, 0.0).astype(BF16)
        carry_s[...] = jnp.zeros_like(carry_s)
        u_s[0] = uc_ref[0]
        u_s[1] = uc_ref[0]

    @pl.when(i > 0)
    def _():
        u_s[0] = ulf_ref[0]
        u_s[1] = ulb_ref[0]

    for d in range(2):
        s_s[d] = jnp.dot(u_s[d], w_s[d], preferred_element_type=F32)
    ys = [jnp.dot(u_s[d], t_s[d], preferred_element_type=F32) for d in range(2)]
    for d in range(2):
        a_r = coef_ref[0, 2 * d:2 * d + 1, :]
        a_i = coef_ref[0, 2 * d + 1:2 * d + 2, :]
        x_r, x_i = carry_s[d, :, 0:S5_HALF], carry_s[d, :, S5_HALF:2 * S5_HALF]
        for k in range(nblk):
            rows = slice((k if d == 0 else nblk - 1 - k) * nb, (k if d == 0 else nblk - 1 - k) * nb + nb)
            x_s[d, rows, 0:S5_HALF] = x_r
            x_s[d, rows, S5_HALF:2 * S5_HALF] = x_i
            s_r = s_s[d, rows, 0:S5_HALF]
            s_i = s_s[d, rows, S5_HALF:2 * S5_HALF]
            x_r, x_i = a_r * x_r - a_i * x_i + s_r, a_r * x_i + a_i * x_r + s_i
        carry_s[d, :, 0:S5_HALF] = x_r
        carry_s[d, :, S5_HALF:2 * S5_HALF] = x_i
    for d in range(2):
        y = ys[d] + jnp.dot(x_s[d].astype(BF16), v_s[d], preferred_element_type=F32)
        ys[d] = y.astype(BF16)

    @pl.when(i == 0)
    def _():
        ycf_ref[0] = ys[0]
        ycb_ref[0] = ys[1]

    @pl.when(i > 0)
    def _():
        ylf_ref[0] = ys[0]
        ylb_ref[0] = ys[1]


def _s5_call(u_c, u_l, tt, wt, vt, coef, nb):
    rows = u_c.shape[1]
    n_lat = u_l.shape[1] // rows
    nblk = rows // nb
    tile = lambda fn: pl.BlockSpec((1, rows, S5_K), fn)
    wspec = lambda a: pl.BlockSpec((2, 1) + a.shape[2:], lambda j, i: (0, j, 0, 0))
    ctx_map = lambda j, i: (j, 0, 0)
    fwd_map = lambda j, i: (j, jnp.maximum(i - 1, 0), 0)
    bwd_map = lambda j, i: (j, jnp.minimum(n_lat - i, n_lat - 1), 0)
    yc = jax.ShapeDtypeStruct(u_c.shape, BF16)
    yl = jax.ShapeDtypeStruct(u_l.shape, BF16)
    return pl.pallas_call(
        functools.partial(_s5_kernel, nb=nb, nblk=nblk),
        grid=(N_SLAB, n_lat + 1),
        in_specs=[tile(ctx_map), tile(fwd_map), tile(bwd_map), wspec(tt), wspec(wt), wspec(vt),
                  pl.BlockSpec((1,) + coef.shape[1:], lambda j, i: (j, 0, 0))],
        out_specs=[tile(ctx_map), tile(ctx_map), tile(fwd_map), tile(bwd_map)],
        out_shape=[yc, yc, yl, yl],
        scratch_shapes=[
            pltpu.VMEM((2, S5_K, S5_K), BF16),
            pltpu.VMEM((2, S5_K, 2 * S5_HALF), BF16),
            pltpu.VMEM((2, 2 * S5_HALF, S5_K), BF16),
            pltpu.VMEM((2, rows, S5_K), BF16),
            pltpu.VMEM((2, rows, 2 * S5_HALF), F32),
            pltpu.VMEM((2, rows, 2 * S5_HALF), F32),
            pltpu.VMEM((2, nb, 2 * S5_HALF), F32),
        ],
        compiler_params=_params(2),
        name="s5_scan",
    )(u_c, u_l, u_l, tt, wt, vt, coef)


def _s5_out_kernel(x_ref, yf_ref, yb_ref, h_ref, d_ref, w_ref, mod_ref, o_ref, ys_s, *, nb, tk):
    pitch = tk + S5_PITCH_PAD
    for kk in range(tk // S5_L):
        rows = slice(kk * nb, (kk + 1) * nb)
        for s in range(S5_L):
            lanes = slice(s * LANES, (s + 1) * LANES)
            for j in range(N_SLAB):
                blk = yf_ref[j, rows, lanes].astype(F32) + yb_ref[j, rows, lanes].astype(F32)
                ys_s[j, pl.ds(kk * S5_L + s, nb, stride=pitch), :] = blk
    y = jnp.concatenate(
        [jnp.concatenate([ys_s[j, bi * pitch:bi * pitch + tk, :] for bi in range(nb)], axis=0) for j in range(N_SLAB)],
        axis=1)
    y = y + d_ref[...] * h_ref[...].reshape(nb * tk, D_MODEL).astype(F32)
    g = 0.5 * y * (1.0 + jnp.tanh(math.sqrt(2.0 / math.pi) * (y + 0.044715 * (y * y * y))))
    z = jnp.dot(g.astype(BF16), w_ref[...], preferred_element_type=F32)
    out = z[:, :D_MODEL] * _sigmoid(z[:, D_MODEL:])
    o_ref[...] = x_ref[...] + mod_ref[:, 2:3, :] * out.reshape(nb, tk, D_MODEL)


def _s5_out_call(x3, y_f, y_b, h3, d_skip, w_glu, layer, mod):
    b, t, _ = x3.shape
    tk = S5_TOK
    rows = (tk // S5_L) * b
    tile = pl.BlockSpec((b, tk, D_MODEL), lambda i: (0, i, 0))
    ytile = pl.BlockSpec((N_SLAB, rows, S5_K), lambda i: (0, i, 0))
    return pl.pallas_call(
        functools.partial(_s5_out_kernel, nb=b, tk=tk),
        grid=(t // tk,),
        in_specs=[
            tile, ytile, ytile, tile,
            pl.BlockSpec((1, D_MODEL), lambda i: (0, 0)),
            _layer_spec(w_glu, layer),
            pl.BlockSpec((b, ADA_CHUNKS, D_MODEL), lambda i: (0, 0, 0)),
        ],
        out_specs=tile,
        out_shape=jax.ShapeDtypeStruct((b, t, D_MODEL), F32),
        scratch_shapes=[pltpu.VMEM((N_SLAB, b * (tk + S5_PITCH_PAD), LANES), F32)],
        compiler_params=_params(1),
        name="s5_out",
    )(x3, y_f, y_b, h3, d_skip.reshape(1, D_MODEL).astype(F32), w_glu, mod)


def _even_layer(x2, xc2, mod_l, mod_c, b, t, tc, need_ctx, p):
    n_c = b * tc
    z_c = _norm_mod_matmul_call(xc2, p['norm1_g'], mod_c, n_c, p['w_in'], p['layer'])
    z_l = _norm_mod_matmul_call(x2, p['norm1_g'], mod_l, t, p['w_in'], p['layer'], tables=_ret_tables(t))
    q_c, k_c, vt_c = _mla_prep_call(z_c, n_c, p['mla'], *_mla_tables(n_c, False))
    q_l, k_l, vt_l = _mla_prep_call(z_l, t, p['mla'], *_mla_tables(t, True))
    a_l = _attn_call(q_l, [k_c, k_l], [vt_c, vt_l], t, [tc, t])
    zero = jnp.zeros((b, N_PAIR, LANES, LANES), F32)
    of_c, ob_c, s_cf, s_cb = _ret_call(z_c, p['lg'], zero, zero, b, tc)
    of_l, ob_l, _, _ = _ret_call(z_l, p['lg'], s_cf, s_cb, b, t)
    x2 = _mix_out_call(x2, a_l, of_l, ob_l, z_l, p['w_out'], p['layer'], mod_l, t)
    if need_ctx:
        a_c = _attn_call(q_c, [k_c], [vt_c], tc, [tc])
        xc2 = _mix_out_call(xc2, a_c, of_c, ob_c, z_c, p['w_out'], p['layer'], mod_c, n_c)
    return x2, xc2


def _odd_layer(x2, xc2, mod_l, mod_c, b, t, tc, need_ctx, p):
    x3, xc3 = x2.reshape(b, t, D_MODEL), xc2.reshape(b, tc, D_MODEL)
    h_c, u_c = _norm_mod_s5_call(xc3, p['norm1_g'], mod_c)
    h_l, u_l = _norm_mod_s5_call(x3, p['norm1_g'], mod_l)
    tt, wt, vt, coef = p['s5']
    ycf, ycb, ylf, ylb = _s5_call(u_c, u_l, tt, wt, vt, coef, b)
    x2 = _s5_out_call(x3, ylf, ylb, h_l, p['d_skip'], p['w_glu'], p['layer'], mod_l).reshape(b * t, D_MODEL)
    if need_ctx:
        xc2 = _s5_out_call(xc3, ycf, ycb, h_c, p['d_skip'], p['w_glu'], p['layer'], mod_c).reshape(b * tc, D_MODEL)
    return x2, xc2


def kernel(x, c, ctx, c_ctx, ada_w, ada_b, norm1_g, norm2_g, mlp_w1, mlp_w2, w_in, mla_q_norm_g, mla_w_uq, mla_kv_norm_g, mla_w_ukv, mla_qn_g, mla_kn_g, ret_lg_f, ret_lg_b, w_out, s5_a_re_f, s5_a_im_f, s5_b_re_f, s5_b_im_f, s5_c_re_f, s5_c_im_f, s5_log_dt_f, s5_a_re_b, s5_a_im_b, s5_b_re_b, s5_b_im_b, s5_c_re_b, s5_c_im_b, s5_log_dt_b, s5_d, s5_w_glu):
    b, t, _ = x.shape
    tc = ctx.shape[1]
    depth = ada_w.shape[0]
    assert b % 8 == 0 and tc % KEY_CHUNK == 0 and t % tc == 0 and tc % S5_TOK == 0
    rows = -(-(b + 1) // 8) * 8
    cc = jnp.zeros((rows, D_MODEL), F32).at[:b].set(c.astype(F32)).at[b].set(c_ctx.astype(F32))
    mod = _ada_all(cc, ada_w.astype(F32), ada_b.astype(F32))
    x2 = x.reshape(b * t, D_MODEL).astype(F32)
    xc2 = ctx.reshape(b * tc, D_MODEL).astype(F32)
    w1_all, w2_all = mlp_w1.astype(BF16), mlp_w2.astype(BF16)
    w_in_all = jnp.concatenate([w_in[:, :, :MLA_IN], jnp.zeros(w_in.shape[:2] + (MLA_IN_PAD - MLA_IN,), w_in.dtype),
                                w_in[:, :, MLA_IN:]], axis=2).astype(BF16)
    w_out_all, w_glu_all = w_out.astype(BF16), s5_w_glu.astype(BF16)
    mla_all = jax.vmap(_mla_weights)(mla_q_norm_g, mla_w_uq, mla_kv_norm_g, mla_w_ukv, mla_qn_g, mla_kn_g)
    lg_all = jnp.stack([jnp.log1p(-jnp.exp2(ret_lg_f.astype(F32))), jnp.log1p(-jnp.exp2(ret_lg_b.astype(F32)))], axis=1)
    s5_all = jax.vmap(_s5_compact_weights)(
        (s5_a_re_f, s5_a_im_f, s5_b_re_f, s5_b_im_f, s5_c_re_f, s5_c_im_f, s5_log_dt_f),
        (s5_a_re_b, s5_a_im_b, s5_b_re_b, s5_b_im_b, s5_c_re_b, s5_c_im_b, s5_log_dt_b))
    for l in range(depth):
        need_ctx = l < depth - 1
        mod_l = mod[l, :b].reshape(b, ADA_CHUNKS, D_MODEL)
        mod_c = jnp.broadcast_to(mod[l, b].reshape(1, ADA_CHUNKS, D_MODEL), (b, ADA_CHUNKS, D_MODEL))
        if l % 2 == 0:
            e = l // 2
            p = dict(norm1_g=norm1_g[l], layer=e, w_in=w_in_all, mla=[a[e] for a in mla_all], lg=lg_all[e], w_out=w_out_all)
            x2, xc2 = _even_layer(x2, xc2, mod_l, mod_c, b, t, tc, need_ctx, p)
        else:
            o = l // 2
            p = dict(norm1_g=norm1_g[l], layer=o, d_skip=s5_d[o], w_glu=w_glu_all, s5=[a[o] for a in s5_all])
            x2, xc2 = _odd_layer(x2, xc2, mod_l, mod_c, b, t, tc, need_ctx, p)
        x2 = _mlp_call(x2, norm2_g[l], mod_l, w1_all, w2_all, l, t)
        if need_ctx:
            xc2 = _mlp_call(xc2, norm2_g[l], mod_c, w1_all, w2_all, l, b * tc)
    return x2.reshape(b, t, D_MODEL).astype(x.dtype)
```
